```python
import math
import jax, jax.numpy as jnp
from jax import lax
import numpy as np

D_MODEL = 2048
BATCH = 2
SEQ = 8192
DEPTH = 4
DEC_BATCH = 32
DEC_SEQ = 32
PAST_LEN = 1024

CHUNK = 64
EPS = 1e-6
LRU_WIDTH = D_MODEL // 2
LRU_HEADS = 16
LRU_HEAD_DIM = LRU_WIDTH // LRU_HEADS
LRU_CONV = 4
LRU_C = 8.0
SCONV_WIDTH = D_MODEL // 2
SCONV_K = 3
MEM_LEN = 256
MEM_HEADS = 4
MEM_WIDTH = D_MODEL // 2
MEM_HEAD_DIM = MEM_WIDTH // MEM_HEADS
N_BRANCH = 3
D_FF = 4 * D_MODEL
IN_SPLITS = [
    LRU_WIDTH,
    2 * LRU_WIDTH,
    2 * LRU_WIDTH + SCONV_WIDTH,
    2 * LRU_WIDTH + 2 * SCONV_WIDTH,
    2 * LRU_WIDTH + 3 * SCONV_WIDTH,
    2 * LRU_WIDTH + 3 * SCONV_WIDTH + MEM_WIDTH,
]
IN_COLS = 2 * LRU_WIDTH + 3 * SCONV_WIDTH + MEM_WIDTH + N_BRANCH * D_MODEL

kernel_name = "hawk_shortconv_memxattn_streaming_encoder_step"


def rms_norm(x, g):
    xf = x.astype(jnp.float32)
    var = jnp.mean(xf * xf, axis=-1, keepdims=True)
    return (xf * lax.rsqrt(var + EPS)).astype(x.dtype) * g


def causal_depthwise_conv(u, buf, w, b=None):
    width = w.shape[0]
    t = u.shape[1]
    up = jnp.concatenate([buf, u], axis=1)
    y = up[:, 0:t] * w[0]
    for k in range(1, width):
        y = y + up[:, k:k + t] * w[k]
    if b is not None:
        y = y + b
    return y, up[:, -(width - 1):]


def block_diag_linear(x, w, b):
    bsz, t, _ = x.shape
    xh = x.reshape(bsz, t, LRU_HEADS, LRU_HEAD_DIM)
    y = jnp.einsum('bthi,hij->bthj', xh, w).reshape(bsz, t, LRU_WIDTH)
    return y + b


def chunked_linear_scan(a, b, h0):
    bsz, t, w = a.shape
    pad = (-t) % CHUNK
    a = jnp.pad(a, ((0, 0), (0, pad), (0, 0)), constant_values=1.0)
    b = jnp.pad(b, ((0, 0), (0, pad), (0, 0)))
    n = (t + pad) // CHUNK
    a = a.reshape(bsz, n, CHUNK, w).transpose(1, 0, 2, 3)
    b = b.reshape(bsz, n, CHUNK, w).transpose(1, 0, 2, 3)

    def combine(left, right):
        a_l, b_l = left
        a_r, b_r = right
        return a_l * a_r, a_r * b_l + b_r

    def step(h, ab):
        a_c, b_c = ab
        a_cum, b_cum = lax.associative_scan(combine, (a_c, b_c), axis=1)
        hs = a_cum * h[:, None, :] + b_cum
        return hs[:, -1], hs

    h_last, hs = lax.scan(step, h0, (a, b))
    hs = hs.transpose(1, 0, 2, 3).reshape(bsz, n * CHUNK, w)[:, :t]
    return hs, h_last


def rg_lru(x, h0, w_a, b_a, w_i, b_i, lam):
    f32 = jnp.float32
    r = jax.nn.sigmoid(block_diag_linear(x, w_a, b_a).astype(f32))
    i = jax.nn.sigmoid(block_diag_linear(x, w_i, b_i).astype(f32))
    log_a = -LRU_C * r * jax.nn.softplus(-lam.astype(f32))
    a = jnp.exp(log_a)
    mult = jnp.sqrt(-jnp.expm1(2.0 * log_a))
    bx = mult * (i * x.astype(f32))
    hs, h_last = chunked_linear_scan(a, bx, h0.astype(f32))
    return hs.astype(x.dtype), h_last.astype(h0.dtype)


def memory_kv(mem, g_mem, w_mem_kv):
    bsz, m, _ = mem.shape
    kv = rms_norm(mem, g_mem) @ w_mem_kv
    k, v = jnp.split(kv, 2, axis=-1)
    return (k.reshape(bsz, m, MEM_HEADS, MEM_HEAD_DIM),
            v.reshape(bsz, m, MEM_HEADS, MEM_HEAD_DIM))


def memory_attention(q, k, v):
    bsz, t, _ = q.shape
    qh = q.reshape(bsz, t, MEM_HEADS, MEM_HEAD_DIM)
    s = jnp.einsum('bthd,bmhd->bhtm', qh, k).astype(jnp.float32) * (MEM_HEAD_DIM ** -0.5)
    p = jax.nn.softmax(s, axis=-1).astype(v.dtype)
    o = jnp.einsum('bhtm,bmhd->bthd', p, v)
    return o.reshape(bsz, t, MEM_WIDTH)


def run_layer(x, mem_k, mem_v, h0, lru_buf, sconv_buf, lw):
    bsz, t, _ = x.shape
    xn = rms_norm(x, lw['g_mix_pre'])
    proj = xn @ lw['w_in']
    lru_x, lru_gate, sc_b, sc_c, sc_h, q, gate_logits = jnp.split(proj, IN_SPLITS, axis=-1)
    u, lru_buf_new = causal_depthwise_conv(lru_x, lru_buf, lw['lru_conv_w'], lw['lru_conv_b'])
    h, h_last = rg_lru(u, h0, lw['lru_w_a'], lw['lru_b_a'], lw['lru_w_i'], lw['lru_b_i'], lw['lru_lambda'])
    y_lru = (h * jax.nn.gelu(lru_gate)) @ lw['w_branch_lru']
    cv, sconv_buf_new = causal_depthwise_conv(sc_c * sc_h, sconv_buf, lw['sconv_w'])
    y_conv = (sc_b * cv) @ lw['w_branch_conv']
    y_mem = memory_attention(q, mem_k, mem_v) @ lw['w_branch_mem']
    gates = jax.nn.sigmoid((gate_logits + lw['b_gate'].reshape(-1)).astype(jnp.float32)).astype(x.dtype)
    gates = gates.reshape(bsz, t, N_BRANCH, D_MODEL)
    merged = gates[:, :, 0] * y_lru + gates[:, :, 1] * y_conv + gates[:, :, 2] * y_mem
    x = x + rms_norm(merged @ lw['w_out'], lw['g_mix_post'])
    hid = jnp.square(jax.nn.relu(rms_norm(x, lw['g_mlp_pre']) @ lw['w_up']))
    x = x + rms_norm(hid @ lw['w_down'], lw['g_mlp_post'])
    return x, h_last, lru_buf_new, sconv_buf_new


def setup_inputs(seed: int = 0) -> dict:
    key = jax.random.key(seed)
    ks = jax.random.split(key, 32)
    f32 = jnp.float32

    def nrm(k, shape, scale):
        return jax.random.normal(k, shape, f32) * scale

    u = jax.random.uniform(ks[0], (DEPTH, LRU_WIDTH), f32, 0.9, 0.999)
    s = u ** (1.0 / LRU_C)
    lru_lambda = jnp.log(s) - jnp.log1p(-s)
    return {
        "x_prompt": nrm(ks[1], (BATCH, SEQ, D_MODEL), 1.0),
        "x_sample": nrm(ks[2], (DEC_BATCH, DEC_SEQ, D_MODEL), 1.0),
        "mem_prompt": nrm(ks[3], (BATCH, MEM_LEN, D_MODEL), 1.0),
        "state_lru_h": nrm(ks[4], (DEPTH, DEC_BATCH, LRU_WIDTH), 0.5),
        "state_lru_conv": nrm(ks[5], (DEPTH, DEC_BATCH, LRU_CONV - 1, LRU_WIDTH), 1.0),
        "state_sconv": nrm(ks[6], (DEPTH, DEC_BATCH, SCONV_K - 1, SCONV_WIDTH), 1.0),
        "cache_mem_k": nrm(ks[7], (DEPTH, DEC_BATCH, MEM_LEN, MEM_HEADS, MEM_HEAD_DIM), 1.0),
        "cache_mem_v": nrm(ks[8], (DEPTH, DEC_BATCH, MEM_LEN, MEM_HEADS, MEM_HEAD_DIM), 1.0),
        "g_mix_pre": 1.0 + nrm(ks[9], (DEPTH, D_MODEL), 0.02),
        "w_in": nrm(ks[10], (DEPTH, D_MODEL, IN_COLS), D_MODEL ** -0.5),
        "b_gate": nrm(ks[11], (DEPTH, N_BRANCH, D_MODEL), 0.01),
        "lru_conv_w": nrm(ks[12], (DEPTH, LRU_CONV, LRU_WIDTH), LRU_CONV ** -0.5),
        "lru_conv_b": nrm(ks[13], (DEPTH, LRU_WIDTH), 0.01),
        "lru_w_a": nrm(ks[14], (DEPTH, LRU_HEADS, LRU_HEAD_DIM, LRU_HEAD_DIM), LRU_HEAD_DIM ** -0.5),
        "lru_b_a": nrm(ks[15], (DEPTH, LRU_WIDTH), 0.01),
        "lru_w_i": nrm(ks[16], (DEPTH, LRU_HEADS, LRU_HEAD_DIM, LRU_HEAD_DIM), LRU_HEAD_DIM ** -0.5),
        "lru_b_i": nrm(ks[17], (DEPTH, LRU_WIDTH), 0.01),
        "lru_lambda": lru_lambda,
        "w_branch_lru": nrm(ks[18], (DEPTH, LRU_WIDTH, D_MODEL), LRU_WIDTH ** -0.5),
        "sconv_w": nrm(ks[19], (DEPTH, SCONV_K, SCONV_WIDTH), SCONV_K ** -0.5),
        "w_branch_conv": nrm(ks[20], (DEPTH, SCONV_WIDTH, D_MODEL), SCONV_WIDTH ** -0.5),
        "g_mem": 1.0 + nrm(ks[21], (DEPTH, D_MODEL), 0.02),
        "w_mem_kv": nrm(ks[22], (DEPTH, D_MODEL, 2 * MEM_WIDTH), D_MODEL ** -0.5),
        "w_branch_mem": nrm(ks[23], (DEPTH, MEM_WIDTH, D_MODEL), MEM_WIDTH ** -0.5),
        "w_out": nrm(ks[24], (DEPTH, D_MODEL, D_MODEL), D_MODEL ** -0.5),
        "g_mix_post": 1.0 + nrm(ks[25], (DEPTH, D_MODEL), 0.02),
        "g_mlp_pre": 1.0 + nrm(ks[26], (DEPTH, D_MODEL), 0.02),
        "w_up": nrm(ks[27], (DEPTH, D_MODEL, D_FF), D_MODEL ** -0.5),
        "w_down": nrm(ks[28], (DEPTH, D_FF, D_MODEL), D_FF ** -0.5),
        "g_mlp_post": 1.0 + nrm(ks[29], (DEPTH, D_MODEL), 0.02),
    }


def reference(x_prompt, x_sample, mem_prompt, state_lru_h, state_lru_conv, state_sconv,
              cache_mem_k, cache_mem_v, g_mix_pre, w_in, b_gate, lru_conv_w, lru_conv_b,
              lru_w_a, lru_b_a, lru_w_i, lru_b_i, lru_lambda, w_branch_lru, sconv_w,
              w_branch_conv, g_mem, w_mem_kv, w_branch_mem, w_out, g_mix_post, g_mlp_pre,
              w_up, w_down, g_mlp_post):
    dt = x_prompt.dtype
    hp = x_prompt
    p_h0 = jnp.zeros((BATCH, LRU_WIDTH), dt)
    p_lb0 = jnp.zeros((BATCH, LRU_CONV - 1, LRU_WIDTH), dt)
    p_sb0 = jnp.zeros((BATCH, SCONV_K - 1, SCONV_WIDTH), dt)
    hs = x_sample
    p_h, p_lb, p_sb, p_mk, p_mv = [], [], [], [], []
    s_h, s_lb, s_sb = [], [], []
    for l in range(DEPTH):
        lw = dict(g_mix_pre=g_mix_pre[l], w_in=w_in[l], b_gate=b_gate[l],
                  lru_conv_w=lru_conv_w[l], lru_conv_b=lru_conv_b[l],
                  lru_w_a=lru_w_a[l], lru_b_a=lru_b_a[l], lru_w_i=lru_w_i[l], lru_b_i=lru_b_i[l],
                  lru_lambda=lru_lambda[l], w_branch_lru=w_branch_lru[l], sconv_w=sconv_w[l],
                  w_branch_conv=w_branch_conv[l], w_branch_mem=w_branch_mem[l], w_out=w_out[l],
                  g_mix_post=g_mix_post[l], g_mlp_pre=g_mlp_pre[l], w_up=w_up[l],
                  w_down=w_down[l], g_mlp_post=g_mlp_post[l])
        mk, mv = memory_kv(mem_prompt, g_mem[l], w_mem_kv[l])
        hp, h_last, lb_new, sb_new = run_layer(hp, mk, mv, p_h0, p_lb0, p_sb0, lw)
        p_h.append(h_last); p_lb.append(lb_new); p_sb.append(sb_new)
        p_mk.append(mk); p_mv.append(mv)
        hs, h_last, lb_new, sb_new = run_layer(hs, cache_mem_k[l], cache_mem_v[l], state_lru_h[l],
                                               state_lru_conv[l], state_sconv[l], lw)
        s_h.append(h_last); s_lb.append(lb_new); s_sb.append(sb_new)
    return (hp, hs,
            jnp.stack(p_h), jnp.stack(p_lb), jnp.stack(p_sb), jnp.stack(p_mk), jnp.stack(p_mv),
            jnp.stack(s_h), jnp.stack(s_lb), jnp.stack(s_sb))
```

```python
import functools

import jax
import jax.numpy as jnp
from jax import lax
from jax.experimental import pallas as pl
from jax.experimental.pallas import tpu as pltpu

F32 = jnp.float32
BF16 = jnp.bfloat16

EPS = 1e-6
LRU_C = 8.0
N_BRANCH = 3

SUBLANES = 8
MXU_DIM = 256
VMEM_LIMIT_BYTES = 56 * 1024 * 1024

PROJ_TM, PROJ_TN = 1024, 1024
MIX_ROWS = 512
ATTN_ROWS = 512
MERGE_TM = 256
MLP_TM, MLP_TF = 512, 1024
PAD_ROWS = SUBLANES


def _tile(n, pref, mult=SUBLANES):
    if n <= pref:
        return n
    for t in range(pref, 0, -1):
        if n % t == 0 and t % mult == 0:
            return t
    raise ValueError(f"no tile for {n} <= {pref}")


def _params(sem):
    return pltpu.CompilerParams(dimension_semantics=sem, vmem_limit_bytes=VMEM_LIMIT_BYTES)


def _rms(x, g):
    var = jnp.mean(x * x, axis=-1, keepdims=True)
    return (x * lax.rsqrt(var + EPS)) * g


def _resident(shape):
    nd = len(shape)
    return pl.BlockSpec(shape, lambda *_: (0,) * nd, pipeline_mode=pl.Buffered(1))


def _proj_kernel(x_ref, g_ref, w_ref, bg_ref, o_ref, xn_ref, *, gelu_blk, gate_blk0):
    j = pl.program_id(1)

    @pl.when(j == 0)
    def _():
        xn_ref[...] = _rms(x_ref[...], g_ref[...]).astype(BF16)

    acc = jnp.dot(xn_ref[...], w_ref[...], preferred_element_type=F32)

    @pl.when(j == gelu_blk)
    def _():
        o_ref[...] = jax.nn.gelu(acc).astype(o_ref.dtype)

    @pl.when(j >= gate_blk0)
    def _():
        o_ref[...] = jax.nn.sigmoid(acc + bg_ref[...]).astype(o_ref.dtype)

    @pl.when(jnp.logical_and(j != gelu_blk, j < gate_blk0))
    def _():
        o_ref[...] = acc.astype(o_ref.dtype)


def _proj(x, g, w, b_gate_row, *, width):
    rows, d = x.shape
    cols = w.shape[1]
    tm, tn = _tile(rows, PROJ_TM), width
    gate_blk0 = (cols - b_gate_row.shape[1]) // tn
    kern = functools.partial(_proj_kernel, gelu_blk=1, gate_blk0=gate_blk0)
    return pl.pallas_call(
        kern,
        grid=(rows // tm, cols // tn),
        in_specs=[
            pl.BlockSpec((tm, d), lambda i, j: (i, 0)),
            pl.BlockSpec((1, d), lambda i, j: (0, 0)),
            pl.BlockSpec((d, tn), lambda i, j: (0, j)),
            pl.BlockSpec((1, tn), lambda i, j: (0, jnp.maximum(j - gate_blk0, 0))),
        ],
        out_specs=pl.BlockSpec((tm, tn), lambda i, j: (i, j)),
        out_shape=jax.ShapeDtypeStruct((rows, cols), BF16),
        scratch_shapes=[pltpu.VMEM((tm, d), BF16)],
        compiler_params=_params(("parallel", "arbitrary")),
        name="proj",
    )(x, g, w, b_gate_row)


def _memkv_kernel(x_ref, g_ref, w_ref, o_ref, xn_ref):
    @pl.when(pl.program_id(1) == 0)
    def _():
        xn_ref[...] = _rms(x_ref[...], g_ref[...]).astype(BF16)

    o_ref[...] = jnp.dot(xn_ref[...], w_ref[...], preferred_element_type=F32)


def _memkv(mem, g, w, *, width):
    rows, d = mem.shape
    cols = w.shape[1]
    tm, tn = _tile(rows, PROJ_TM), width
    return pl.pallas_call(
        _memkv_kernel,
        grid=(rows // tm, cols // tn),
        in_specs=[
            pl.BlockSpec((tm, d), lambda i, j: (i, 0)),
            pl.BlockSpec((1, d), lambda i, j: (0, 0)),
            pl.BlockSpec((d, tn), lambda i, j: (0, j)),
        ],
        out_specs=pl.BlockSpec((tm, tn), lambda i, j: (i, j)),
        out_shape=jax.ShapeDtypeStruct((rows, cols), F32),
        scratch_shapes=[pltpu.VMEM((tm, d), BF16)],
        compiler_params=_params(("parallel", "arbitrary")),
        name="memkv",
    )(mem, g, w)


def _softplus(x):
    return jnp.maximum(x, 0.0) + jnp.log1p(jnp.exp(-jnp.abs(x)))


def _mix_kernel(lx_ref, gg_ref, sb_ref, sc_ref, sh_ref, h0_ref, lb0_ref, sb0_ref,
                cw_ref, cb_ref, wa_ref, ba_ref, wi_ref, bi_ref, lam_ref, scw_ref,
                zl_ref, zc_ref, hn_ref, lbn_ref, sbn_ref,
                xpad, cpad, acum, bcum, hcar):
    t = pl.program_id(1)
    rows, width = lx_ref.shape
    kl, ks = cw_ref.shape[0], scw_ref.shape[0]
    l0, s0 = PAD_ROWS - (kl - 1), PAD_ROWS - (ks - 1)

    @pl.when(t == 0)
    def _():
        xpad[l0:PAD_ROWS, :] = lb0_ref[0]
        cpad[s0:PAD_ROWS, :] = sb0_ref[0]
        hcar[...] = h0_ref[0]

    xpad[PAD_ROWS:PAD_ROWS + rows, :] = lx_ref[...].astype(F32)
    cpad[PAD_ROWS:PAD_ROWS + rows, :] = sc_ref[...].astype(F32) * sh_ref[...].astype(F32)

    u = xpad[l0:l0 + rows, :] * cw_ref[0:1, :]
    for k in range(1, kl):
        u = u + xpad[l0 + k:l0 + k + rows, :] * cw_ref[k:k + 1, :]
    u = u + cb_ref[...]
    cv = cpad[s0:s0 + rows, :] * scw_ref[0:1, :]
    for k in range(1, ks):
        cv = cv + cpad[s0 + k:s0 + k + rows, :] * scw_ref[k:k + 1, :]
    zc_ref[...] = (sb_ref[...].astype(F32) * cv).astype(zc_ref.dtype)

    lb_new = xpad[PAD_ROWS + rows - (kl - 1):PAD_ROWS + rows, :]
    sb_new = cpad[PAD_ROWS + rows - (ks - 1):PAD_ROWS + rows, :]
    lbn_ref[0] = lb_new
    sbn_ref[0] = sb_new
    xpad[l0:PAD_ROWS, :] = lb_new
    cpad[s0:PAD_ROWS, :] = sb_new

    neg_c_sp = -LRU_C * _softplus(-lam_ref[...])
    groups = rows // SUBLANES
    n_chunks = wa_ref.shape[0]
    bd = width // n_chunks
    for c in range(n_chunks):
        sl = slice(c * bd, (c + 1) * bd)
        uc = u[:, sl]
        ub = uc.astype(BF16)
        ya = jnp.dot(ub, wa_ref[c], preferred_element_type=F32) + ba_ref[:, sl]
        yi = jnp.dot(ub, wi_ref[c], preferred_element_type=F32) + bi_ref[:, sl]
        r = jax.nn.sigmoid(ya)
        ig = jax.nn.sigmoid(yi)
        log_a = neg_c_sp[:, sl] * r
        a = jnp.exp(log_a)
        th = jnp.tanh(log_a)
        mult = jnp.sqrt(-2.0 * th / (1.0 - th))
        b = mult * (ig * uc)
        a3 = a.reshape(groups, SUBLANES, bd)
        b3 = b.reshape(groups, SUBLANES, bd)
        row = lax.broadcasted_iota(jnp.int32, a3.shape, 1)
        d = 1
        while d < SUBLANES:
            a_prev = pltpu.roll(a3, d, axis=1)
            b_prev = pltpu.roll(b3, d, axis=1)
            take = row >= d
            b3 = jnp.where(take, a3 * b_prev + b3, b3)
            a3 = jnp.where(take, a3 * a_prev, a3)
            d *= 2
        acum[:, sl] = a3.reshape(rows, bd)
        bcum[:, sl] = b3.reshape(rows, bd)

    def body(g, h):
        r0 = pl.multiple_of(g * SUBLANES, SUBLANES)
        hg = acum[pl.ds(r0, SUBLANES), :] * h + bcum[pl.ds(r0, SUBLANES), :]
        bcum[pl.ds(r0, SUBLANES), :] = hg
        return hg[SUBLANES - 1:SUBLANES, :]

    h_last = lax.fori_loop(0, groups, body, hcar[...])
    hcar[...] = h_last
    hn_ref[0] = h_last
    zl_ref[...] = (bcum[...] * gg_ref[...].astype(F32)).astype(zl_ref.dtype)


def _mix(proj, h0, lb0, sb0, lw, *, n_streams, seq, width):
    rows = _tile(seq, MIX_ROWS)
    nt = seq // rows
    kl, ks = lw["lru_conv_w"].shape[0], lw["sconv_w"].shape[0]
    nck = lw["wa_bd"].shape[0]

    def col(c):
        return pl.BlockSpec((rows, width), lambda s, t: (s * nt + t, c))

    def per_stream(k):
        return pl.BlockSpec((1, k, width), lambda s, t: (s, 0, 0))

    def row_param(k):
        return pl.BlockSpec((k, width), lambda s, t: (0, 0))

    bd_spec = pl.BlockSpec((nck, width // nck, width // nck), lambda s, t: (0, 0, 0))
    total = n_streams * seq
    return pl.pallas_call(
        _mix_kernel,
        grid=(n_streams, nt),
        in_specs=[col(0), col(1), col(2), col(3), col(4),
                  per_stream(1), per_stream(kl - 1), per_stream(ks - 1),
                  row_param(kl), row_param(1), bd_spec, row_param(1), bd_spec, row_param(1),
                  row_param(1), row_param(ks)],
        out_specs=[pl.BlockSpec((rows, width), lambda s, t: (s * nt + t, 0)),
                   pl.BlockSpec((rows, width), lambda s, t: (s * nt + t, 0)),
                   per_stream(1), per_stream(kl - 1), per_stream(ks - 1)],
        out_shape=[jax.ShapeDtypeStruct((total, width), BF16),
                   jax.ShapeDtypeStruct((total, width), BF16),
                   jax.ShapeDtypeStruct((n_streams, 1, width), F32),
                   jax.ShapeDtypeStruct((n_streams, kl - 1, width), F32),
                   jax.ShapeDtypeStruct((n_streams, ks - 1, width), F32)],
        scratch_shapes=[pltpu.VMEM((PAD_ROWS + rows, width), F32),
                        pltpu.VMEM((PAD_ROWS + rows, width), F32),
                        pltpu.VMEM((rows, width), F32),
                        pltpu.VMEM((rows, width), F32),
                        pltpu.VMEM((1, width), F32)],
        compiler_params=_params(("parallel", "arbitrary")),
        name="mix",
    )(proj, proj, proj, proj, proj, h0, lb0, sb0,
      lw["lru_conv_w"], lw["lru_conv_b"], lw["wa_bd"], lw["lru_b_a"], lw["wi_bd"], lw["lru_b_i"],
      lw["lru_lambda"], lw["sconv_w"])


def _attn_kernel(q_ref, k_ref, v_ref, o_ref, *, heads):
    dh = q_ref.shape[1] // heads
    scale = dh ** -0.5
    for h in range(heads):
        sl = slice(h * dh, (h + 1) * dh)
        kh = k_ref[0, :, sl].astype(BF16)
        vh = v_ref[0, :, sl].astype(BF16)
        s = lax.dot_general(q_ref[:, sl], kh, (((1,), (1,)), ((), ())),
                            preferred_element_type=F32) * scale
        e = jnp.exp(s - jnp.max(s, axis=-1, keepdims=True))
        p = e / jnp.sum(e, axis=-1, keepdims=True)
        o_ref[:, sl] = jnp.dot(p.astype(BF16), vh, preferred_element_type=F32).astype(o_ref.dtype)


def _attn(proj, mem_k, mem_v, *, n_streams, seq, width, q_blk, heads):
    rows = _tile(seq, ATTN_ROWS)
    nt = seq // rows
    mlen = mem_k.shape[1]
    kv_spec = pl.BlockSpec((1, mlen, width), lambda s, t: (s, 0, 0))
    return pl.pallas_call(
        functools.partial(_attn_kernel, heads=heads),
        grid=(n_streams, nt),
        in_specs=[pl.BlockSpec((rows, width), lambda s, t: (s * nt + t, q_blk)), kv_spec, kv_spec],
        out_specs=pl.BlockSpec((rows, width), lambda s, t: (s * nt + t, 0)),
        out_shape=jax.ShapeDtypeStruct((n_streams * seq, width), BF16),
        compiler_params=_params(("parallel", "arbitrary")),
        name="attn",
    )(proj, mem_k, mem_v)


def _merge_kernel(zl_ref, zc_ref, zm_ref, g0_ref, g1_ref, g2_ref, x_ref,
                  wl_ref, wc_ref, wm_ref, wo_ref, gp_ref, o_ref):
    m = g0_ref[...].astype(F32) * jnp.dot(zl_ref[...], wl_ref[...], preferred_element_type=F32)
    m = m + g1_ref[...].astype(F32) * jnp.dot(zc_ref[...], wc_ref[...], preferred_element_type=F32)
    m = m + g2_ref[...].astype(F32) * jnp.dot(zm_ref[...], wm_ref[...], preferred_element_type=F32)
    y = jnp.dot(m.astype(BF16), wo_ref[...], preferred_element_type=F32)
    o_ref[...] = x_ref[...] + _rms(y, gp_ref[...])


def _merge(zl, zc, zm, proj, x, lw, *, width):
    rows, d = x.shape
    tm = _tile(rows, MERGE_TM)
    gate_blk0 = (proj.shape[1] - N_BRANCH * d) // d

    def act(w):
        return pl.BlockSpec((tm, w), lambda i: (i, 0))

    def gate(b):
        return pl.BlockSpec((tm, d), lambda i: (i, gate_blk0 + b))

    return pl.pallas_call(
        _merge_kernel,
        grid=(rows // tm,),
        in_specs=[act(width), act(width), act(width), gate(0), gate(1), gate(2), act(d),
                  _resident((width, d)), _resident((width, d)), _resident((width, d)),
                  _resident((d, d)), _resident((1, d))],
        out_specs=act(d),
        out_shape=jax.ShapeDtypeStruct((rows, d), F32),
        compiler_params=_params(("parallel",)),
        name="merge",
    )(zl, zc, zm, proj, proj, proj, x,
      lw["w_branch_lru"], lw["w_branch_conv"], lw["w_branch_mem"], lw["w_out"], lw["g_mix_post"])


def _mlp_kernel(x_ref, gpre_ref, wu_ref, wd_ref, gpost_ref, o_ref, xn_ref, acc_ref):
    f = pl.program_id(1)

    @pl.when(f == 0)
    def _():
        xn_ref[...] = _rms(x_ref[...], gpre_ref[...]).astype(BF16)

    hid = jnp.dot(xn_ref[...], wu_ref[...], preferred_element_type=F32)
    hid = jnp.square(jnp.maximum(hid, 0.0)).astype(BF16)
    part = jnp.dot(hid, wd_ref[...], preferred_element_type=F32)

    @pl.when(f == 0)
    def _():
        acc_ref[...] = part

    @pl.when(f > 0)
    def _():
        acc_ref[...] += part

    @pl.when(f == pl.num_programs(1) - 1)
    def _():
        o_ref[...] = x_ref[...] + _rms(acc_ref[...], gpost_ref[...])


def _mlp(x, lw):
    rows, d = x.shape
    dff = lw["w_up"].shape[1]
    tm, tf = _tile(rows, MLP_TM), _tile(dff, MLP_TF, 128)
    return pl.pallas_call(
        _mlp_kernel,
        grid=(rows // tm, dff // tf),
        in_specs=[pl.BlockSpec((tm, d), lambda i, f: (i, 0)),
                  pl.BlockSpec((1, d), lambda i, f: (0, 0)),
                  pl.BlockSpec((d, tf), lambda i, f: (0, f)),
                  pl.BlockSpec((tf, d), lambda i, f: (f, 0)),
                  pl.BlockSpec((1, d), lambda i, f: (0, 0))],
        out_specs=pl.BlockSpec((tm, d), lambda i, f: (i, 0)),
        out_shape=jax.ShapeDtypeStruct((rows, d), F32),
        scratch_shapes=[pltpu.VMEM((tm, d), BF16), pltpu.VMEM((tm, d), F32)],
        compiler_params=_params(("parallel", "arbitrary")),
        name="mlp",
    )(x, lw["g_mlp_pre"], lw["w_up"], lw["w_down"], lw["g_mlp_post"])


def _block_diag_pack(w):
    heads, dh, _ = w.shape
    per = max(1, min(heads, MXU_DIM // dh))
    n = heads // per
    wg = w.reshape(n, per, dh, dh)
    eye = jnp.eye(per, dtype=w.dtype)
    out = jnp.einsum("nhij,hg->nhigj", wg, eye)
    return out.reshape(n, per * dh, per * dh).astype(BF16)


def _run_layer(x, mem_k, mem_v, h0, lb0, sb0, lw, *, n_streams, seq, width, heads):
    proj = _proj(x, lw["g_mix_pre"], lw["w_in"], lw["b_gate"], width=width)
    zl, zc, hn, lbn, sbn = _mix(proj, h0, lb0, sb0, lw, n_streams=n_streams, seq=seq, width=width)
    zm = _attn(proj, mem_k, mem_v, n_streams=n_streams, seq=seq, width=width, q_blk=5, heads=heads)
    x = _merge(zl, zc, zm, proj, x, lw, width=width)
    x = _mlp(x, lw)
    return x, hn, lbn, sbn


def kernel(x_prompt, x_sample, mem_prompt, state_lru_h, state_lru_conv, state_sconv, cache_mem_k, cache_mem_v, g_mix_pre, w_in, b_gate, lru_conv_w, lru_conv_b, lru_w_a, lru_b_a, lru_w_i, lru_b_i, lru_lambda, w_branch_lru, sconv_w, w_branch_conv, g_mem, w_mem_kv, w_branch_mem, w_out, g_mix_post, g_mlp_pre, w_up, w_down, g_mlp_post):
    depth = w_in.shape[0]
    bp, sp, d = x_prompt.shape
    bs, ss, _ = x_sample.shape
    width = lru_lambda.shape[1]
    mlen, heads, dh = cache_mem_k.shape[2:]
    kl, ks = lru_conv_w.shape[1], sconv_w.shape[1]

    hp = x_prompt.reshape(bp * sp, d)
    hs = x_sample.reshape(bs * ss, d)
    mem2d = mem_prompt.reshape(bp * mlen, d)
    p_h0 = jnp.zeros((bp, 1, width), F32)
    p_lb0 = jnp.zeros((bp, kl - 1, width), F32)
    p_sb0 = jnp.zeros((bp, ks - 1, width), F32)

    outs = {k: [] for k in ("p_h", "p_lb", "p_sb", "p_mk", "p_mv", "s_h", "s_lb", "s_sb")}
    for l in range(depth):
        lw = dict(
            g_mix_pre=g_mix_pre[l].reshape(1, d), w_in=w_in[l].astype(BF16),
            b_gate=b_gate[l].reshape(1, N_BRANCH * d),
            lru_conv_w=lru_conv_w[l], lru_conv_b=lru_conv_b[l].reshape(1, width),
            wa_bd=_block_diag_pack(lru_w_a[l]), lru_b_a=lru_b_a[l].reshape(1, width),
            wi_bd=_block_diag_pack(lru_w_i[l]), lru_b_i=lru_b_i[l].reshape(1, width),
            lru_lambda=lru_lambda[l].reshape(1, width), sconv_w=sconv_w[l],
            w_branch_lru=w_branch_lru[l].astype(BF16), w_branch_conv=w_branch_conv[l].astype(BF16),
            w_branch_mem=w_branch_mem[l].astype(BF16), w_out=w_out[l].astype(BF16),
            g_mix_post=g_mix_post[l].reshape(1, d), g_mlp_pre=g_mlp_pre[l].reshape(1, d),
            w_up=w_up[l].astype(BF16), w_down=w_down[l].astype(BF16),
            g_mlp_post=g_mlp_post[l].reshape(1, d))
        kv = _memkv(mem2d, g_mem[l].reshape(1, d), w_mem_kv[l].astype(BF16), width=width)
        mk = kv[:, :width].reshape(bp, mlen, width)
        mv = kv[:, width:].reshape(bp, mlen, width)
        hp, hn, lbn, sbn = _run_layer(hp, mk, mv, p_h0, p_lb0, p_sb0, lw,
                                      n_streams=bp, seq=sp, width=width, heads=heads)
        outs["p_h"].append(hn.reshape(bp, width))
        outs["p_lb"].append(lbn)
        outs["p_sb"].append(sbn)
        outs["p_mk"].append(mk.reshape(bp, mlen, heads, dh))
        outs["p_mv"].append(mv.reshape(bp, mlen, heads, dh))
        hs, hn, lbn, sbn = _run_layer(
            hs, cache_mem_k[l].reshape(bs, mlen, width), cache_mem_v[l].reshape(bs, mlen, width),
            state_lru_h[l].reshape(bs, 1, width), state_lru_conv[l], state_sconv[l], lw,
            n_streams=bs, seq=ss, width=width, heads=heads)
        outs["s_h"].append(hn.reshape(bs, width))
        outs["s_lb"].append(lbn)
        outs["s_sb"].append(sbn)

    return (hp.reshape(bp, sp, d), hs.reshape(bs, ss, d),
            jnp.stack(outs["p_h"]), jnp.stack(outs["p_lb"]), jnp.stack(outs["p_sb"]),
            jnp.stack(outs["p_mk"]), jnp.stack(outs["p_mv"]),
            jnp.stack(outs["s_h"]), jnp.stack(outs["s_lb"]), jnp.stack(outs["s_sb"]))
```

```python
import functools

import jax
import jax.numpy as jnp
from jax import lax
from jax.experimental import pallas as pl
from jax.experimental.pallas import tpu as pltpu

F32 = jnp.float32
BF16 = jnp.bfloat16

EPS = 1e-6
LRU_C = 8.0
N_BRANCH = 3

SUBLANES = 8
PACKED_ROWS = 16
MXU_DIM = 256
VMEM_LIMIT_BYTES = 56 * 1024 * 1024

PROJ_TM = 1024
MIX_ROWS = 512
ATTN_ROWS = 512
ATTN_KV_BYTES = 4 * 1024 * 1024
MERGE_TM = 256
MLP_TM, MLP_TF = 512, 1024
PAD_ROWS = SUBLANES


def _tile(n, pref, mult=SUBLANES):
    if n <= pref:
        return n
    for t in range(pref, 0, -1):
        if n % t == 0 and t % mult == 0:
            return t
    raise ValueError(f"no tile for {n} <= {pref}")


def _params(sem):
    return pltpu.CompilerParams(dimension_semantics=sem, vmem_limit_bytes=VMEM_LIMIT_BYTES)


def _rms(x, g):
    var = jnp.mean(x * x, axis=-1, keepdims=True)
    return (x * lax.rsqrt(var + EPS)) * g


def _sigmoid(x):
    return 0.5 * jnp.tanh(0.5 * x) + 0.5


def _layer_spec(l, shape, pipeline_mode=None):
    nd = len(shape)
    kw = {} if pipeline_mode is None else {"pipeline_mode": pipeline_mode}
    return pl.BlockSpec((None,) + tuple(shape), lambda *_: (l,) + (0,) * nd, **kw)


def _proj_kernel(x_ref, g_ref, w_ref, bg_ref, o_ref, xn_ref, *, gelu_blk, gate_blk0):
    j = pl.program_id(1)

    @pl.when(j == 0)
    def _():
        xn_ref[...] = _rms(x_ref[...], g_ref[...]).astype(BF16)

    def dot():
        return jnp.dot(xn_ref[...], w_ref[...], preferred_element_type=F32)

    @pl.when(j == gelu_blk)
    def _():
        o_ref[...] = jax.nn.gelu(dot()).astype(o_ref.dtype)

    @pl.when(j >= gate_blk0)
    def _():
        o_ref[...] = _sigmoid(dot() + bg_ref[...]).astype(o_ref.dtype)

    @pl.when(jnp.logical_and(j != gelu_blk, j < gate_blk0))
    def _():
        o_ref[...] = dot().astype(o_ref.dtype)


def _proj(x, wts, l, *, width):
    rows, d = x.shape
    cols = wts["w_in"].shape[2]
    tm, tn = _tile(rows, PROJ_TM), width
    gate_blk0 = (cols - N_BRANCH * d) // tn
    kern = functools.partial(_proj_kernel, gelu_blk=1, gate_blk0=gate_blk0)
    return pl.pallas_call(
        kern,
        grid=(rows // tm, cols // tn),
        in_specs=[
            pl.BlockSpec((tm, d), lambda i, j: (i, 0)),
            _layer_spec(l, (1, d)),
            pl.BlockSpec((None, d, tn), lambda i, j: (l, 0, j)),
            pl.BlockSpec((None, 1, tn), lambda i, j: (l, 0, jnp.maximum(j - gate_blk0, 0))),
        ],
        out_specs=pl.BlockSpec((tm, tn), lambda i, j: (i, j)),
        out_shape=jax.ShapeDtypeStruct((rows, cols), BF16),
        scratch_shapes=[pltpu.VMEM((tm, d), BF16)],
        compiler_params=_params(("parallel", "arbitrary")),
        name="proj",
    )(x, wts["g_mix_pre"], wts["w_in"], wts["b_gate"])


def _memkv_kernel(x_ref, g_ref, w_ref, o_ref, xn_ref):
    @pl.when(pl.program_id(1) == 0)
    def _():
        xn_ref[...] = _rms(x_ref[...], g_ref[...]).astype(BF16)

    o_ref[...] = jnp.dot(xn_ref[...], w_ref[...], preferred_element_type=F32)


def _memkv(mem, wts, l, *, width):
    rows, d = mem.shape
    cols = wts["w_mem_kv"].shape[2]
    tm, tn = _tile(rows, PROJ_TM), width
    return pl.pallas_call(
        _memkv_kernel,
        grid=(rows // tm, cols // tn),
        in_specs=[
            pl.BlockSpec((tm, d), lambda i, j: (i, 0)),
            _layer_spec(l, (1, d)),
            pl.BlockSpec((None, d, tn), lambda i, j: (l, 0, j)),
        ],
        out_specs=pl.BlockSpec((None, tm, tn), lambda i, j: (j, i, 0)),
        out_shape=jax.ShapeDtypeStruct((cols // tn, rows, tn), F32),
        scratch_shapes=[pltpu.VMEM((tm, d), BF16)],
        compiler_params=_params(("parallel", "arbitrary")),
        name="memkv",
    )(mem, wts["g_mem"], wts["w_mem_kv"])


def _softplus(x):
    return jnp.maximum(x, 0.0) + jnp.log1p(jnp.exp(-jnp.abs(x)))


def _causal_conv(pad_ref, w_ref, rows):
    taps = w_ref.shape[0]
    xe = pad_ref[...]
    y = None
    for k in range(taps):
        back = taps - 1 - k
        xk = xe if back == 0 else pltpu.roll(xe, back, axis=0)
        term = xk[PAD_ROWS:PAD_ROWS + rows, :] * w_ref[k:k + 1, :]
        y = term if y is None else y + term
    return y


def _mix_kernel(lx_ref, gg_ref, sb_ref, sc_ref, sh_ref, h0_ref, lb0_ref, sb0_ref,
                cw_ref, cb_ref, wa_ref, ba_ref, wi_ref, bi_ref, lam_ref, scw_ref,
                zl_ref, zc_ref, hn_ref, lbn_ref, sbn_ref,
                xpad, cpad, acum, bcum, hcar):
    t = pl.program_id(1)
    rows, width = lx_ref.shape
    kl, ks = cw_ref.shape[0], scw_ref.shape[0]
    l0, s0 = PAD_ROWS - (kl - 1), PAD_ROWS - (ks - 1)

    @pl.when(t == 0)
    def _():
        xpad[0:PAD_ROWS, :] = jnp.zeros((PAD_ROWS, width), F32)
        cpad[0:PAD_ROWS, :] = jnp.zeros((PAD_ROWS, width), F32)
        xpad[l0:PAD_ROWS, :] = lb0_ref[0]
        cpad[s0:PAD_ROWS, :] = sb0_ref[0]
        hcar[...] = h0_ref[0]

    xpad[PAD_ROWS:PAD_ROWS + rows, :] = lx_ref[...].astype(F32)
    cpad[PAD_ROWS:PAD_ROWS + rows, :] = sc_ref[...].astype(F32) * sh_ref[...].astype(F32)

    u = _causal_conv(xpad, cw_ref, rows) + cb_ref[...]
    cv = _causal_conv(cpad, scw_ref, rows)
    zc_ref[...] = (sb_ref[...].astype(F32) * cv).astype(zc_ref.dtype)

    lb_new = xpad[PAD_ROWS + rows - (kl - 1):PAD_ROWS + rows, :]
    sb_new = cpad[PAD_ROWS + rows - (ks - 1):PAD_ROWS + rows, :]
    lbn_ref[0] = lb_new
    sbn_ref[0] = sb_new
    xpad[l0:PAD_ROWS, :] = lb_new
    cpad[s0:PAD_ROWS, :] = sb_new

    neg_c_sp = -LRU_C * _softplus(-lam_ref[...])
    groups = rows // SUBLANES
    n_chunks = wa_ref.shape[0]
    bd = width // n_chunks
    for c in range(n_chunks):
        sl = slice(c * bd, (c + 1) * bd)
        uc = u[:, sl]
        ub = uc.astype(BF16)
        ya = jnp.dot(ub, wa_ref[c], preferred_element_type=F32) + ba_ref[:, sl]
        yi = jnp.dot(ub, wi_ref[c], preferred_element_type=F32) + bi_ref[:, sl]
        r = _sigmoid(ya)
        ig = _sigmoid(yi)
        log_a = neg_c_sp[:, sl] * r
        a = jnp.exp(log_a)
        th = jnp.tanh(log_a)
        mult = jnp.sqrt(-2.0 * th / (1.0 - th))
        b = mult * (ig * uc)
        a3 = a.reshape(groups, SUBLANES, bd)
        b3 = b.reshape(groups, SUBLANES, bd)
        row = lax.broadcasted_iota(jnp.int32, a3.shape, 1)
        d = 1
        while d < SUBLANES:
            a_prev = pltpu.roll(a3, d, axis=1)
            b_prev = pltpu.roll(b3, d, axis=1)
            take = row >= d
            b3 = jnp.where(take, a3 * b_prev + b3, b3)
            a3 = jnp.where(take, a3 * a_prev, a3)
            d *= 2
        acum[:, sl] = a3.reshape(rows, bd)
        bcum[:, sl] = b3.reshape(rows, bd)

    def body(g, h):
        r0 = pl.multiple_of(g * PACKED_ROWS, PACKED_ROWS)
        r1 = pl.multiple_of(r0 + SUBLANES, SUBLANES)
        h1 = acum[pl.ds(r0, SUBLANES), :] * h + bcum[pl.ds(r0, SUBLANES), :]
        h2 = acum[pl.ds(r1, SUBLANES), :] * h1[SUBLANES - 1:SUBLANES, :] + bcum[pl.ds(r1, SUBLANES), :]
        hh = jnp.concatenate([h1, h2], axis=0)
        gate = gg_ref[pl.ds(r0, PACKED_ROWS), :].astype(F32)
        zl_ref[pl.ds(r0, PACKED_ROWS), :] = (hh * gate).astype(zl_ref.dtype)
        return h2[SUBLANES - 1:SUBLANES, :]

    h_last = lax.fori_loop(0, rows // PACKED_ROWS, body, hcar[...])
    hcar[...] = h_last
    hn_ref[0] = h_last


def _mix(proj, h0, lb0, sb0, state_off, wts, l, *, n_streams, seq, width):
    rows = _tile(seq, MIX_ROWS, PACKED_ROWS)
    nt = seq // rows
    kl, ks = wts["lru_conv_w"].shape[1], wts["sconv_w"].shape[1]
    nck = wts["wa_bd"].shape[1]

    def col(c):
        return pl.BlockSpec((rows, width), lambda s, t: (s * nt + t, c))

    def state_in(k):
        return pl.BlockSpec((1, k, width), lambda s, t: (state_off + s, 0, 0))

    def state_out(k):
        return pl.BlockSpec((1, k, width), lambda s, t: (s, 0, 0))

    bd_spec = _layer_spec(l, (nck, width // nck, width // nck))
    total = n_streams * seq
    return pl.pallas_call(
        _mix_kernel,
        grid=(n_streams, nt),
        in_specs=[col(0), col(1), col(2), col(3), col(4),
                  state_in(1), state_in(kl - 1), state_in(ks - 1),
                  _layer_spec(l, (kl, width)), _layer_spec(l, (1, width)),
                  bd_spec, _layer_spec(l, (1, width)), bd_spec, _layer_spec(l, (1, width)),
                  _layer_spec(l, (1, width)), _layer_spec(l, (ks, width))],
        out_specs=[pl.BlockSpec((rows, width), lambda s, t: (s * nt + t, 0)),
                   pl.BlockSpec((rows, width), lambda s, t: (s * nt + t, 0)),
                   state_out(1), state_out(kl - 1), state_out(ks - 1)],
        out_shape=[jax.ShapeDtypeStruct((total, width), BF16),
                   jax.ShapeDtypeStruct((total, width), BF16),
                   jax.ShapeDtypeStruct((n_streams, 1, width), F32),
                   jax.ShapeDtypeStruct((n_streams, kl - 1, width), F32),
                   jax.ShapeDtypeStruct((n_streams, ks - 1, width), F32)],
        scratch_shapes=[pltpu.VMEM((PAD_ROWS + rows, width), F32),
                        pltpu.VMEM((PAD_ROWS + rows, width), F32),
                        pltpu.VMEM((rows, width), F32),
                        pltpu.VMEM((rows, width), F32),
                        pltpu.VMEM((1, width), F32)],
        compiler_params=_params(("parallel", "arbitrary")),
        name="mix",
    )(proj, proj, proj, proj, proj, h0, lb0, sb0,
      wts["lru_conv_w"], wts["lru_conv_b"], wts["wa_bd"], wts["lru_b_a"], wts["wi_bd"], wts["lru_b_i"],
      wts["lru_lambda"], wts["sconv_w"])


def _attn_kernel(q_ref, k_ref, v_ref, o_ref, *, heads, seq_rows):
    dh = q_ref.shape[1] // heads
    scale = dh ** -0.5
    for g in range(k_ref.shape[0]):
        rs = slice(g * seq_rows, (g + 1) * seq_rows)
        for h in range(heads):
            sl = slice(h * dh, (h + 1) * dh)
            kh = k_ref[g, :, sl].astype(BF16)
            vh = v_ref[g, :, sl].astype(BF16)
            s = lax.dot_general(q_ref[rs, sl], kh, (((1,), (1,)), ((), ())),
                                preferred_element_type=F32) * scale
            e = jnp.exp(s - jnp.max(s, axis=-1, keepdims=True))
            p = e / jnp.sum(e, axis=-1, keepdims=True)
            o_ref[rs, sl] = jnp.dot(p.astype(BF16), vh, preferred_element_type=F32).astype(o_ref.dtype)


def _attn(proj, mem_k, mem_v, kv_off, *, n_streams, seq, width, q_blk, heads):
    mlen = mem_k.shape[1]
    rows = _tile(seq, ATTN_ROWS)
    nt = seq // rows
    per_step = 1
    if nt == 1:
        per_step = _tile(n_streams, max(1, ATTN_KV_BYTES // (mlen * width * 4)), 1)
    assert kv_off % per_step == 0
    blk = per_step * rows
    kv_spec = pl.BlockSpec((per_step, mlen, width), lambda s, t: (kv_off // per_step + s, 0, 0))
    return pl.pallas_call(
        functools.partial(_attn_kernel, heads=heads, seq_rows=rows),
        grid=(n_streams // per_step, nt),
        in_specs=[pl.BlockSpec((blk, width), lambda s, t: (s * nt + t, q_blk)), kv_spec, kv_spec],
        out_specs=pl.BlockSpec((blk, width), lambda s, t: (s * nt + t, 0)),
        out_shape=jax.ShapeDtypeStruct((n_streams * seq, width), BF16),
        compiler_params=_params(("parallel", "arbitrary")),
        name="attn",
    )(proj, mem_k, mem_v)


def _merge_kernel(zl_ref, zc_ref, zm_ref, g0_ref, g1_ref, g2_ref, x_ref,
                  wl_ref, wc_ref, wm_ref, wo_ref, gp_ref, o_ref):
    m = g0_ref[...].astype(F32) * jnp.dot(zl_ref[...], wl_ref[...], preferred_element_type=F32)
    m = m + g1_ref[...].astype(F32) * jnp.dot(zc_ref[...], wc_ref[...], preferred_element_type=F32)
    m = m + g2_ref[...].astype(F32) * jnp.dot(zm_ref[...], wm_ref[...], preferred_element_type=F32)
    y = jnp.dot(m.astype(BF16), wo_ref[...], preferred_element_type=F32)
    o_ref[...] = x_ref[...] + _rms(y, gp_ref[...])


def _merge(zl, zc, zm, proj, x, wts, l, *, width):
    rows, d = x.shape
    tm = _tile(rows, MERGE_TM)
    gate_blk0 = (proj.shape[1] - N_BRANCH * d) // d
    once = pl.Buffered(1)

    def act(w):
        return pl.BlockSpec((tm, w), lambda i: (i, 0))

    def gate(b):
        return pl.BlockSpec((tm, d), lambda i: (i, gate_blk0 + b))

    return pl.pallas_call(
        _merge_kernel,
        grid=(rows // tm,),
        in_specs=[act(width), act(width), act(width), gate(0), gate(1), gate(2), act(d),
                  _layer_spec(l, (width, d), once), _layer_spec(l, (width, d), once),
                  _layer_spec(l, (width, d), once), _layer_spec(l, (d, d), once),
                  _layer_spec(l, (1, d), once)],
        out_specs=act(d),
        out_shape=jax.ShapeDtypeStruct((rows, d), F32),
        compiler_params=_params(("parallel",)),
        name="merge",
    )(zl, zc, zm, proj, proj, proj, x,
      wts["w_branch_lru"], wts["w_branch_conv"], wts["w_branch_mem"], wts["w_out"], wts["g_mix_post"])


def _mlp_kernel(x_ref, gpre_ref, wu_ref, wd_ref, gpost_ref, o_ref, xn_ref, acc_ref):
    f = pl.program_id(1)

    @pl.when(f == 0)
    def _():
        xn_ref[...] = _rms(x_ref[...], gpre_ref[...]).astype(BF16)
        acc_ref[...] = jnp.zeros(acc_ref.shape, F32)

    hid = jnp.dot(xn_ref[...], wu_ref[...], preferred_element_type=F32)
    hid = jnp.square(jnp.maximum(hid, 0.0)).astype(BF16)
    acc_ref[...] += jnp.dot(hid, wd_ref[...], preferred_element_type=F32)

    @pl.when(f == pl.num_programs(1) - 1)
    def _():
        o_ref[...] = x_ref[...] + _rms(acc_ref[...], gpost_ref[...])


def _mlp(x, wts, l):
    rows, d = x.shape
    dff = wts["w_up"].shape[2]
    tm, tf = _tile(rows, MLP_TM), _tile(dff, MLP_TF, 128)
    return pl.pallas_call(
        _mlp_kernel,
        grid=(rows // tm, dff // tf),
        in_specs=[pl.BlockSpec((tm, d), lambda i, f: (i, 0)),
                  _layer_spec(l, (1, d)),
                  pl.BlockSpec((None, d, tf), lambda i, f: (l, 0, f)),
                  pl.BlockSpec((None, tf, d), lambda i, f: (l, f, 0)),
                  _layer_spec(l, (1, d))],
        out_specs=pl.BlockSpec((tm, d), lambda i, f: (i, 0)),
        out_shape=jax.ShapeDtypeStruct((rows, d), F32),
        scratch_shapes=[pltpu.VMEM((tm, d), BF16), pltpu.VMEM((tm, d), F32)],
        compiler_params=_params(("parallel", "arbitrary")),
        name="mlp",
    )(x, wts["g_mlp_pre"], wts["w_up"], wts["w_down"], wts["g_mlp_post"])


def _block_diag_pack(w):
    depth, heads, dh, _ = w.shape
    per = max(1, min(heads, MXU_DIM // dh))
    n = heads // per
    wg = w.reshape(depth, n, per, dh, dh)
    eye = jnp.eye(per, dtype=w.dtype)
    out = jnp.einsum("lnhij,hg->lnhigj", wg, eye)
    return out.reshape(depth, n, per * dh, per * dh).astype(BF16)


def _run_layer(x, mem_k, mem_v, kv_off, h0, lb0, sb0, state_off, wts, l, *, n_streams, seq, width, heads):
    proj = _proj(x, wts, l, width=width)
    zl, zc, hn, lbn, sbn = _mix(proj, h0, lb0, sb0, state_off, wts, l,
                                n_streams=n_streams, seq=seq, width=width)
    zm = _attn(proj, mem_k, mem_v, kv_off, n_streams=n_streams, seq=seq, width=width, q_blk=5, heads=heads)
    x = _merge(zl, zc, zm, proj, x, wts, l, width=width)
    x = _mlp(x, wts, l)
    return x, hn, lbn, sbn


def kernel(x_prompt, x_sample, mem_prompt, state_lru_h, state_lru_conv, state_sconv, cache_mem_k, cache_mem_v, g_mix_pre, w_in, b_gate, lru_conv_w, lru_conv_b, lru_w_a, lru_b_a, lru_w_i, lru_b_i, lru_lambda, w_branch_lru, sconv_w, w_branch_conv, g_mem, w_mem_kv, w_branch_mem, w_out, g_mix_post, g_mlp_pre, w_up, w_down, g_mlp_post):
    depth = w_in.shape[0]
    bp, sp, d = x_prompt.shape
    bs, ss, _ = x_sample.shape
    width = lru_lambda.shape[1]
    mlen, heads, dh = cache_mem_k.shape[2:]
    kl, ks = lru_conv_w.shape[1], sconv_w.shape[1]

    def rowvec(p):
        return p.reshape(depth, 1, -1)

    wts = dict(
        g_mix_pre=rowvec(g_mix_pre), w_in=w_in.astype(BF16), b_gate=rowvec(b_gate),
        lru_conv_w=lru_conv_w, lru_conv_b=rowvec(lru_conv_b),
        wa_bd=_block_diag_pack(lru_w_a), lru_b_a=rowvec(lru_b_a),
        wi_bd=_block_diag_pack(lru_w_i), lru_b_i=rowvec(lru_b_i),
        lru_lambda=rowvec(lru_lambda), sconv_w=sconv_w,
        w_branch_lru=w_branch_lru.astype(BF16), w_branch_conv=w_branch_conv.astype(BF16),
        w_branch_mem=w_branch_mem.astype(BF16), w_out=w_out.astype(BF16),
        g_mix_post=rowvec(g_mix_post), g_mlp_pre=rowvec(g_mlp_pre),
        w_up=w_up.astype(BF16), w_down=w_down.astype(BF16), g_mlp_post=rowvec(g_mlp_post),
        g_mem=rowvec(g_mem), w_mem_kv=w_mem_kv.astype(BF16))

    hp = x_prompt.reshape(bp * sp, d)
    hs = x_sample.reshape(bs * ss, d)
    mem2d = mem_prompt.reshape(bp * mlen, d)
    p_h0 = jnp.zeros((bp, 1, width), F32)
    p_lb0 = jnp.zeros((bp, kl - 1, width), F32)
    p_sb0 = jnp.zeros((bp, ks - 1, width), F32)
    s_h0 = state_lru_h.reshape(depth * bs, 1, width)
    s_lb0 = state_lru_conv.reshape(depth * bs, kl - 1, width)
    s_sb0 = state_sconv.reshape(depth * bs, ks - 1, width)
    s_mk = cache_mem_k.reshape(depth * bs, mlen, width)
    s_mv = cache_mem_v.reshape(depth * bs, mlen, width)

    outs = {k: [] for k in ("p_h", "p_lb", "p_sb", "p_mk", "p_mv", "s_h", "s_lb", "s_sb")}
    for l in range(depth):
        kv = _memkv(mem2d, wts, l, width=width)
        mk = kv[0].reshape(bp, mlen, width)
        mv = kv[1].reshape(bp, mlen, width)
        hp, hn, lbn, sbn = _run_layer(hp, mk, mv, 0, p_h0, p_lb0, p_sb0, 0, wts, l,
                                      n_streams=bp, seq=sp, width=width, heads=heads)
        outs["p_h"].append(hn.reshape(bp, width))
        outs["p_lb"].append(lbn)
        outs["p_sb"].append(sbn)
        outs["p_mk"].append(mk.reshape(bp, mlen, heads, dh))
        outs["p_mv"].append(mv.reshape(bp, mlen, heads, dh))
        hs, hn, lbn, sbn = _run_layer(hs, s_mk, s_mv, l * bs, s_h0, s_lb0, s_sb0, l * bs, wts, l,
                                      n_streams=bs, seq=ss, width=width, heads=heads)
        outs["s_h"].append(hn.reshape(bs, width))
        outs["s_lb"].append(lbn)
        outs["s_sb"].append(sbn)

    return (hp.reshape(bp, sp, d), hs.reshape(bs, ss, d),
            jnp.stack(outs["p_h"]), jnp.stack(outs["p_lb"]), jnp.stack(outs["p_sb"]),
            jnp.stack(outs["p_mk"]), jnp.stack(outs["p_mv"]),
            jnp.stack(outs["s_h"]), jnp.stack(outs["s_lb"]), jnp.stack(outs["s_sb"]))
```

```python
import functools

import jax
import jax.numpy as jnp
from jax import lax
from jax.experimental import pallas as pl
from jax.experimental.pallas import tpu as pltpu

F32 = jnp.float32
BF16 = jnp.bfloat16

EPS = 1e-6
LRU_C = 8.0
N_BRANCH = 3

SUBLANES = 8
PACKED_ROWS = 16
MXU_DIM = 256
VMEM_LIMIT_BYTES = 56 * 1024 * 1024

PROJ_TM = 1024
MIX_ROWS = 512
ATTN_ROWS = 512
ATTN_KV_BYTES = 4 * 1024 * 1024
MERGE_TM = 256
MLP_TM, MLP_TF = 512, 1024
PAD_ROWS = SUBLANES


def _tile(n, pref, mult=SUBLANES):
    if n <= pref:
        return n
    for t in range(pref, 0, -1):
        if n % t == 0 and t % mult == 0:
            return t
    raise ValueError(f"no tile for {n} <= {pref}")


def _params(sem):
    return pltpu.CompilerParams(dimension_semantics=sem, vmem_limit_bytes=VMEM_LIMIT_BYTES)


def _rms(x, g):
    var = jnp.mean(x * x, axis=-1, keepdims=True)
    return (x * lax.rsqrt(var + EPS)) * g


def _sigmoid(x):
    return 0.5 * jnp.tanh(0.5 * x) + 0.5


def _layer_spec(l, shape, pipeline_mode=None):
    nd = len(shape)
    kw = {} if pipeline_mode is None else {"pipeline_mode": pipeline_mode}
    return pl.BlockSpec((None,) + tuple(shape), lambda *_: (l,) + (0,) * nd, **kw)


def _proj_kernel(x_ref, g_ref, w_ref, bg_ref, o_ref, xn_ref, *, gelu_blk, gate_blk0):
    j = pl.program_id(1)

    @pl.when(j == 0)
    def _():
        xn_ref[...] = _rms(x_ref[...], g_ref[...]).astype(BF16)

    def dot():
        return jnp.dot(xn_ref[...], w_ref[...], preferred_element_type=F32)

    @pl.when(j == gelu_blk)
    def _():
        o_ref[...] = jax.nn.gelu(dot()).astype(o_ref.dtype)

    @pl.when(j >= gate_blk0)
    def _():
        o_ref[...] = _sigmoid(dot() + bg_ref[...]).astype(o_ref.dtype)

    @pl.when(jnp.logical_and(j != gelu_blk, j < gate_blk0))
    def _():
        o_ref[...] = dot().astype(o_ref.dtype)


def _proj(x, wts, l, *, width):
    rows, d = x.shape
    cols = wts["w_in"].shape[2]
    tm, tn = _tile(rows, PROJ_TM), width
    gate_blk0 = (cols - N_BRANCH * d) // tn
    kern = functools.partial(_proj_kernel, gelu_blk=1, gate_blk0=gate_blk0)
    return pl.pallas_call(
        kern,
        grid=(rows // tm, cols // tn),
        in_specs=[
            pl.BlockSpec((tm, d), lambda i, j: (i, 0)),
            _layer_spec(l, (1, d)),
            pl.BlockSpec((None, d, tn), lambda i, j: (l, 0, j)),
            pl.BlockSpec((None, 1, tn), lambda i, j: (l, 0, jnp.maximum(j - gate_blk0, 0))),
        ],
        out_specs=pl.BlockSpec((tm, tn), lambda i, j: (i, j)),
        out_shape=jax.ShapeDtypeStruct((rows, cols), BF16),
        scratch_shapes=[pltpu.VMEM((tm, d), BF16)],
        compiler_params=_params(("parallel", "arbitrary")),
        name="proj",
    )(x, wts["g_mix_pre"], wts["w_in"], wts["b_gate"])


def _memkv_kernel(x_ref, g_ref, w_ref, o_ref, xn_ref):
    @pl.when(pl.program_id(1) == 0)
    def _():
        xn_ref[...] = _rms(x_ref[...], g_ref[...]).astype(BF16)

    o_ref[...] = jnp.dot(xn_ref[...], w_ref[...], preferred_element_type=F32)


def _memkv(mem, wts, l, *, width):
    rows, d = mem.shape
    cols = wts["w_mem_kv"].shape[2]
    tm, tn = _tile(rows, PROJ_TM), width
    return pl.pallas_call(
        _memkv_kernel,
        grid=(rows // tm, cols // tn),
        in_specs=[
            pl.BlockSpec((tm, d), lambda i, j: (i, 0)),
            _layer_spec(l, (1, d)),
            pl.BlockSpec((None, d, tn), lambda i, j: (l, 0, j)),
        ],
        out_specs=pl.BlockSpec((None, tm, tn), lambda i, j: (j, i, 0)),
        out_shape=jax.ShapeDtypeStruct((cols // tn, rows, tn), F32),
        scratch_shapes=[pltpu.VMEM((tm, d), BF16)],
        compiler_params=_params(("parallel", "arbitrary")),
        name="memkv",
    )(mem, wts["g_mem"], wts["w_mem_kv"])


def _softplus(x):
    return jnp.maximum(x, 0.0) + jnp.log1p(jnp.exp(-jnp.abs(x)))


def _causal_conv(pad_ref, w_ref, rows):
    taps = w_ref.shape[0]
    xe = pad_ref[...]
    y = None
    for k in range(taps):
        back = taps - 1 - k
        xk = xe if back == 0 else pltpu.roll(xe, back, axis=0)
        term = xk[PAD_ROWS:PAD_ROWS + rows, :] * w_ref[k:k + 1, :]
        y = term if y is None else y + term
    return y


def _mix_kernel(lx_ref, gg_ref, sb_ref, sc_ref, sh_ref, h0_ref, lb0_ref, sb0_ref,
                cw_ref, cb_ref, wa_ref, ba_ref, wi_ref, bi_ref, lam_ref, scw_ref,
                zl_ref, zc_ref, hn_ref, lbn_ref, sbn_ref,
                xpad, cpad, acum, bcum, hcar):
    t = pl.program_id(1)
    rows, width = lx_ref.shape
    kl, ks = cw_ref.shape[0], scw_ref.shape[0]
    l0, s0 = PAD_ROWS - (kl - 1), PAD_ROWS - (ks - 1)

    @pl.when(t == 0)
    def _():
        xpad[0:PAD_ROWS, :] = jnp.zeros((PAD_ROWS, width), F32)
        cpad[0:PAD_ROWS, :] = jnp.zeros((PAD_ROWS, width), F32)
        xpad[l0:PAD_ROWS, :] = lb0_ref[0]
        cpad[s0:PAD_ROWS, :] = sb0_ref[0]
        hcar[...] = h0_ref[0]

    xpad[PAD_ROWS:PAD_ROWS + rows, :] = lx_ref[...].astype(F32)
    cpad[PAD_ROWS:PAD_ROWS + rows, :] = sc_ref[...].astype(F32) * sh_ref[...].astype(F32)

    u = _causal_conv(xpad, cw_ref, rows) + cb_ref[...]
    cv = _causal_conv(cpad, scw_ref, rows)
    zc_ref[...] = (sb_ref[...].astype(F32) * cv).astype(zc_ref.dtype)

    lb_new = xpad[PAD_ROWS + rows - (kl - 1):PAD_ROWS + rows, :]
    sb_new = cpad[PAD_ROWS + rows - (ks - 1):PAD_ROWS + rows, :]
    lbn_ref[0] = lb_new
    sbn_ref[0] = sb_new
    xpad[l0:PAD_ROWS, :] = lb_new
    cpad[s0:PAD_ROWS, :] = sb_new

    neg_c_sp = -LRU_C * _softplus(-lam_ref[...])
    groups = rows // SUBLANES
    n_chunks = wa_ref.shape[0]
    bd = width // n_chunks
    for c in range(n_chunks):
        sl = slice(c * bd, (c + 1) * bd)
        uc = u[:, sl]
        ub = uc.astype(BF16)
        ya = jnp.dot(ub, wa_ref[c], preferred_element_type=F32) + ba_ref[:, sl]
        yi = jnp.dot(ub, wi_ref[c], preferred_element_type=F32) + bi_ref[:, sl]
        r = _sigmoid(ya)
        ig = _sigmoid(yi)
        log_a = neg_c_sp[:, sl] * r
        a = jnp.exp(log_a)
        th = jnp.tanh(log_a)
        mult = jnp.sqrt(-2.0 * th) * lax.rsqrt(1.0 - th)
        b = mult * (ig * uc)
        a3 = a.reshape(groups, SUBLANES, bd)
        b3 = b.reshape(groups, SUBLANES, bd)
        row = lax.broadcasted_iota(jnp.int32, a3.shape, 1)
        d = 1
        while d < SUBLANES:
            a_prev = pltpu.roll(a3, d, axis=1)
            b_prev = pltpu.roll(b3, d, axis=1)
            take = row >= d
            b3 = jnp.where(take, a3 * b_prev + b3, b3)
            a3 = jnp.where(take, a3 * a_prev, a3)
            d *= 2
        acum[:, sl] = a3.reshape(rows, bd)
        bcum[:, sl] = b3.reshape(rows, bd)

    def body(g, h):
        r0 = pl.multiple_of(g * PACKED_ROWS, PACKED_ROWS)
        r1 = pl.multiple_of(r0 + SUBLANES, SUBLANES)
        h1 = acum[pl.ds(r0, SUBLANES), :] * h + bcum[pl.ds(r0, SUBLANES), :]
        h2 = acum[pl.ds(r1, SUBLANES), :] * h1[SUBLANES - 1:SUBLANES, :] + bcum[pl.ds(r1, SUBLANES), :]
        hh = jnp.concatenate([h1, h2], axis=0)
        gate = gg_ref[pl.ds(r0, PACKED_ROWS), :].astype(F32)
        zl_ref[pl.ds(r0, PACKED_ROWS), :] = (hh * gate).astype(zl_ref.dtype)
        return h2[SUBLANES - 1:SUBLANES, :]

    h_last = lax.fori_loop(0, rows // PACKED_ROWS, body, hcar[...])
    hcar[...] = h_last
    hn_ref[0] = h_last


def _mix(proj, h0, lb0, sb0, state_off, wts, l, *, n_streams, seq, width):
    rows = _tile(seq, MIX_ROWS, PACKED_ROWS)
    nt = seq // rows
    kl, ks = wts["lru_conv_w"].shape[1], wts["sconv_w"].shape[1]
    nck = wts["wa_bd"].shape[1]

    def col(c):
        return pl.BlockSpec((rows, width), lambda s, t: (s * nt + t, c))

    def state_in(k):
        return pl.BlockSpec((1, k, width), lambda s, t: (state_off + s, 0, 0))

    def state_out(k):
        return pl.BlockSpec((1, k, width), lambda s, t: (s, 0, 0))

    bd_spec = _layer_spec(l, (nck, width // nck, width // nck))
    total = n_streams * seq
    return pl.pallas_call(
        _mix_kernel,
        grid=(n_streams, nt),
        in_specs=[col(0), col(1), col(2), col(3), col(4),
                  state_in(1), state_in(kl - 1), state_in(ks - 1),
                  _layer_spec(l, (kl, width)), _layer_spec(l, (1, width)),
                  bd_spec, _layer_spec(l, (1, width)), bd_spec, _layer_spec(l, (1, width)),
                  _layer_spec(l, (1, width)), _layer_spec(l, (ks, width))],
        out_specs=[pl.BlockSpec((rows, width), lambda s, t: (s * nt + t, 0)),
                   pl.BlockSpec((rows, width), lambda s, t: (s * nt + t, 0)),
                   state_out(1), state_out(kl - 1), state_out(ks - 1)],
        out_shape=[jax.ShapeDtypeStruct((total, width), BF16),
                   jax.ShapeDtypeStruct((total, width), BF16),
                   jax.ShapeDtypeStruct((n_streams, 1, width), F32),
                   jax.ShapeDtypeStruct((n_streams, kl - 1, width), F32),
                   jax.ShapeDtypeStruct((n_streams, ks - 1, width), F32)],
        scratch_shapes=[pltpu.VMEM((PAD_ROWS + rows, width), F32),
                        pltpu.VMEM((PAD_ROWS + rows, width), F32),
                        pltpu.VMEM((rows, width), F32),
                        pltpu.VMEM((rows, width), F32),
                        pltpu.VMEM((1, width), F32)],
        compiler_params=_params(("parallel", "arbitrary")),
        name="mix",
    )(proj, proj, proj, proj, proj, h0, lb0, sb0,
      wts["lru_conv_w"], wts["lru_conv_b"], wts["wa_bd"], wts["lru_b_a"], wts["wi_bd"], wts["lru_b_i"],
      wts["lru_lambda"], wts["sconv_w"])


def _attn_kernel(q_ref, k_ref, v_ref, o_ref, *, heads, seq_rows):
    dh = q_ref.shape[1] // heads
    scale = dh ** -0.5
    if len(k_ref.shape) == 4:
        mh = k_ref.shape[1] * heads
        for g in range(k_ref.shape[0]):
            rs = slice(g * seq_rows, (g + 1) * seq_rows)
            kf = k_ref[g].reshape(mh, dh).astype(BF16)
            vf = v_ref[g].reshape(mh, dh).astype(BF16)
            col_head = lax.broadcasted_iota(jnp.int32, (seq_rows, mh), 1) % heads
            for h in range(heads):
                sl = slice(h * dh, (h + 1) * dh)
                s = lax.dot_general(q_ref[rs, sl], kf, (((1,), (1,)), ((), ())),
                                    preferred_element_type=F32) * scale
                own = col_head == h
                s = jnp.where(own, s, -jnp.inf)
                e = jnp.where(own, jnp.exp(s - jnp.max(s, axis=-1, keepdims=True)), 0.0)
                p = e / jnp.sum(e, axis=-1, keepdims=True)
                o_ref[rs, sl] = jnp.dot(p.astype(BF16), vf, preferred_element_type=F32).astype(o_ref.dtype)
        return
    for g in range(k_ref.shape[0]):
        rs = slice(g * seq_rows, (g + 1) * seq_rows)
        for h in range(heads):
            sl = slice(h * dh, (h + 1) * dh)
            kh = k_ref[g, :, sl].astype(BF16)
            vh = v_ref[g, :, sl].astype(BF16)
            s = lax.dot_general(q_ref[rs, sl], kh, (((1,), (1,)), ((), ())),
                                preferred_element_type=F32) * scale
            e = jnp.exp(s - jnp.max(s, axis=-1, keepdims=True))
            p = e / jnp.sum(e, axis=-1, keepdims=True)
            o_ref[rs, sl] = jnp.dot(p.astype(BF16), vh, preferred_element_type=F32).astype(o_ref.dtype)


def _attn(proj, mem_k, mem_v, kv_off, *, n_streams, seq, width, q_blk, heads):
    mlen = mem_k.shape[1]
    rows = _tile(seq, ATTN_ROWS)
    nt = seq // rows
    per_step = 1
    if nt == 1:
        per_step = _tile(n_streams, max(1, ATTN_KV_BYTES // (mlen * width * 4)), 1)
    assert kv_off % per_step == 0
    blk = per_step * rows
    kv_shape = (per_step,) + mem_k.shape[1:]
    kv_spec = pl.BlockSpec(kv_shape, lambda s, t: (kv_off // per_step + s,) + (0,) * (len(kv_shape) - 1))
    return pl.pallas_call(
        functools.partial(_attn_kernel, heads=heads, seq_rows=rows),
        grid=(n_streams // per_step, nt),
        in_specs=[pl.BlockSpec((blk, width), lambda s, t: (s * nt + t, q_blk)), kv_spec, kv_spec],
        out_specs=pl.BlockSpec((blk, width), lambda s, t: (s * nt + t, 0)),
        out_shape=jax.ShapeDtypeStruct((n_streams * seq, width), BF16),
        compiler_params=_params(("parallel", "arbitrary")),
        name="attn",
    )(proj, mem_k, mem_v)


def _merge_kernel(zl_ref, zc_ref, zm_ref, g0_ref, g1_ref, g2_ref, x_ref,
                  wl_ref, wc_ref, wm_ref, wo_ref, gp_ref, o_ref):
    m = g0_ref[...].astype(F32) * jnp.dot(zl_ref[...], wl_ref[...], preferred_element_type=F32)
    m = m + g1_ref[...].astype(F32) * jnp.dot(zc_ref[...], wc_ref[...], preferred_element_type=F32)
    m = m + g2_ref[...].astype(F32) * jnp.dot(zm_ref[...], wm_ref[...], preferred_element_type=F32)
    y = jnp.dot(m.astype(BF16), wo_ref[...], preferred_element_type=F32)
    o_ref[...] = x_ref[...] + _rms(y, gp_ref[...])


def _merge(zl, zc, zm, proj, x, wts, l, *, width):
    rows, d = x.shape
    tm = _tile(rows, MERGE_TM)
    gate_blk0 = (proj.shape[1] - N_BRANCH * d) // d
    once = pl.Buffered(1)

    def act(w):
        return pl.BlockSpec((tm, w), lambda i: (i, 0))

    def gate(b):
        return pl.BlockSpec((tm, d), lambda i: (i, gate_blk0 + b))

    return pl.pallas_call(
        _merge_kernel,
        grid=(rows // tm,),
        in_specs=[act(width), act(width), act(width), gate(0), gate(1), gate(2), act(d),
                  _layer_spec(l, (width, d), once), _layer_spec(l, (width, d), once),
                  _layer_spec(l, (width, d), once), _layer_spec(l, (d, d), once),
                  _layer_spec(l, (1, d), once)],
        out_specs=act(d),
        out_shape=jax.ShapeDtypeStruct((rows, d), F32),
        compiler_params=_params(("parallel",)),
        name="merge",
    )(zl, zc, zm, proj, proj, proj, x,
      wts["w_branch_lru"], wts["w_branch_conv"], wts["w_branch_mem"], wts["w_out"], wts["g_mix_post"])


def _mlp_kernel(x_ref, gpre_ref, wu_ref, wd_ref, gpost_ref, o_ref, xn_ref, acc_ref):
    f = pl.program_id(1)
    last = pl.num_programs(1) - 1
    tm = x_ref.shape[0]

    def partial_down(rows):
        hid = jnp.dot(xn_ref[rows, :], wu_ref[...], preferred_element_type=F32)
        hid = jnp.square(jnp.maximum(hid, 0.0)).astype(BF16)
        return jnp.dot(hid, wd_ref[...], preferred_element_type=F32)

    @pl.when(f == 0)
    def _():
        xn_ref[...] = _rms(x_ref[...], gpre_ref[...]).astype(BF16)
        acc_ref[...] = partial_down(slice(None))

    @pl.when(jnp.logical_and(f > 0, f < last))
    def _():
        acc_ref[...] += partial_down(slice(None))

    @pl.when(f == last)
    def _():
        for half in (slice(0, tm // 2), slice(tm // 2, tm)):
            y = acc_ref[half, :] + partial_down(half)
            o_ref[half, :] = x_ref[half, :] + _rms(y, gpost_ref[...])


def _mlp(x, wts, l):
    rows, d = x.shape
    dff = wts["w_up"].shape[2]
    tm, tf = _tile(rows, MLP_TM, 2 * SUBLANES), _tile(dff, MLP_TF, 128)
    assert dff // tf >= 2
    return pl.pallas_call(
        _mlp_kernel,
        grid=(rows // tm, dff // tf),
        in_specs=[pl.BlockSpec((tm, d), lambda i, f: (i, 0)),
                  _layer_spec(l, (1, d)),
                  pl.BlockSpec((None, d, tf), lambda i, f: (l, 0, f)),
                  pl.BlockSpec((None, tf, d), lambda i, f: (l, f, 0)),
                  _layer_spec(l, (1, d))],
        out_specs=pl.BlockSpec((tm, d), lambda i, f: (i, 0)),
        out_shape=jax.ShapeDtypeStruct((rows, d), F32),
        scratch_shapes=[pltpu.VMEM((tm, d), BF16), pltpu.VMEM((tm, d), F32)],
        compiler_params=_params(("parallel", "arbitrary")),
        name="mlp",
    )(x, wts["g_mlp_pre"], wts["w_up"], wts["w_down"], wts["g_mlp_post"])


def _block_diag_pack(w):
    depth, heads, dh, _ = w.shape
    per = max(1, min(heads, MXU_DIM // dh))
    n = heads // per
    wg = w.reshape(depth, n, per, dh, dh)
    eye = jnp.eye(per, dtype=w.dtype)
    out = jnp.einsum("lnhij,hg->lnhigj", wg, eye)
    return out.reshape(depth, n, per * dh, per * dh).astype(BF16)


def _run_layer(x, mem_k, mem_v, kv_off, h0, lb0, sb0, state_off, wts, l, *, n_streams, seq, width, heads):
    proj = _proj(x, wts, l, width=width)
    zl, zc, hn, lbn, sbn = _mix(proj, h0, lb0, sb0, state_off, wts, l,
                                n_streams=n_streams, seq=seq, width=width)
    zm = _attn(proj, mem_k, mem_v, kv_off, n_streams=n_streams, seq=seq, width=width, q_blk=5, heads=heads)
    x = _merge(zl, zc, zm, proj, x, wts, l, width=width)
    x = _mlp(x, wts, l)
    return x, hn, lbn, sbn


def kernel(x_prompt, x_sample, mem_prompt, state_lru_h, state_lru_conv, state_sconv, cache_mem_k, cache_mem_v, g_mix_pre, w_in, b_gate, lru_conv_w, lru_conv_b, lru_w_a, lru_b_a, lru_w_i, lru_b_i, lru_lambda, w_branch_lru, sconv_w, w_branch_conv, g_mem, w_mem_kv, w_branch_mem, w_out, g_mix_post, g_mlp_pre, w_up, w_down, g_mlp_post):
    depth = w_in.shape[0]
    bp, sp, d = x_prompt.shape
    bs, ss, _ = x_sample.shape
    width = lru_lambda.shape[1]
    mlen, heads, dh = cache_mem_k.shape[2:]
    kl, ks = lru_conv_w.shape[1], sconv_w.shape[1]

    def rowvec(p):
        return p.reshape(depth, 1, -1)

    wts = dict(
        g_mix_pre=rowvec(g_mix_pre), w_in=w_in.astype(BF16), b_gate=rowvec(b_gate),
        lru_conv_w=lru_conv_w, lru_conv_b=rowvec(lru_conv_b),
        wa_bd=_block_diag_pack(lru_w_a), lru_b_a=rowvec(lru_b_a),
        wi_bd=_block_diag_pack(lru_w_i), lru_b_i=rowvec(lru_b_i),
        lru_lambda=rowvec(lru_lambda), sconv_w=sconv_w,
        w_branch_lru=w_branch_lru.astype(BF16), w_branch_conv=w_branch_conv.astype(BF16),
        w_branch_mem=w_branch_mem.astype(BF16), w_out=w_out.astype(BF16),
        g_mix_post=rowvec(g_mix_post), g_mlp_pre=rowvec(g_mlp_pre),
        w_up=w_up.astype(BF16), w_down=w_down.astype(BF16), g_mlp_post=rowvec(g_mlp_post),
        g_mem=rowvec(g_mem), w_mem_kv=w_mem_kv.astype(BF16))

    hp = x_prompt.reshape(bp * sp, d)
    hs = x_sample.reshape(bs * ss, d)
    mem2d = mem_prompt.reshape(bp * mlen, d)
    p_h0 = jnp.zeros((bp, 1, width), F32)
    p_lb0 = jnp.zeros((bp, kl - 1, width), F32)
    p_sb0 = jnp.zeros((bp, ks - 1, width), F32)
    s_h0 = state_lru_h.reshape(depth * bs, 1, width)
    s_lb0 = state_lru_conv.reshape(depth * bs, kl - 1, width)
    s_sb0 = state_sconv.reshape(depth * bs, ks - 1, width)
    s_mk = cache_mem_k.reshape(depth * bs, mlen, heads, dh)
    s_mv = cache_mem_v.reshape(depth * bs, mlen, heads, dh)

    outs = {k: [] for k in ("p_h", "p_lb", "p_sb", "p_mk", "p_mv", "s_h", "s_lb", "s_sb")}
    for l in range(depth):
        kv = _memkv(mem2d, wts, l, width=width)
        mk = kv[0].reshape(bp, mlen, width)
        mv = kv[1].reshape(bp, mlen, width)
        hp, hn, lbn, sbn = _run_layer(hp, mk, mv, 0, p_h0, p_lb0, p_sb0, 0, wts, l,
                                      n_streams=bp, seq=sp, width=width, heads=heads)
        outs["p_h"].append(hn.reshape(bp, width))
        outs["p_lb"].append(lbn)
        outs["p_sb"].append(sbn)
        outs["p_mk"].append(mk.reshape(bp, mlen, heads, dh))
        outs["p_mv"].append(mv.reshape(bp, mlen, heads, dh))
        hs, hn, lbn, sbn = _run_layer(hs, s_mk, s_mv, l * bs, s_h0, s_lb0, s_sb0, l * bs, wts, l,
                                      n_streams=bs, seq=ss, width=width, heads=heads)
        outs["s_h"].append(hn.reshape(bs, width))
        outs["s_lb"].append(lbn)
        outs["s_sb"].append(sbn)

    return (hp.reshape(bp, sp, d), hs.reshape(bs, ss, d),
            jnp.stack(outs["p_h"]), jnp.stack(outs["p_lb"]), jnp.stack(outs["p_sb"]),
            jnp.stack(outs["p_mk"]), jnp.stack(outs["p_mv"]),
            jnp.stack(outs["s_h"]), jnp.stack(outs["s_lb"]), jnp.stack(outs["s_sb"]))
```

```python
import functools

import jax
import jax.numpy as jnp
from jax import lax
from jax.experimental import pallas as pl
from jax.experimental.pallas import tpu as pltpu

F32 = jnp.float32
BF16 = jnp.bfloat16

EPS = 1e-6
LRU_C = 8.0
N_BRANCH = 3

SUBLANES = 8
PACKED_ROWS = 16
MXU_DIM = 256
VMEM_LIMIT_BYTES = 56 * 1024 * 1024

PROJ_TM = 1024
MIX_ROWS = 512
ATTN_ROWS = 512
ATTN_KV_BYTES = 4 * 1024 * 1024
MERGE_TM = 256
MERGE_TN = 512
CARRY_PARTS = 2
MLP_TM, MLP_TF = 512, 1024
PAD_ROWS = SUBLANES


def _tile(n, pref, mult=SUBLANES):
    if n <= pref:
        return n
    for t in range(pref, 0, -1):
        if n % t == 0 and t % mult == 0:
            return t
    raise ValueError(f"no tile for {n} <= {pref}")


def _params(sem):
    return pltpu.CompilerParams(dimension_semantics=sem, vmem_limit_bytes=VMEM_LIMIT_BYTES)


def _rms(x, g):
    var = jnp.mean(x * x, axis=-1, keepdims=True)
    return (x * lax.rsqrt(var + EPS)) * g


def _sigmoid(x):
    return 0.5 * jnp.tanh(0.5 * x) + 0.5


def _layer_spec(l, shape, pipeline_mode=None):
    nd = len(shape)
    kw = {} if pipeline_mode is None else {"pipeline_mode": pipeline_mode}
    return pl.BlockSpec((None,) + tuple(shape), lambda *_: (l,) + (0,) * nd, **kw)


def _proj_kernel(x_ref, g_ref, w_ref, bg_ref, o_ref, xn_ref, *, gelu_blk, gate_blk0):
    j = pl.program_id(1)

    @pl.when(j == 0)
    def _():
        xn_ref[...] = _rms(x_ref[...], g_ref[...]).astype(BF16)

    def dot():
        return jnp.dot(xn_ref[...], w_ref[...], preferred_element_type=F32)

    @pl.when(j == gelu_blk)
    def _():
        o_ref[...] = jax.nn.gelu(dot()).astype(o_ref.dtype)

    @pl.when(j >= gate_blk0)
    def _():
        o_ref[...] = _sigmoid(dot() + bg_ref[...]).astype(o_ref.dtype)

    @pl.when(jnp.logical_and(j != gelu_blk, j < gate_blk0))
    def _():
        o_ref[...] = dot().astype(o_ref.dtype)


def _proj(x, wts, l, *, width):
    rows, d = x.shape
    cols = wts["w_in"].shape[2]
    tm, tn = _tile(rows, PROJ_TM), width
    gate_blk0 = (cols - N_BRANCH * d) // tn
    kern = functools.partial(_proj_kernel, gelu_blk=1, gate_blk0=gate_blk0)
    return pl.pallas_call(
        kern,
        grid=(rows // tm, cols // tn),
        in_specs=[
            pl.BlockSpec((tm, d), lambda i, j: (i, 0)),
            _layer_spec(l, (1, d)),
            pl.BlockSpec((None, d, tn), lambda i, j: (l, 0, j)),
            pl.BlockSpec((None, 1, tn), lambda i, j: (l, 0, jnp.maximum(j - gate_blk0, 0))),
        ],
        out_specs=pl.BlockSpec((tm, tn), lambda i, j: (i, j)),
        out_shape=jax.ShapeDtypeStruct((rows, cols), BF16),
        scratch_shapes=[pltpu.VMEM((tm, d), BF16)],
        compiler_params=_params(("parallel", "arbitrary")),
        name="proj",
    )(x, wts["g_mix_pre"], wts["w_in"], wts["b_gate"])


def _memkv_kernel(x_ref, g_ref, w_ref, o_ref, xn_ref):
    @pl.when(pl.program_id(1) == 0)
    def _():
        xn_ref[...] = _rms(x_ref[...], g_ref[...]).astype(BF16)

    o_ref[...] = jnp.dot(xn_ref[...], w_ref[...], preferred_element_type=F32)


def _memkv(mem, wts, l, *, width):
    rows, d = mem.shape
    cols = wts["w_mem_kv"].shape[2]
    tm, tn = _tile(rows, PROJ_TM), width
    return pl.pallas_call(
        _memkv_kernel,
        grid=(rows // tm, cols // tn),
        in_specs=[
            pl.BlockSpec((tm, d), lambda i, j: (i, 0)),
            _layer_spec(l, (1, d)),
            pl.BlockSpec((None, d, tn), lambda i, j: (l, 0, j)),
        ],
        out_specs=pl.BlockSpec((None, tm, tn), lambda i, j: (j, i, 0)),
        out_shape=jax.ShapeDtypeStruct((cols // tn, rows, tn), F32),
        scratch_shapes=[pltpu.VMEM((tm, d), BF16)],
        compiler_params=_params(("parallel", "arbitrary")),
        name="memkv",
    )(mem, wts["g_mem"], wts["w_mem_kv"])


def _softplus(x):
    return jnp.maximum(x, 0.0) + jnp.log1p(jnp.exp(-jnp.abs(x)))


def _causal_conv(pad_ref, w_ref, rows):
    taps = w_ref.shape[0]
    xe = pad_ref[...]
    y = None
    for k in range(taps):
        back = taps - 1 - k
        xk = xe if back == 0 else pltpu.roll(xe, back, axis=0)
        term = xk[PAD_ROWS:PAD_ROWS + rows, :] * w_ref[k:k + 1, :]
        y = term if y is None else y + term
    return y


def _mix_init(h0_ref, lb0_ref, sb0_ref, xpad, cpad, hcar):
    width = xpad.shape[1]
    xpad[0:PAD_ROWS, :] = jnp.zeros((PAD_ROWS, width), F32)
    cpad[0:PAD_ROWS, :] = jnp.zeros((PAD_ROWS, width), F32)
    xpad[PAD_ROWS - lb0_ref.shape[1]:PAD_ROWS, :] = lb0_ref[0]
    cpad[PAD_ROWS - sb0_ref.shape[1]:PAD_ROWS, :] = sb0_ref[0]
    hcar[...] = h0_ref[0]


def _mix_state_out(hn_ref, lbn_ref, sbn_ref, xpad, cpad, hcar):
    hn_ref[0] = hcar[...]
    lbn_ref[0] = xpad[PAD_ROWS - lbn_ref.shape[1]:PAD_ROWS, :]
    sbn_ref[0] = cpad[PAD_ROWS - sbn_ref.shape[1]:PAD_ROWS, :]


def _mix_lru_conv(lx_ref, cw_ref, cb_ref, xpad, uscr):
    rows = lx_ref.shape[0]
    kl = cw_ref.shape[0]
    xpad[PAD_ROWS:PAD_ROWS + rows, :] = lx_ref[...].astype(F32)
    uscr[...] = _causal_conv(xpad, cw_ref, rows) + cb_ref[...]
    xpad[PAD_ROWS - (kl - 1):PAD_ROWS, :] = xpad[PAD_ROWS + rows - (kl - 1):PAD_ROWS + rows, :]


def _mix_gate_dots(uscr, wa_ref, ba_ref, wi_ref, bi_ref, acum, bcum):
    n_chunks = wa_ref.shape[0]
    bd = uscr.shape[1] // n_chunks
    for c in range(n_chunks):
        sl = slice(c * bd, (c + 1) * bd)
        ub = uscr[:, sl].astype(BF16)
        acum[:, sl] = jnp.dot(ub, wa_ref[c], preferred_element_type=F32) + ba_ref[:, sl]
        bcum[:, sl] = jnp.dot(ub, wi_ref[c], preferred_element_type=F32) + bi_ref[:, sl]


def _mix_sconv(sb_ref, sc_ref, sh_ref, scw_ref, zc_ref, cpad):
    rows = sc_ref.shape[0]
    ks = scw_ref.shape[0]
    cpad[PAD_ROWS:PAD_ROWS + rows, :] = sc_ref[...].astype(F32) * sh_ref[...].astype(F32)
    cv = _causal_conv(cpad, scw_ref, rows)
    zc_ref[...] = (sb_ref[...].astype(F32) * cv).astype(zc_ref.dtype)
    cpad[PAD_ROWS - (ks - 1):PAD_ROWS, :] = cpad[PAD_ROWS + rows - (ks - 1):PAD_ROWS + rows, :]


def _mix_scan_chunk(c, n_chunks, lam_ref, uscr, ya_s, yi_s, acum, bcum):
    rows, width = uscr.shape
    groups = rows // SUBLANES
    bd = width // n_chunks
    sl = slice(c * bd, (c + 1) * bd)
    neg_c_sp = -LRU_C * _softplus(-lam_ref[:, sl])
    uc = uscr[:, sl]
    r = _sigmoid(ya_s[:, sl])
    ig = _sigmoid(yi_s[:, sl])
    log_a = neg_c_sp * r
    a = jnp.exp(log_a)
    th = jnp.tanh(log_a)
    mult = jnp.sqrt(-2.0 * th) * lax.rsqrt(1.0 - th)
    b = mult * (ig * uc)
    a3 = a.reshape(groups, SUBLANES, bd)
    b3 = b.reshape(groups, SUBLANES, bd)
    row = lax.broadcasted_iota(jnp.int32, a3.shape, 1)
    d = 1
    while d < SUBLANES:
        a_prev = pltpu.roll(a3, d, axis=1)
        b_prev = pltpu.roll(b3, d, axis=1)
        take = row >= d
        b3 = jnp.where(take, a3 * b_prev + b3, b3)
        a3 = jnp.where(take, a3 * a_prev, a3)
        d *= 2
    acum[:, sl] = a3.reshape(rows, bd)
    bcum[:, sl] = b3.reshape(rows, bd)


def _mix_carry(lo, hi, gg_ref, zl_ref, acum, bcum, hcar, *, unroll):
    def body(g, h):
        r0 = pl.multiple_of(g * PACKED_ROWS, PACKED_ROWS)
        r1 = pl.multiple_of(r0 + SUBLANES, SUBLANES)
        h1 = acum[pl.ds(r0, SUBLANES), :] * h + bcum[pl.ds(r0, SUBLANES), :]
        h2 = acum[pl.ds(r1, SUBLANES), :] * h1[SUBLANES - 1:SUBLANES, :] + bcum[pl.ds(r1, SUBLANES), :]
        hh = jnp.concatenate([h1, h2], axis=0)
        gate = gg_ref[pl.ds(r0, PACKED_ROWS), :].astype(F32)
        zl_ref[pl.ds(r0, PACKED_ROWS), :] = (hh * gate).astype(zl_ref.dtype)
        return h2[SUBLANES - 1:SUBLANES, :]

    hcar[...] = lax.fori_loop(lo, hi, body, hcar[...], unroll=unroll)


def _mix_kernel(lx_ref, gg_ref, sb_ref, sc_ref, sh_ref, h0_ref, lb0_ref, sb0_ref,
                cw_ref, cb_ref, wa_ref, ba_ref, wi_ref, bi_ref, lam_ref, scw_ref,
                zl_ref, zc_ref, hn_ref, lbn_ref, sbn_ref,
                xpad, cpad, acum, bcum, hcar, uscr):
    @pl.when(pl.program_id(1) == 0)
    def _():
        _mix_init(h0_ref, lb0_ref, sb0_ref, xpad, cpad, hcar)

    _mix_lru_conv(lx_ref, cw_ref, cb_ref, xpad, uscr)
    _mix_gate_dots(uscr, wa_ref, ba_ref, wi_ref, bi_ref, acum, bcum)
    _mix_sconv(sb_ref, sc_ref, sh_ref, scw_ref, zc_ref, cpad)
    n_chunks = wa_ref.shape[0]
    for c in range(n_chunks):
        _mix_scan_chunk(c, n_chunks, lam_ref, uscr, acum, bcum, acum, bcum)
    _mix_carry(0, lx_ref.shape[0] // PACKED_ROWS, gg_ref, zl_ref, acum, bcum, hcar, unroll=False)
    _mix_state_out(hn_ref, lbn_ref, sbn_ref, xpad, cpad, hcar)


def _mix_scratch(rows, width):
    return [pltpu.VMEM((PAD_ROWS + rows, width), F32),
            pltpu.VMEM((PAD_ROWS + rows, width), F32),
            pltpu.VMEM((rows, width), F32),
            pltpu.VMEM((rows, width), F32),
            pltpu.VMEM((1, width), F32),
            pltpu.VMEM((rows, width), F32)]


def _mix_param_specs(wts, l, width):
    kl, ks = wts["lru_conv_w"].shape[1], wts["sconv_w"].shape[1]
    nck = wts["wa_bd"].shape[1]
    bd_spec = _layer_spec(l, (nck, width // nck, width // nck))
    return [_layer_spec(l, (kl, width)), _layer_spec(l, (1, width)),
            bd_spec, _layer_spec(l, (1, width)), bd_spec, _layer_spec(l, (1, width)),
            _layer_spec(l, (1, width)), _layer_spec(l, (ks, width))]


def _mix_param_args(wts):
    return (wts["lru_conv_w"], wts["lru_conv_b"], wts["wa_bd"], wts["lru_b_a"], wts["wi_bd"], wts["lru_b_i"],
            wts["lru_lambda"], wts["sconv_w"])


def _mix(proj, h0, lb0, sb0, state_off, wts, l, *, n_streams, seq, width):
    rows = _tile(seq, MIX_ROWS, PACKED_ROWS)
    nt = seq // rows
    kl, ks = wts["lru_conv_w"].shape[1], wts["sconv_w"].shape[1]

    def col(c):
        return pl.BlockSpec((rows, width), lambda s, t: (s * nt + t, c))

    def state_in(k):
        return pl.BlockSpec((1, k, width), lambda s, t: (state_off + s, 0, 0))

    def state_out(k):
        return pl.BlockSpec((1, k, width), lambda s, t: (s, 0, 0))

    total = n_streams * seq
    return pl.pallas_call(
        _mix_kernel,
        grid=(n_streams, nt),
        in_specs=[col(0), col(1), col(2), col(3), col(4),
                  state_in(1), state_in(kl - 1), state_in(ks - 1)] + _mix_param_specs(wts, l, width),
        out_specs=[pl.BlockSpec((rows, width), lambda s, t: (s * nt + t, 0)),
                   pl.BlockSpec((rows, width), lambda s, t: (s * nt + t, 0)),
                   state_out(1), state_out(kl - 1), state_out(ks - 1)],
        out_shape=[jax.ShapeDtypeStruct((total, width), BF16),
                   jax.ShapeDtypeStruct((total, width), BF16),
                   jax.ShapeDtypeStruct((n_streams, 1, width), F32),
                   jax.ShapeDtypeStruct((n_streams, kl - 1, width), F32),
                   jax.ShapeDtypeStruct((n_streams, ks - 1, width), F32)],
        scratch_shapes=_mix_scratch(rows, width),
        compiler_params=_params(("parallel", "arbitrary")),
        name="mix",
    )(proj, proj, proj, proj, proj, h0, lb0, sb0, *_mix_param_args(wts))


def _attn_kernel(q_ref, k_ref, v_ref, o_ref, *, heads, seq_rows):
    dh = q_ref.shape[1] // heads
    scale = dh ** -0.5
    if len(k_ref.shape) == 4:
        mh = k_ref.shape[1] * heads
        for g in range(k_ref.shape[0]):
            rs = slice(g * seq_rows, (g + 1) * seq_rows)
            kf = k_ref[g].reshape(mh, dh).astype(BF16)
            vf = v_ref[g].reshape(mh, dh).astype(BF16)
            col_head = lax.broadcasted_iota(jnp.int32, (seq_rows, mh), 1) % heads
            for h in range(heads):
                sl = slice(h * dh, (h + 1) * dh)
                s = lax.dot_general(q_ref[rs, sl], kf, (((1,), (1,)), ((), ())),
                                    preferred_element_type=F32) * scale
                own = col_head == h
                s = jnp.where(own, s, -jnp.inf)
                e = jnp.where(own, jnp.exp(s - jnp.max(s, axis=-1, keepdims=True)), 0.0)
                p = e / jnp.sum(e, axis=-1, keepdims=True)
                o_ref[rs, sl] = jnp.dot(p.astype(BF16), vf, preferred_element_type=F32).astype(o_ref.dtype)
        return
    for g in range(k_ref.shape[0]):
        rs = slice(g * seq_rows, (g + 1) * seq_rows)
        for h in range(heads):
            sl = slice(h * dh, (h + 1) * dh)
            kh = k_ref[g, :, sl].astype(BF16)
            vh = v_ref[g, :, sl].astype(BF16)
            s = lax.dot_general(q_ref[rs, sl], kh, (((1,), (1,)), ((), ())),
                                preferred_element_type=F32) * scale
            e = jnp.exp(s - jnp.max(s, axis=-1, keepdims=True))
            p = e / jnp.sum(e, axis=-1, keepdims=True)
            o_ref[rs, sl] = jnp.dot(p.astype(BF16), vh, preferred_element_type=F32).astype(o_ref.dtype)


def _attn(proj, mem_k, mem_v, kv_off, *, n_streams, seq, width, q_blk, heads):
    mlen = mem_k.shape[1]
    rows = _tile(seq, ATTN_ROWS)
    nt = seq // rows
    per_step = 1
    if nt == 1:
        per_step = _tile(n_streams, max(1, ATTN_KV_BYTES // (mlen * width * 4)), 1)
    assert kv_off % per_step == 0
    blk = per_step * rows
    kv_shape = (per_step,) + mem_k.shape[1:]
    kv_spec = pl.BlockSpec(kv_shape, lambda s, t: (kv_off // per_step + s,) + (0,) * (len(kv_shape) - 1))
    return pl.pallas_call(
        functools.partial(_attn_kernel, heads=heads, seq_rows=rows),
        grid=(n_streams // per_step, nt),
        in_specs=[pl.BlockSpec((blk, width), lambda s, t: (s * nt + t, q_blk)), kv_spec, kv_spec],
        out_specs=pl.BlockSpec((blk, width), lambda s, t: (s * nt + t, 0)),
        out_shape=jax.ShapeDtypeStruct((n_streams * seq, width), BF16),
        compiler_params=_params(("parallel", "arbitrary")),
        name="attn",
    )(proj, mem_k, mem_v)


def _merge_m_tile(cs, zl_ref, zc_ref, zm_ref, g0_ref, g1_ref, g2_ref, wl_ref, wc_ref, wm_ref):
    m = g0_ref[:, cs].astype(F32) * jnp.dot(zl_ref[...], wl_ref[:, cs], preferred_element_type=F32)
    m = m + g1_ref[:, cs].astype(F32) * jnp.dot(zc_ref[...], wc_ref[:, cs], preferred_element_type=F32)
    m = m + g2_ref[:, cs].astype(F32) * jnp.dot(zm_ref[...], wm_ref[:, cs], preferred_element_type=F32)
    return m.astype(BF16)


def _merge_kernel(zl_ref, zc_ref, zm_ref, g0_ref, g1_ref, g2_ref, x_ref,
                  wl_ref, wc_ref, wm_ref, wo_ref, gp_ref, o_ref):
    m = _merge_m_tile(slice(None), zl_ref, zc_ref, zm_ref, g0_ref, g1_ref, g2_ref, wl_ref, wc_ref, wm_ref)
    y = jnp.dot(m, wo_ref[...], preferred_element_type=F32)
    o_ref[...] = x_ref[...] + _rms(y, gp_ref[...])


def _merge_weight_specs(l, width, d):
    once = pl.Buffered(1)
    return [_layer_spec(l, (width, d), once), _layer_spec(l, (width, d), once),
            _layer_spec(l, (width, d), once), _layer_spec(l, (d, d), once),
            _layer_spec(l, (1, d), once)]


def _merge_weight_args(wts):
    return (wts["w_branch_lru"], wts["w_branch_conv"], wts["w_branch_mem"], wts["w_out"], wts["g_mix_post"])


def _merge(zl, zc, zm, proj, x, wts, l, *, width):
    rows, d = x.shape
    tm = _tile(rows, MERGE_TM)
    gate_blk0 = (proj.shape[1] - N_BRANCH * d) // d

    def act(w):
        return pl.BlockSpec((tm, w), lambda i: (i, 0))

    def gate(b):
        return pl.BlockSpec((tm, d), lambda i: (i, gate_blk0 + b))

    return pl.pallas_call(
        _merge_kernel,
        grid=(rows // tm,),
        in_specs=[act(width), act(width), act(width), gate(0), gate(1), gate(2), act(d)]
        + _merge_weight_specs(l, width, d),
        out_specs=act(d),
        out_shape=jax.ShapeDtypeStruct((rows, d), F32),
        compiler_params=_params(("parallel",)),
        name="merge",
    )(zl, zc, zm, proj, proj, proj, x, *_merge_weight_args(wts))


def _mixmerge_kernel(lx_ref, gg_ref, sb_ref, sc_ref, sh_ref, h0_ref, lb0_ref, sb0_ref,
                     cw_ref, cb_ref, wa_ref, ba_ref, wi_ref, bi_ref, lam_ref, scw_ref,
                     zm_ref, g0_ref, g1_ref, g2_ref, x_ref, wl_ref, wc_ref, wm_ref, wo_ref, gp_ref,
                     o_ref, hn_ref, lbn_ref, sbn_ref,
                     xpad, cpad, acum, bcum, hcar, uscr, ya_s, yi_s, zl_s, zc_s, m_s, y_s,
                     *, blocks_per_stream, n_blocks):
    g = pl.program_id(0)
    kl = cw_ref.shape[0]

    @pl.when(g == 0)
    def _():
        for ref in (cpad, hcar, uscr, ya_s, yi_s, zl_s, zc_s):
            ref[...] = jnp.zeros(ref.shape, ref.dtype)

    @pl.when(jnp.logical_and(g % blocks_per_stream == 0, g < n_blocks))
    def _():
        xpad[0:PAD_ROWS, :] = jnp.zeros((PAD_ROWS, xpad.shape[1]), F32)
        xpad[PAD_ROWS - (kl - 1):PAD_ROWS, :] = lb0_ref[0]

    scan_valid = jnp.logical_and(g >= 1, g <= n_blocks)

    @pl.when(jnp.logical_and((g - 1) % blocks_per_stream == 0, scan_valid))
    def _():
        cpad[0:PAD_ROWS, :] = jnp.zeros((PAD_ROWS, cpad.shape[1]), F32)
        cpad[PAD_ROWS - sb0_ref.shape[1]:PAD_ROWS, :] = sb0_ref[0]
        hcar[...] = h0_ref[0]

    rows, d = x_ref.shape
    n_chunks = wa_ref.shape[0]
    col_tiles = [slice(j, j + MERGE_TN) for j in range(0, d, MERGE_TN)]
    tiles = rows // PACKED_ROWS
    carry_parts = [(k * tiles // CARRY_PARTS, (k + 1) * tiles // CARRY_PARTS) for k in range(CARRY_PARTS)]

    def m_piece(cs):
        def run():
            m_s[:, cs] = _merge_m_tile(cs, zl_s, zc_s, zm_ref, g0_ref, g1_ref, g2_ref, wl_ref, wc_ref, wm_ref)
        return run

    def y_piece(cs):
        def run():
            y_s[:, cs] = jnp.dot(m_s[...], wo_ref[:, cs], preferred_element_type=F32)
        return run

    def chunk_piece(c):
        return lambda: _mix_scan_chunk(c, n_chunks, lam_ref, uscr, ya_s, yi_s, acum, bcum)

    def carry_piece(lo, hi):
        return lambda: _mix_carry(lo, hi, gg_ref, zl_s, acum, bcum, hcar, unroll=True)

    matmul_pieces = [m_piece(cs) for cs in col_tiles] + [y_piece(cs) for cs in col_tiles]
    vector_pieces = ([chunk_piece(c) for c in range(n_chunks)]
                     + [None] * max(0, len(col_tiles) - n_chunks)
                     + [lambda: _mix_sconv(sb_ref, sc_ref, sh_ref, scw_ref, zc_s, cpad)]
                     + [carry_piece(lo, hi) for lo, hi in carry_parts]
                     + [lambda: _mix_lru_conv(lx_ref, cw_ref, cb_ref, xpad, uscr)])
    assert n_chunks <= len(col_tiles)
    for k in range(max(len(matmul_pieces), len(vector_pieces))):
        if k < len(matmul_pieces):
            matmul_pieces[k]()
        if k < len(vector_pieces) and vector_pieces[k] is not None:
            vector_pieces[k]()
    _mix_gate_dots(uscr, wa_ref, ba_ref, wi_ref, bi_ref, ya_s, yi_s)
    o_ref[...] = x_ref[...] + _rms(y_s[...], gp_ref[...])

    @pl.when(g < n_blocks)
    def _():
        lbn_ref[0] = xpad[PAD_ROWS - (kl - 1):PAD_ROWS, :]

    @pl.when(scan_valid)
    def _():
        hn_ref[0] = hcar[...]
        sbn_ref[0] = cpad[PAD_ROWS - sbn_ref.shape[1]:PAD_ROWS, :]


def _mixmerge(proj, zm, x, h0, lb0, sb0, state_off, wts, l, *, n_streams, seq, width):
    rows, d = x.shape
    tm = _tile(seq, MERGE_TM, PACKED_ROWS)
    nt = seq // tm
    n = n_streams * nt
    kl, ks = wts["lru_conv_w"].shape[1], wts["sconv_w"].shape[1]
    gate_blk0 = (proj.shape[1] - N_BRANCH * d) // d

    def blk(lag):
        return lambda g: jnp.clip(g - lag, 0, n - 1)

    conv_blk, scan_blk, merge_blk = blk(0), blk(1), blk(2)

    def col(c, which):
        return pl.BlockSpec((tm, width), lambda g: (which(g), c))

    def state_in(k, which):
        return pl.BlockSpec((1, k, width), lambda g: (state_off + which(g) // nt, 0, 0))

    def state_out(k, which):
        return pl.BlockSpec((1, k, width), lambda g: (which(g) // nt, 0, 0))

    def gate(b):
        return pl.BlockSpec((tm, d), lambda g: (merge_blk(g), gate_blk0 + b))

    def act(w):
        return pl.BlockSpec((tm, w), lambda g: (merge_blk(g), 0))

    kern = functools.partial(_mixmerge_kernel, blocks_per_stream=nt, n_blocks=n)
    act_bf16 = pltpu.VMEM((tm, width), BF16)
    act_f32 = pltpu.VMEM((tm, width), F32)
    return pl.pallas_call(
        kern,
        grid=(n + 2,),
        in_specs=[col(0, conv_blk), col(1, scan_blk), col(2, scan_blk), col(3, scan_blk), col(4, scan_blk),
                  state_in(1, scan_blk), state_in(kl - 1, conv_blk), state_in(ks - 1, scan_blk)]
        + _mix_param_specs(wts, l, width)
        + [act(width), gate(0), gate(1), gate(2), act(d)] + _merge_weight_specs(l, width, d),
        out_specs=[act(d), state_out(1, scan_blk), state_out(kl - 1, conv_blk), state_out(ks - 1, scan_blk)],
        out_shape=[jax.ShapeDtypeStruct((rows, d), F32),
                   jax.ShapeDtypeStruct((n_streams, 1, width), F32),
                   jax.ShapeDtypeStruct((n_streams, kl - 1, width), F32),
                   jax.ShapeDtypeStruct((n_streams, ks - 1, width), F32)],
        scratch_shapes=_mix_scratch(tm, width) + [act_f32, act_f32, act_bf16, act_bf16,
                                                  pltpu.VMEM((tm, d), BF16), pltpu.VMEM((tm, d), F32)],
        compiler_params=_params(("arbitrary",)),
        name="mixmerge",
    )(proj, proj, proj, proj, proj, h0, lb0, sb0, *_mix_param_args(wts),
      zm, proj, proj, proj, x, *_merge_weight_args(wts))


def _mlp_kernel(x_ref, gpre_ref, wu_ref, wd_ref, gpost_ref, o_ref, xn_ref, acc_ref):
    f = pl.program_id(1)
    last = pl.num_programs(1) - 1
    tm = x_ref.shape[0]

    def partial_down(rows):
        hid = jnp.dot(xn_ref[rows, :], wu_ref[...], preferred_element_type=F32)
        hid = jnp.square(jnp.maximum(hid, 0.0)).astype(BF16)
        return jnp.dot(hid, wd_ref[...], preferred_element_type=F32)

    @pl.when(f == 0)
    def _():
        xn_ref[...] = _rms(x_ref[...], gpre_ref[...]).astype(BF16)
        acc_ref[...] = partial_down(slice(None))

    @pl.when(jnp.logical_and(f > 0, f < last))
    def _():
        acc_ref[...] += partial_down(slice(None))

    @pl.when(f == last)
    def _():
        for half in (slice(0, tm // 2), slice(tm // 2, tm)):
            y = acc_ref[half, :] + partial_down(half)
            o_ref[half, :] = x_ref[half, :] + _rms(y, gpost_ref[...])


def _mlp(x, wts, l):
    rows, d = x.shape
    dff = wts["w_up"].shape[2]
    tm, tf = _tile(rows, MLP_TM, 2 * SUBLANES), _tile(dff, MLP_TF, 128)
    assert dff // tf >= 2
    return pl.pallas_call(
        _mlp_kernel,
        grid=(rows // tm, dff // tf),
        in_specs=[pl.BlockSpec((tm, d), lambda i, f: (i, 0)),
                  _layer_spec(l, (1, d)),
                  pl.BlockSpec((None, d, tf), lambda i, f: (l, 0, f)),
                  pl.BlockSpec((None, tf, d), lambda i, f: (l, f, 0)),
                  _layer_spec(l, (1, d))],
        out_specs=pl.BlockSpec((tm, d), lambda i, f: (i, 0)),
        out_shape=jax.ShapeDtypeStruct((rows, d), F32),
        scratch_shapes=[pltpu.VMEM((tm, d), BF16), pltpu.VMEM((tm, d), F32)],
        compiler_params=_params(("parallel", "arbitrary")),
        name="mlp",
    )(x, wts["g_mlp_pre"], wts["w_up"], wts["w_down"], wts["g_mlp_post"])


def _block_diag_pack(w):
    depth, heads, dh, _ = w.shape
    per = max(1, min(heads, MXU_DIM // dh))
    n = heads // per
    wg = w.reshape(depth, n, per, dh, dh)
    eye = jnp.eye(per, dtype=w.dtype)
    out = jnp.einsum("lnhij,hg->lnhigj", wg, eye)
    return out.reshape(depth, n, per * dh, per * dh).astype(BF16)


def _run_layer(x, mem_k, mem_v, kv_off, h0, lb0, sb0, state_off, wts, l, *, n_streams, seq, width, heads):
    proj = _proj(x, wts, l, width=width)
    zm = _attn(proj, mem_k, mem_v, kv_off, n_streams=n_streams, seq=seq, width=width, q_blk=5, heads=heads)
    if seq >= MERGE_TM and seq % MERGE_TM == 0:
        x, hn, lbn, sbn = _mixmerge(proj, zm, x, h0, lb0, sb0, state_off, wts, l,
                                    n_streams=n_streams, seq=seq, width=width)
    else:
        zl, zc, hn, lbn, sbn = _mix(proj, h0, lb0, sb0, state_off, wts, l,
                                    n_streams=n_streams, seq=seq, width=width)
        x = _merge(zl, zc, zm, proj, x, wts, l, width=width)
    x = _mlp(x, wts, l)
    return x, hn, lbn, sbn


def kernel(x_prompt, x_sample, mem_prompt, state_lru_h, state_lru_conv, state_sconv, cache_mem_k, cache_mem_v, g_mix_pre, w_in, b_gate, lru_conv_w, lru_conv_b, lru_w_a, lru_b_a, lru_w_i, lru_b_i, lru_lambda, w_branch_lru, sconv_w, w_branch_conv, g_mem, w_mem_kv, w_branch_mem, w_out, g_mix_post, g_mlp_pre, w_up, w_down, g_mlp_post):
    depth = w_in.shape[0]
    bp, sp, d = x_prompt.shape
    bs, ss, _ = x_sample.shape
    width = lru_lambda.shape[1]
    mlen, heads, dh = cache_mem_k.shape[2:]
    kl, ks = lru_conv_w.shape[1], sconv_w.shape[1]

    def rowvec(p):
        return p.reshape(depth, 1, -1)

    wts = dict(
        g_mix_pre=rowvec(g_mix_pre), w_in=w_in.astype(BF16), b_gate=rowvec(b_gate),
        lru_conv_w=lru_conv_w, lru_conv_b=rowvec(lru_conv_b),
        wa_bd=_block_diag_pack(lru_w_a), lru_b_a=rowvec(lru_b_a),
        wi_bd=_block_diag_pack(lru_w_i), lru_b_i=rowvec(lru_b_i),
        lru_lambda=rowvec(lru_lambda), sconv_w=sconv_w,
        w_branch_lru=w_branch_lru.astype(BF16), w_branch_conv=w_branch_conv.astype(BF16),
        w_branch_mem=w_branch_mem.astype(BF16), w_out=w_out.astype(BF16),
        g_mix_post=rowvec(g_mix_post), g_mlp_pre=rowvec(g_mlp_pre),
        w_up=w_up.astype(BF16), w_down=w_down.astype(BF16), g_mlp_post=rowvec(g_mlp_post),
        g_mem=rowvec(g_mem), w_mem_kv=w_mem_kv.astype(BF16))

    hp = x_prompt.reshape(bp * sp, d)
    hs = x_sample.reshape(bs * ss, d)
    mem2d = mem_prompt.reshape(bp * mlen, d)
    p_h0 = jnp.zeros((bp, 1, width), F32)
    p_lb0 = jnp.zeros((bp, kl - 1, width), F32)
    p_sb0 = jnp.zeros((bp, ks - 1, width), F32)
    s_h0 = state_lru_h.reshape(depth * bs, 1, width)
    s_lb0 = state_lru_conv.reshape(depth * bs, kl - 1, width)
    s_sb0 = state_sconv.reshape(depth * bs, ks - 1, width)
    s_mk = cache_mem_k.reshape(depth * bs, mlen, heads, dh)
    s_mv = cache_mem_v.reshape(depth * bs, mlen, heads, dh)

    outs = {k: [] for k in ("p_h", "p_lb", "p_sb", "p_mk", "p_mv", "s_h", "s_lb", "s_sb")}
    for l in range(depth):
        kv = _memkv(mem2d, wts, l, width=width)
        mk = kv[0].reshape(bp, mlen, width)
        mv = kv[1].reshape(bp, mlen, width)
        hp, hn, lbn, sbn = _run_layer(hp, mk, mv, 0, p_h0, p_lb0, p_sb0, 0, wts, l,
                                      n_streams=bp, seq=sp, width=width, heads=heads)
        outs["p_h"].append(hn.reshape(bp, width))
        outs["p_lb"].append(lbn)
        outs["p_sb"].append(sbn)
        outs["p_mk"].append(mk.reshape(bp, mlen, heads, dh))
        outs["p_mv"].append(mv.reshape(bp, mlen, heads, dh))
        hs, hn, lbn, sbn = _run_layer(hs, s_mk, s_mv, l * bs, s_h0, s_lb0, s_sb0, l * bs, wts, l,
                                      n_streams=bs, seq=ss, width=width, heads=heads)
        outs["s_h"].append(hn.reshape(bs, width))
        outs["s_lb"].append(lbn)
        outs["s_sb"].append(sbn)

    return (hp.reshape(bp, sp, d), hs.reshape(bs, ss, d),
            jnp.stack(outs["p_h"]), jnp.stack(outs["p_lb"]), jnp.stack(outs["p_sb"]),
            jnp.stack(outs["p_mk"]), jnp.stack(outs["p_mv"]),
            jnp.stack(outs["s_h"]), jnp.stack(outs["s_lb"]), jnp.stack(outs["s_sb"]))
```

```python
import functools

import jax
import jax.numpy as jnp
from jax import lax
from jax.experimental import pallas as pl
from jax.experimental.pallas import tpu as pltpu

F32 = jnp.float32
BF16 = jnp.bfloat16

EPS = 1e-6
LRU_C = 8.0
N_BRANCH = 3

SUBLANES = 8
PACKED_ROWS = 16
MXU_DIM = 256
VMEM_LIMIT_BYTES = 56 * 1024 * 1024

PROJ_TM = 1024
MIX_ROWS = 512
ATTN_ROWS = 512
ATTN_KV_BYTES = 4 * 1024 * 1024
MERGE_TM = 256
MERGE_TN = 512
MLP_TM, MLP_TF = 512, 1024
PAD_ROWS = SUBLANES
SCAN_ROWS = 32


def _tile(n, pref, mult=SUBLANES):
    if n <= pref:
        return n
    for t in range(pref, 0, -1):
        if n % t == 0 and t % mult == 0:
            return t
    raise ValueError(f"no tile for {n} <= {pref}")


def _params(sem):
    return pltpu.CompilerParams(dimension_semantics=sem, vmem_limit_bytes=VMEM_LIMIT_BYTES)


def _rms(x, g):
    var = jnp.mean(x * x, axis=-1, keepdims=True)
    return (x * lax.rsqrt(var + EPS)) * g


def _sigmoid(x):
    return 0.5 * jnp.tanh(0.5 * x) + 0.5


def _layer_spec(l, shape, pipeline_mode=None):
    nd = len(shape)
    kw = {} if pipeline_mode is None else {"pipeline_mode": pipeline_mode}
    return pl.BlockSpec((None,) + tuple(shape), lambda *_: (l,) + (0,) * nd, **kw)


def _prenorm_kernel(x_ref, g_ref, o_ref):
    o_ref[...] = _rms(x_ref[...], g_ref[...]).astype(o_ref.dtype)


def _prenorm(x, wts, l):
    rows, d = x.shape
    tm = _tile(rows, PROJ_TM)
    return pl.pallas_call(
        _prenorm_kernel,
        grid=(rows // tm,),
        in_specs=[pl.BlockSpec((tm, d), lambda i: (i, 0)), _layer_spec(l, (1, d))],
        out_specs=pl.BlockSpec((tm, d), lambda i: (i, 0)),
        out_shape=jax.ShapeDtypeStruct((rows, d), BF16),
        compiler_params=_params(("parallel",)),
        name="prenorm",
    )(x, wts["g_mix_pre"])


def _proj_kernel(xn_ref, w_ref, bg_ref, o_ref, *, gelu_blk, gate_blk0):
    j = pl.program_id(1)

    def dot():
        return jnp.dot(xn_ref[...], w_ref[...], preferred_element_type=F32)

    @pl.when(j == gelu_blk)
    def _():
        o_ref[...] = jax.nn.gelu(dot()).astype(o_ref.dtype)

    @pl.when(j >= gate_blk0)
    def _():
        o_ref[...] = _sigmoid(dot() + bg_ref[...]).astype(o_ref.dtype)

    @pl.when(jnp.logical_and(j != gelu_blk, j < gate_blk0))
    def _():
        o_ref[...] = dot().astype(o_ref.dtype)


def _proj(xn, wts, l, *, width):
    rows, d = xn.shape
    cols = wts["w_in"].shape[2]
    tm, tn = _tile(rows, PROJ_TM), width
    gate_blk0 = (cols - N_BRANCH * d) // tn
    kern = functools.partial(_proj_kernel, gelu_blk=1, gate_blk0=gate_blk0)
    return pl.pallas_call(
        kern,
        grid=(rows // tm, cols // tn),
        in_specs=[
            pl.BlockSpec((tm, d), lambda i, j: (i, 0)),
            pl.BlockSpec((None, d, tn), lambda i, j: (l, 0, j)),
            pl.BlockSpec((None, 1, tn), lambda i, j: (l, 0, jnp.maximum(j - gate_blk0, 0))),
        ],
        out_specs=pl.BlockSpec((tm, tn), lambda i, j: (i, j)),
        out_shape=jax.ShapeDtypeStruct((rows, cols), BF16),
        compiler_params=_params(("parallel", "arbitrary")),
        name="proj",
    )(xn, wts["w_in"], wts["b_gate"])


def _memkv_kernel(x_ref, g_ref, w_ref, o_ref, xn_ref):
    @pl.when(pl.program_id(1) == 0)
    def _():
        xn_ref[...] = _rms(x_ref[...], g_ref[...]).astype(BF16)

    o_ref[...] = jnp.dot(xn_ref[...], w_ref[...], preferred_element_type=F32)


def _memkv(mem, wts, l, *, width):
    rows, d = mem.shape
    cols = wts["w_mem_kv"].shape[2]
    tm, tn = _tile(rows, PROJ_TM), width
    return pl.pallas_call(
        _memkv_kernel,
        grid=(rows // tm, cols // tn),
        in_specs=[
            pl.BlockSpec((tm, d), lambda i, j: (i, 0)),
            _layer_spec(l, (1, d)),
            pl.BlockSpec((None, d, tn), lambda i, j: (l, 0, j)),
        ],
        out_specs=pl.BlockSpec((None, tm, tn), lambda i, j: (j, i, 0)),
        out_shape=jax.ShapeDtypeStruct((cols // tn, rows, tn), F32),
        scratch_shapes=[pltpu.VMEM((tm, d), BF16)],
        compiler_params=_params(("parallel", "arbitrary")),
        name="memkv",
    )(mem, wts["g_mem"], wts["w_mem_kv"])


def _softplus(x):
    return jnp.maximum(x, 0.0) + jnp.log1p(jnp.exp(-jnp.abs(x)))


def _causal_conv(pad_ref, w_ref, rows):
    taps = w_ref.shape[0]
    xe = pad_ref[...]
    y = None
    for k in range(taps):
        back = taps - 1 - k
        xk = xe if back == 0 else pltpu.roll(xe, back, axis=0)
        term = xk[PAD_ROWS:PAD_ROWS + rows, :] * w_ref[k:k + 1, :]
        y = term if y is None else y + term
    return y


def _mix_init(h0_ref, lb0_ref, sb0_ref, xpad, cpad, hcar):
    width = xpad.shape[1]
    xpad[0:PAD_ROWS, :] = jnp.zeros((PAD_ROWS, width), F32)
    cpad[0:PAD_ROWS, :] = jnp.zeros((PAD_ROWS, width), F32)
    xpad[PAD_ROWS - lb0_ref.shape[1]:PAD_ROWS, :] = lb0_ref[0]
    cpad[PAD_ROWS - sb0_ref.shape[1]:PAD_ROWS, :] = sb0_ref[0]
    hcar[...] = h0_ref[0]


def _mix_state_out(hn_ref, lbn_ref, sbn_ref, xpad, cpad, hcar):
    hn_ref[0] = hcar[...]
    lbn_ref[0] = xpad[PAD_ROWS - lbn_ref.shape[1]:PAD_ROWS, :]
    sbn_ref[0] = cpad[PAD_ROWS - sbn_ref.shape[1]:PAD_ROWS, :]


def _mix_lru_conv(lx_ref, cw_ref, cb_ref, xpad, uscr):
    rows = lx_ref.shape[0]
    kl = cw_ref.shape[0]
    xpad[PAD_ROWS:PAD_ROWS + rows, :] = lx_ref[...].astype(F32)
    uscr[...] = _causal_conv(xpad, cw_ref, rows) + cb_ref[...]
    xpad[PAD_ROWS - (kl - 1):PAD_ROWS, :] = xpad[PAD_ROWS + rows - (kl - 1):PAD_ROWS + rows, :]


def _mix_gate_dots(uscr, wa_ref, ba_ref, wi_ref, bi_ref, acum, bcum):
    n_chunks = wa_ref.shape[0]
    bd = uscr.shape[1] // n_chunks
    for c in range(n_chunks):
        sl = slice(c * bd, (c + 1) * bd)
        ub = uscr[:, sl].astype(BF16)
        acum[:, sl] = jnp.dot(ub, wa_ref[c], preferred_element_type=F32) + ba_ref[:, sl]
        bcum[:, sl] = jnp.dot(ub, wi_ref[c], preferred_element_type=F32) + bi_ref[:, sl]


def _mix_sconv(sb_ref, sc_ref, sh_ref, scw_ref, zc_ref, cpad):
    rows = sc_ref.shape[0]
    ks = scw_ref.shape[0]
    cpad[PAD_ROWS:PAD_ROWS + rows, :] = sc_ref[...].astype(F32) * sh_ref[...].astype(F32)
    cv = _causal_conv(cpad, scw_ref, rows)
    zc_ref[...] = (sb_ref[...].astype(F32) * cv).astype(zc_ref.dtype)
    cpad[PAD_ROWS - (ks - 1):PAD_ROWS, :] = cpad[PAD_ROWS + rows - (ks - 1):PAD_ROWS + rows, :]


def _mix_scan_chunk(c, n_chunks, lam_ref, uscr, ya_s, yi_s, acum, bcum):
    rows, width = uscr.shape
    bd = width // n_chunks
    sl = slice(c * bd, (c + 1) * bd)
    neg_c_sp = -LRU_C * _softplus(-lam_ref[:, sl])
    tr = _tile(rows, SCAN_ROWS)
    groups = tr // SUBLANES
    row = lax.broadcasted_iota(jnp.int32, (groups, SUBLANES, bd), 1)
    for r0 in range(0, rows, tr):
        rs = slice(r0, r0 + tr)
        r = _sigmoid(ya_s[rs, sl])
        ig = _sigmoid(yi_s[rs, sl])
        log_a = neg_c_sp * r
        a = jnp.exp(log_a)
        th = jnp.tanh(log_a)
        mult = jnp.sqrt(-2.0 * th) * lax.rsqrt(1.0 - th)
        b = mult * (ig * uscr[rs, sl])
        a3 = a.reshape(groups, SUBLANES, bd)
        b3 = b.reshape(groups, SUBLANES, bd)
        d = 1
        while d < SUBLANES:
            a_prev = pltpu.roll(a3, d, axis=1)
            b_prev = pltpu.roll(b3, d, axis=1)
            take = row >= d
            b3 = jnp.where(take, a3 * b_prev + b3, b3)
            a3 = jnp.where(take, a3 * a_prev, a3)
            d *= 2
        acum[rs, sl] = a3.reshape(tr, bd)
        bcum[rs, sl] = b3.reshape(tr, bd)


def _mix_carry(lo, hi, gg_ref, zl_ref, acum, bcum, hcar, *, unroll):
    def body(g, h):
        r0 = pl.multiple_of(g * PACKED_ROWS, PACKED_ROWS)
        r1 = pl.multiple_of(r0 + SUBLANES, SUBLANES)
        h1 = acum[pl.ds(r0, SUBLANES), :] * h + bcum[pl.ds(r0, SUBLANES), :]
        h2 = acum[pl.ds(r1, SUBLANES), :] * h1[SUBLANES - 1:SUBLANES, :] + bcum[pl.ds(r1, SUBLANES), :]
        hh = jnp.concatenate([h1, h2], axis=0)
        gate = gg_ref[pl.ds(r0, PACKED_ROWS), :].astype(F32)
        zl_ref[pl.ds(r0, PACKED_ROWS), :] = (hh * gate).astype(zl_ref.dtype)
        return h2[SUBLANES - 1:SUBLANES, :]

    hcar[...] = lax.fori_loop(lo, hi, body, hcar[...], unroll=unroll)


def _mix_kernel(lx_ref, gg_ref, sb_ref, sc_ref, sh_ref, h0_ref, lb0_ref, sb0_ref,
                cw_ref, cb_ref, wa_ref, ba_ref, wi_ref, bi_ref, lam_ref, scw_ref,
                zl_ref, zc_ref, hn_ref, lbn_ref, sbn_ref,
                xpad, cpad, acum, bcum, hcar, uscr):
    @pl.when(pl.program_id(1) == 0)
    def _():
        _mix_init(h0_ref, lb0_ref, sb0_ref, xpad, cpad, hcar)

    _mix_lru_conv(lx_ref, cw_ref, cb_ref, xpad, uscr)
    _mix_gate_dots(uscr, wa_ref, ba_ref, wi_ref, bi_ref, acum, bcum)
    _mix_sconv(sb_ref, sc_ref, sh_ref, scw_ref, zc_ref, cpad)
    n_chunks = wa_ref.shape[0]
    for c in range(n_chunks):
        _mix_scan_chunk(c, n_chunks, lam_ref, uscr, acum, bcum, acum, bcum)
    _mix_carry(0, lx_ref.shape[0] // PACKED_ROWS, gg_ref, zl_ref, acum, bcum, hcar, unroll=False)
    _mix_state_out(hn_ref, lbn_ref, sbn_ref, xpad, cpad, hcar)


def _mix_scratch(rows, width):
    return [pltpu.VMEM((PAD_ROWS + rows, width), F32),
            pltpu.VMEM((PAD_ROWS + rows, width), F32),
            pltpu.VMEM((rows, width), F32),
            pltpu.VMEM((rows, width), F32),
            pltpu.VMEM((1, width), F32),
            pltpu.VMEM((rows, width), F32)]


def _mix_param_specs(wts, l, width):
    kl, ks = wts["lru_conv_w"].shape[1], wts["sconv_w"].shape[1]
    nck = wts["wa_bd"].shape[1]
    bd_spec = _layer_spec(l, (nck, width // nck, width // nck))
    return [_layer_spec(l, (kl, width)), _layer_spec(l, (1, width)),
            bd_spec, _layer_spec(l, (1, width)), bd_spec, _layer_spec(l, (1, width)),
            _layer_spec(l, (1, width)), _layer_spec(l, (ks, width))]


def _mix_param_args(wts):
    return (wts["lru_conv_w"], wts["lru_conv_b"], wts["wa_bd"], wts["lru_b_a"], wts["wi_bd"], wts["lru_b_i"],
            wts["lru_lambda"], wts["sconv_w"])


def _mix(proj, h0, lb0, sb0, state_off, wts, l, *, n_streams, seq, width):
    rows = _tile(seq, MIX_ROWS, PACKED_ROWS)
    nt = seq // rows
    kl, ks = wts["lru_conv_w"].shape[1], wts["sconv_w"].shape[1]

    def col(c):
        return pl.BlockSpec((rows, width), lambda s, t: (s * nt + t, c))

    def state_in(k):
        return pl.BlockSpec((1, k, width), lambda s, t: (state_off + s, 0, 0))

    def state_out(k):
        return pl.BlockSpec((1, k, width), lambda s, t: (s, 0, 0))

    total = n_streams * seq
    return pl.pallas_call(
        _mix_kernel,
        grid=(n_streams, nt),
        in_specs=[col(0), col(1), col(2), col(3), col(4),
                  state_in(1), state_in(kl - 1), state_in(ks - 1)] + _mix_param_specs(wts, l, width),
        out_specs=[pl.BlockSpec((rows, width), lambda s, t: (s * nt + t, 0)),
                   pl.BlockSpec((rows, width), lambda s, t: (s * nt + t, 0)),
                   state_out(1), state_out(kl - 1), state_out(ks - 1)],
        out_shape=[jax.ShapeDtypeStruct((total, width), BF16),
                   jax.ShapeDtypeStruct((total, width), BF16),
                   jax.ShapeDtypeStruct((n_streams, 1, width), F32),
                   jax.ShapeDtypeStruct((n_streams, kl - 1, width), F32),
                   jax.ShapeDtypeStruct((n_streams, ks - 1, width), F32)],
        scratch_shapes=_mix_scratch(rows, width),
        compiler_params=_params(("parallel", "arbitrary")),
        name="mix",
    )(proj, proj, proj, proj, proj, h0, lb0, sb0, *_mix_param_args(wts))


def _attn_kernel(q_ref, k_ref, v_ref, o_ref, *, heads, seq_rows):
    dh = q_ref.shape[1] // heads
    scale = dh ** -0.5
    if len(k_ref.shape) == 4:
        mh = k_ref.shape[1] * heads
        shape = (heads * seq_rows, mh)
        row = lax.broadcasted_iota(jnp.int32, shape, 0)
        row_head = sum((row >= h * seq_rows).astype(jnp.int32) for h in range(1, heads))
        own = row_head == lax.broadcasted_iota(jnp.int32, shape, 1) % heads
        for g in range(k_ref.shape[0]):
            rs = slice(g * seq_rows, (g + 1) * seq_rows)
            kf = k_ref[g].reshape(mh, dh).astype(BF16)
            vf = v_ref[g].reshape(mh, dh).astype(BF16)
            qs = jnp.concatenate([q_ref[rs, h * dh:(h + 1) * dh] for h in range(heads)], axis=0)
            s = lax.dot_general(qs, kf, (((1,), (1,)), ((), ())), preferred_element_type=F32) * scale
            s = jnp.where(own, s, -jnp.inf)
            e = jnp.where(own, jnp.exp(s - jnp.max(s, axis=-1, keepdims=True)), 0.0)
            p = e / jnp.sum(e, axis=-1, keepdims=True)
            o = jnp.dot(p.astype(BF16), vf, preferred_element_type=F32).astype(o_ref.dtype)
            for h in range(heads):
                o_ref[rs, h * dh:(h + 1) * dh] = o[h * seq_rows:(h + 1) * seq_rows, :]
        return
    for g in range(k_ref.shape[0]):
        rs = slice(g * seq_rows, (g + 1) * seq_rows)
        for h in range(heads):
            sl = slice(h * dh, (h + 1) * dh)
            kh = k_ref[g, :, sl].astype(BF16)
            vh = v_ref[g, :, sl].astype(BF16)
            s = lax.dot_general(q_ref[rs, sl], kh, (((1,), (1,)), ((), ())),
                                preferred_element_type=F32) * scale
            e = jnp.exp(s - jnp.max(s, axis=-1, keepdims=True))
            p = e / jnp.sum(e, axis=-1, keepdims=True)
            o_ref[rs, sl] = jnp.dot(p.astype(BF16), vh, preferred_element_type=F32).astype(o_ref.dtype)


def _attn(proj, mem_k, mem_v, kv_off, *, n_streams, seq, width, q_blk, heads):
    mlen = mem_k.shape[1]
    rows = _tile(seq, ATTN_ROWS)
    nt = seq // rows
    per_step = 1
    if nt == 1:
        per_step = _tile(n_streams, max(1, ATTN_KV_BYTES // (mlen * width * 4)), 1)
    assert kv_off % per_step == 0
    blk = per_step * rows
    kv_shape = (per_step,) + mem_k.shape[1:]
    kv_spec = pl.BlockSpec(kv_shape, lambda s, t: (kv_off // per_step + s,) + (0,) * (len(kv_shape) - 1))
    return pl.pallas_call(
        functools.partial(_attn_kernel, heads=heads, seq_rows=rows),
        grid=(n_streams // per_step, nt),
        in_specs=[pl.BlockSpec((blk, width), lambda s, t: (s * nt + t, q_blk)), kv_spec, kv_spec],
        out_specs=pl.BlockSpec((blk, width), lambda s, t: (s * nt + t, 0)),
        out_shape=jax.ShapeDtypeStruct((n_streams * seq, width), BF16),
        compiler_params=_params(("parallel", "arbitrary")),
        name="attn",
    )(proj, mem_k, mem_v)


def _merge_m_tile(cs, zl_ref, zc_ref, zm_ref, g0_ref, g1_ref, g2_ref, wl_ref, wc_ref, wm_ref):
    m = g0_ref[:, cs].astype(F32) * jnp.dot(zl_ref[...], wl_ref[:, cs], preferred_element_type=F32)
    m = m + g1_ref[:, cs].astype(F32) * jnp.dot(zc_ref[...], wc_ref[:, cs], preferred_element_type=F32)
    m = m + g2_ref[:, cs].astype(F32) * jnp.dot(zm_ref[...], wm_ref[:, cs], preferred_element_type=F32)
    return m.astype(BF16)


def _merge_kernel(zl_ref, zc_ref, zm_ref, g0_ref, g1_ref, g2_ref, x_ref,
                  wl_ref, wc_ref, wm_ref, wo_ref, gp_ref, o_ref):
    m = _merge_m_tile(slice(None), zl_ref, zc_ref, zm_ref, g0_ref, g1_ref, g2_ref, wl_ref, wc_ref, wm_ref)
    y = jnp.dot(m, wo_ref[...], preferred_element_type=F32)
    o_ref[...] = x_ref[...] + _rms(y, gp_ref[...])


def _merge_weight_specs(l, width, d):
    once = pl.Buffered(1)
    return [_layer_spec(l, (width, d), once), _layer_spec(l, (width, d), once),
            _layer_spec(l, (width, d), once), _layer_spec(l, (d, d), once),
            _layer_spec(l, (1, d), once)]


def _merge_weight_args(wts):
    return (wts["w_branch_lru"], wts["w_branch_conv"], wts["w_branch_mem"], wts["w_out"], wts["g_mix_post"])


def _merge(zl, zc, zm, proj, x, wts, l, *, width):
    rows, d = x.shape
    tm = _tile(rows, MERGE_TM)
    gate_blk0 = (proj.shape[1] - N_BRANCH * d) // d

    def act(w):
        return pl.BlockSpec((tm, w), lambda i: (i, 0))

    def gate(b):
        return pl.BlockSpec((tm, d), lambda i: (i, gate_blk0 + b))

    return pl.pallas_call(
        _merge_kernel,
        grid=(rows // tm,),
        in_specs=[act(width), act(width), act(width), gate(0), gate(1), gate(2), act(d)]
        + _merge_weight_specs(l, width, d),
        out_specs=act(d),
        out_shape=jax.ShapeDtypeStruct((rows, d), F32),
        compiler_params=_params(("parallel",)),
        name="merge",
    )(zl, zc, zm, proj, proj, proj, x, *_merge_weight_args(wts))


def _mixmerge_kernel(lx_ref, gg_ref, sb_ref, sc_ref, sh_ref, h0_ref, lb0_ref, sb0_ref,
                     cw_ref, cb_ref, wa_ref, ba_ref, wi_ref, bi_ref, lam_ref, scw_ref,
                     zm_ref, g0_ref, g1_ref, g2_ref, x_ref, wl_ref, wc_ref, wm_ref, wo_ref, gp_ref,
                     o_ref, hn_ref, lbn_ref, sbn_ref,
                     xpad, cpad, acum, bcum, hcar, uscr, ya_s, yi_s, zl_s, zc_s, m_f, m_s, y_s,
                     *, blocks_per_stream, n_blocks):
    g = pl.program_id(0)
    kl = cw_ref.shape[0]

    @pl.when(g == 0)
    def _():
        for ref in (cpad, hcar, uscr, ya_s, yi_s):
            ref[...] = jnp.zeros(ref.shape, ref.dtype)

    @pl.when(jnp.logical_and(g % blocks_per_stream == 0, g < n_blocks))
    def _():
        xpad[0:PAD_ROWS, :] = jnp.zeros((PAD_ROWS, xpad.shape[1]), F32)
        xpad[PAD_ROWS - (kl - 1):PAD_ROWS, :] = lb0_ref[0]

    scan_valid = g >= 1

    @pl.when(jnp.logical_and((g - 1) % blocks_per_stream == 0, scan_valid))
    def _():
        cpad[0:PAD_ROWS, :] = jnp.zeros((PAD_ROWS, cpad.shape[1]), F32)
        cpad[PAD_ROWS - sb0_ref.shape[1]:PAD_ROWS, :] = sb0_ref[0]
        hcar[...] = h0_ref[0]

    rows, d = x_ref.shape
    n_chunks = wa_ref.shape[0]
    col_tiles = [slice(j, j + MERGE_TN) for j in range(0, d, MERGE_TN)]

    def gated(gate_ref, z_ref, w_ref, cs):
        return gate_ref[:, cs].astype(F32) * jnp.dot(z_ref[...], w_ref[:, cs], preferred_element_type=F32)

    tiles = rows // PACKED_ROWS
    vector_pieces = [functools.partial(_mix_scan_chunk, c, n_chunks, lam_ref, uscr, ya_s, yi_s, acum, bcum)
                     for c in range(n_chunks)]
    vector_pieces += [functools.partial(_mix_carry, k * tiles // 2, (k + 1) * tiles // 2,
                                        gg_ref, zl_s, acum, bcum, hcar, unroll=True) for k in range(2)]
    _mix_sconv(sb_ref, sc_ref, sh_ref, scw_ref, zc_s, cpad)
    for cs in col_tiles:
        m_f[:, cs] = gated(g2_ref, zm_ref, wm_ref, cs)
        if vector_pieces:
            vector_pieces.pop(0)()
    for cs in col_tiles:
        m_f[:, cs] += gated(g1_ref, zc_s, wc_ref, cs)
        if vector_pieces:
            vector_pieces.pop(0)()
    while vector_pieces:
        vector_pieces.pop(0)()
    for cs in col_tiles:
        m_s[:, cs] = (m_f[:, cs] + gated(g0_ref, zl_s, wl_ref, cs)).astype(BF16)
    for k, cs in enumerate(col_tiles):
        y_s[:, cs] = jnp.dot(m_s[...], wo_ref[:, cs], preferred_element_type=F32)
        if k == 0:
            _mix_lru_conv(lx_ref, cw_ref, cb_ref, xpad, uscr)
    _mix_gate_dots(uscr, wa_ref, ba_ref, wi_ref, bi_ref, ya_s, yi_s)
    o_ref[...] = x_ref[...] + _rms(y_s[...], gp_ref[...])

    @pl.when(g < n_blocks)
    def _():
        lbn_ref[0] = xpad[PAD_ROWS - (kl - 1):PAD_ROWS, :]

    @pl.when(scan_valid)
    def _():
        hn_ref[0] = hcar[...]
        sbn_ref[0] = cpad[PAD_ROWS - sbn_ref.shape[1]:PAD_ROWS, :]


def _mixmerge(proj, zm, x, h0, lb0, sb0, state_off, wts, l, *, n_streams, seq, width):
    rows, d = x.shape
    tm = _tile(seq, MERGE_TM, PACKED_ROWS)
    nt = seq // tm
    n = n_streams * nt
    kl, ks = wts["lru_conv_w"].shape[1], wts["sconv_w"].shape[1]
    gate_blk0 = (proj.shape[1] - N_BRANCH * d) // d

    def blk(lag):
        return lambda g: jnp.clip(g - lag, 0, n - 1)

    conv_blk, scan_blk, merge_blk = blk(0), blk(1), blk(1)

    def col(c, which):
        return pl.BlockSpec((tm, width), lambda g: (which(g), c))

    def state_in(k, which):
        return pl.BlockSpec((1, k, width), lambda g: (state_off + which(g) // nt, 0, 0))

    def state_out(k, which):
        return pl.BlockSpec((1, k, width), lambda g: (which(g) // nt, 0, 0))

    def gate(b):
        return pl.BlockSpec((tm, d), lambda g: (merge_blk(g), gate_blk0 + b))

    def act(w):
        return pl.BlockSpec((tm, w), lambda g: (merge_blk(g), 0))

    kern = functools.partial(_mixmerge_kernel, blocks_per_stream=nt, n_blocks=n)
    act_bf16 = pltpu.VMEM((tm, width), BF16)
    act_f32 = pltpu.VMEM((tm, width), F32)
    return pl.pallas_call(
        kern,
        grid=(n + 1,),
        in_specs=[col(0, conv_blk), col(1, scan_blk), col(2, scan_blk), col(3, scan_blk), col(4, scan_blk),
                  state_in(1, scan_blk), state_in(kl - 1, conv_blk), state_in(ks - 1, scan_blk)]
        + _mix_param_specs(wts, l, width)
        + [act(width), gate(0), gate(1), gate(2), act(d)] + _merge_weight_specs(l, width, d),
        out_specs=[act(d), state_out(1, scan_blk), state_out(kl - 1, conv_blk), state_out(ks - 1, scan_blk)],
        out_shape=[jax.ShapeDtypeStruct((rows, d), F32),
                   jax.ShapeDtypeStruct((n_streams, 1, width), F32),
                   jax.ShapeDtypeStruct((n_streams, kl - 1, width), F32),
                   jax.ShapeDtypeStruct((n_streams, ks - 1, width), F32)],
        scratch_shapes=_mix_scratch(tm, width) + [act_f32, act_f32, act_bf16, act_bf16, pltpu.VMEM((tm, d), F32),
                                                  pltpu.VMEM((tm, d), BF16), pltpu.VMEM((tm, d), F32)],
        compiler_params=_params(("arbitrary",)),
        name="mixmerge",
    )(proj, proj, proj, proj, proj, h0, lb0, sb0, *_mix_param_args(wts),
      zm, proj, proj, proj, x, *_merge_weight_args(wts))


def _mlp_kernel(x_ref, gpre_ref, wu_ref, wd_ref, gpost_ref, *rest, emit_next):
    if emit_next:
        gnext_ref, o_ref, xnext_ref, xn_ref, acc_ref = rest
    else:
        o_ref, xn_ref, acc_ref = rest
    f = pl.program_id(1)
    last = pl.num_programs(1) - 1
    tm = x_ref.shape[0]

    def partial_down(rows):
        hid = jnp.dot(xn_ref[rows, :], wu_ref[...], preferred_element_type=F32)
        hid = jnp.square(jnp.maximum(hid, 0.0)).astype(BF16)
        return jnp.dot(hid, wd_ref[...], preferred_element_type=F32)

    @pl.when(f == 0)
    def _():
        xn_ref[...] = _rms(x_ref[...], gpre_ref[...]).astype(BF16)
        acc_ref[...] = partial_down(slice(None))

    @pl.when(jnp.logical_and(f > 0, f < last))
    def _():
        acc_ref[...] += partial_down(slice(None))

    @pl.when(f == last)
    def _():
        for half in (slice(0, tm // 2), slice(tm // 2, tm)):
            y = acc_ref[half, :] + partial_down(half)
            out = x_ref[half, :] + _rms(y, gpost_ref[...])
            o_ref[half, :] = out
            if emit_next:
                xnext_ref[half, :] = _rms(out, gnext_ref[...]).astype(xnext_ref.dtype)


def _mlp(x, wts, l, *, emit_next):
    rows, d = x.shape
    dff = wts["w_up"].shape[2]
    tm, tf = _tile(rows, MLP_TM, 2 * SUBLANES), _tile(dff, MLP_TF, 128)
    assert dff // tf >= 2
    row_blk = pl.BlockSpec((tm, d), lambda i, f: (i, 0))
    in_specs = [row_blk,
                _layer_spec(l, (1, d)),
                pl.BlockSpec((None, d, tf), lambda i, f: (l, 0, f)),
                pl.BlockSpec((None, tf, d), lambda i, f: (l, f, 0)),
                _layer_spec(l, (1, d))]
    args = [x, wts["g_mlp_pre"], wts["w_up"], wts["w_down"], wts["g_mlp_post"]]
    out_specs, out_shape = [row_blk], [jax.ShapeDtypeStruct((rows, d), F32)]
    if emit_next:
        in_specs.append(_layer_spec(l + 1, (1, d)))
        args.append(wts["g_mix_pre"])
        out_specs.append(row_blk)
        out_shape.append(jax.ShapeDtypeStruct((rows, d), BF16))
    res = pl.pallas_call(
        functools.partial(_mlp_kernel, emit_next=emit_next),
        grid=(rows // tm, dff // tf),
        in_specs=in_specs,
        out_specs=out_specs,
        out_shape=out_shape,
        scratch_shapes=[pltpu.VMEM((tm, d), BF16), pltpu.VMEM((tm, d), F32)],
        compiler_params=_params(("parallel", "arbitrary")),
        name="mlp",
    )(*args)
    return (res[0], res[1]) if emit_next else (res[0], None)


def _block_diag_pack(w):
    depth, heads, dh, _ = w.shape
    per = max(1, min(heads, MXU_DIM // dh))
    n = heads // per
    wg = w.reshape(depth, n, per, dh, dh)
    eye = jnp.eye(per, dtype=w.dtype)
    out = jnp.einsum("lnhij,hg->lnhigj", wg, eye)
    return out.reshape(depth, n, per * dh, per * dh).astype(BF16)


def _run_layer(x, xn, mem_k, mem_v, kv_off, h0, lb0, sb0, state_off, wts, l,
               *, n_streams, seq, width, heads, last):
    proj = _proj(xn, wts, l, width=width)
    zm = _attn(proj, mem_k, mem_v, kv_off, n_streams=n_streams, seq=seq, width=width, q_blk=5, heads=heads)
    if seq >= MERGE_TM and seq % MERGE_TM == 0:
        x, hn, lbn, sbn = _mixmerge(proj, zm, x, h0, lb0, sb0, state_off, wts, l,
                                    n_streams=n_streams, seq=seq, width=width)
    else:
        zl, zc, hn, lbn, sbn = _mix(proj, h0, lb0, sb0, state_off, wts, l,
                                    n_streams=n_streams, seq=seq, width=width)
        x = _merge(zl, zc, zm, proj, x, wts, l, width=width)
    x, xn_next = _mlp(x, wts, l, emit_next=not last)
    return x, xn_next, hn, lbn, sbn


def kernel(x_prompt, x_sample, mem_prompt, state_lru_h, state_lru_conv, state_sconv, cache_mem_k, cache_mem_v, g_mix_pre, w_in, b_gate, lru_conv_w, lru_conv_b, lru_w_a, lru_b_a, lru_w_i, lru_b_i, lru_lambda, w_branch_lru, sconv_w, w_branch_conv, g_mem, w_mem_kv, w_branch_mem, w_out, g_mix_post, g_mlp_pre, w_up, w_down, g_mlp_post):
    depth = w_in.shape[0]
    bp, sp, d = x_prompt.shape
    bs, ss, _ = x_sample.shape
    width = lru_lambda.shape[1]
    mlen, heads, dh = cache_mem_k.shape[2:]
    kl, ks = lru_conv_w.shape[1], sconv_w.shape[1]

    def rowvec(p):
        return p.reshape(depth, 1, -1)

    wts = dict(
        g_mix_pre=rowvec(g_mix_pre), w_in=w_in.astype(BF16), b_gate=rowvec(b_gate),
        lru_conv_w=lru_conv_w, lru_conv_b=rowvec(lru_conv_b),
        wa_bd=_block_diag_pack(lru_w_a), lru_b_a=rowvec(lru_b_a),
        wi_bd=_block_diag_pack(lru_w_i), lru_b_i=rowvec(lru_b_i),
        lru_lambda=rowvec(lru_lambda), sconv_w=sconv_w,
        w_branch_lru=w_branch_lru.astype(BF16), w_branch_conv=w_branch_conv.astype(BF16),
        w_branch_mem=w_branch_mem.astype(BF16), w_out=w_out.astype(BF16),
        g_mix_post=rowvec(g_mix_post), g_mlp_pre=rowvec(g_mlp_pre),
        w_up=w_up.astype(BF16), w_down=w_down.astype(BF16), g_mlp_post=rowvec(g_mlp_post),
        g_mem=rowvec(g_mem), w_mem_kv=w_mem_kv.astype(BF16))

    hp = x_prompt.reshape(bp * sp, d)
    hs = x_sample.reshape(bs * ss, d)
    mem2d = mem_prompt.reshape(bp * mlen, d)
    p_h0 = jnp.zeros((bp, 1, width), F32)
    p_lb0 = jnp.zeros((bp, kl - 1, width), F32)
    p_sb0 = jnp.zeros((bp, ks - 1, width), F32)
    s_h0 = state_lru_h.reshape(depth * bs, 1, width)
    s_lb0 = state_lru_conv.reshape(depth * bs, kl - 1, width)
    s_sb0 = state_sconv.reshape(depth * bs, ks - 1, width)
    s_mk = cache_mem_k.reshape(depth * bs, mlen, heads, dh)
    s_mv = cache_mem_v.reshape(depth * bs, mlen, heads, dh)

    outs = {k: [] for k in ("p_h", "p_lb", "p_sb", "p_mk", "p_mv", "s_h", "s_lb", "s_sb")}
    hpn = _prenorm(hp, wts, 0)
    hsn = _prenorm(hs, wts, 0)
    for l in range(depth):
        last = l == depth - 1
        kv = _memkv(mem2d, wts, l, width=width)
        mk = kv[0].reshape(bp, mlen, width)
        mv = kv[1].reshape(bp, mlen, width)
        hp, hpn, hn, lbn, sbn = _run_layer(hp, hpn, mk, mv, 0, p_h0, p_lb0, p_sb0, 0, wts, l,
                                           n_streams=bp, seq=sp, width=width, heads=heads, last=last)
        outs["p_h"].append(hn.reshape(bp, width))
        outs["p_lb"].append(lbn)
        outs["p_sb"].append(sbn)
        outs["p_mk"].append(mk.reshape(bp, mlen, heads, dh))
        outs["p_mv"].append(mv.reshape(bp, mlen, heads, dh))
        hs, hsn, hn, lbn, sbn = _run_layer(hs, hsn, s_mk, s_mv, l * bs, s_h0, s_lb0, s_sb0, l * bs, wts, l,
                                           n_streams=bs, seq=ss, width=width, heads=heads, last=last)
        outs["s_h"].append(hn.reshape(bs, width))
        outs["s_lb"].append(lbn)
        outs["s_sb"].append(sbn)

    return (hp.reshape(bp, sp, d), hs.reshape(bs, ss, d),
            jnp.stack(outs["p_h"]), jnp.stack(outs["p_lb"]), jnp.stack(outs["p_sb"]),
            jnp.stack(outs["p_mk"]), jnp.stack(outs["p_mv"]),
            jnp.stack(outs["s_h"]), jnp.stack(outs["s_lb"]), jnp.stack(outs["s_sb"]))
```

```python
import functools

import jax
import jax.numpy as jnp
from jax import lax
from jax.experimental import pallas as pl
from jax.experimental.pallas import tpu as pltpu

F32 = jnp.float32
BF16 = jnp.bfloat16

EPS = 1e-6
LRU_C = 8.0
N_BRANCH = 3

SUBLANES = 8
PACKED_ROWS = 16
MXU_DIM = 256
VMEM_LIMIT_BYTES = 56 * 1024 * 1024

PROJ_TM = 1024
MIX_ROWS = 512
ATTN_ROWS = 512
ATTN_KV_BYTES = 4 * 1024 * 1024
MERGE_TM = 256
MERGE_TN = 512
MLP_TM, MLP_TF = 512, 1024
PAD_ROWS = SUBLANES
SCAN_ROWS = 32


def _tile(n, pref, mult=SUBLANES):
    if n <= pref:
        return n
    for t in range(pref, 0, -1):
        if n % t == 0 and t % mult == 0:
            return t
    raise ValueError(f"no tile for {n} <= {pref}")


def _params(sem):
    return pltpu.CompilerParams(dimension_semantics=sem, vmem_limit_bytes=VMEM_LIMIT_BYTES)


def _rms(x, g):
    var = jnp.mean(x * x, axis=-1, keepdims=True)
    return (x * lax.rsqrt(var + EPS)) * g


def _sigmoid(x):
    return 0.5 * jnp.tanh(0.5 * x) + 0.5


def _layer_spec(l, shape, pipeline_mode=None):
    nd = len(shape)
    kw = {} if pipeline_mode is None else {"pipeline_mode": pipeline_mode}
    return pl.BlockSpec((None,) + tuple(shape), lambda *_: (l,) + (0,) * nd, **kw)


def _split(n, parts_max, mult):
    for p in range(parts_max, 0, -1):
        if n % p == 0 and (n // p) % mult == 0:
            return p
    raise ValueError(f"cannot split {n} into <= {parts_max} parts of multiples of {mult}")


def _cast_specs(l, shape, grid, row_axis):
    r, c = shape
    col_axis = 1 - row_axis
    pr = _split(r, grid[row_axis], PACKED_ROWS)
    pc = _split(c, grid[col_axis], 128)
    blk = (r // pr, c // pc)

    def idx(*g):
        return jnp.minimum(g[row_axis], pr - 1), jnp.minimum(g[col_axis], pc - 1)

    return (pl.BlockSpec((None,) + blk, lambda *g: (l,) + idx(*g)),
            pl.BlockSpec((None,) + blk, lambda *g: (0,) + idx(*g)))


def _prenorm_kernel(x_ref, g_ref, o_ref):
    o_ref[...] = _rms(x_ref[...], g_ref[...]).astype(o_ref.dtype)


def _prenorm(x, wts, l):
    rows, d = x.shape
    tm = _tile(rows, PROJ_TM)
    return pl.pallas_call(
        _prenorm_kernel,
        grid=(rows // tm,),
        in_specs=[pl.BlockSpec((tm, d), lambda i: (i, 0)), _layer_spec(l, (1, d))],
        out_specs=pl.BlockSpec((tm, d), lambda i: (i, 0)),
        out_shape=jax.ShapeDtypeStruct((rows, d), BF16),
        compiler_params=_params(("parallel",)),
        name="prenorm",
    )(x, wts["g_mix_pre"])


def _proj_kernel(xn_ref, w_ref, bg_ref, *rest, gelu_blk, gate_blk0):
    n_cast = (len(rest) - 1) // 2
    cast_in, o_ref, cast_out = rest[:n_cast], rest[n_cast], rest[n_cast + 1:]
    j = pl.program_id(1)

    def dot():
        for src, dst in zip(cast_in, cast_out):
            dst[...] = src[...].astype(dst.dtype)
        return jnp.dot(xn_ref[...], w_ref[...], preferred_element_type=F32)

    @pl.when(j == gelu_blk)
    def _():
        o_ref[...] = jax.nn.gelu(dot()).astype(o_ref.dtype)

    @pl.when(j >= gate_blk0)
    def _():
        o_ref[...] = _sigmoid(dot() + bg_ref[...]).astype(o_ref.dtype)

    @pl.when(jnp.logical_and(j != gelu_blk, j < gate_blk0))
    def _():
        o_ref[...] = dot().astype(o_ref.dtype)


def _proj(xn, wts, l, *, width, cast=()):
    rows, d = xn.shape
    w_in, w_l = wts["w_in"]
    cols = w_in.shape[2]
    tm, tn = _tile(rows, PROJ_TM, PACKED_ROWS), width
    gate_blk0 = (cols - N_BRANCH * d) // tn
    grid = (rows // tm, cols // tn)
    kern = functools.partial(_proj_kernel, gelu_blk=1, gate_blk0=gate_blk0)
    cast_specs = [_cast_specs(l, w.shape[1:], grid, 0) for w in cast]
    res = pl.pallas_call(
        kern,
        grid=grid,
        in_specs=[
            pl.BlockSpec((tm, d), lambda i, j: (i, 0)),
            pl.BlockSpec((None, d, tn), lambda i, j: (w_l, 0, j)),
            pl.BlockSpec((None, 1, tn), lambda i, j: (l, 0, jnp.maximum(j - gate_blk0, 0))),
        ] + [cs[0] for cs in cast_specs],
        out_specs=[pl.BlockSpec((tm, tn), lambda i, j: (i, j))] + [cs[1] for cs in cast_specs],
        out_shape=[jax.ShapeDtypeStruct((rows, cols), BF16)]
        + [jax.ShapeDtypeStruct((1,) + w.shape[1:], BF16) for w in cast],
        compiler_params=_params(("arbitrary", "arbitrary")),
        name="proj",
    )(xn, w_in, wts["b_gate"], *cast)
    return res


def _memkv_kernel(x_ref, g_ref, w_ref, o_ref, xn_ref):
    @pl.when(pl.program_id(1) == 0)
    def _():
        xn_ref[...] = _rms(x_ref[...], g_ref[...]).astype(BF16)

    o_ref[...] = jnp.dot(xn_ref[...], w_ref[...], preferred_element_type=F32)


def _memkv(mem, wts, l, *, width):
    rows, d = mem.shape
    cols = wts["w_mem_kv"].shape[2]
    tm, tn = _tile(rows, PROJ_TM), width
    return pl.pallas_call(
        _memkv_kernel,
        grid=(rows // tm, cols // tn),
        in_specs=[
            pl.BlockSpec((tm, d), lambda i, j: (i, 0)),
            _layer_spec(l, (1, d)),
            pl.BlockSpec((None, d, tn), lambda i, j: (l, 0, j)),
        ],
        out_specs=pl.BlockSpec((None, tm, tn), lambda i, j: (j, i, 0)),
        out_shape=jax.ShapeDtypeStruct((cols // tn, rows, tn), F32),
        scratch_shapes=[pltpu.VMEM((tm, d), BF16)],
        compiler_params=_params(("parallel", "arbitrary")),
        name="memkv",
    )(mem, wts["g_mem"], wts["w_mem_kv"])


def _softplus(x):
    return jnp.maximum(x, 0.0) + jnp.log1p(jnp.exp(-jnp.abs(x)))


def _causal_conv(pad_ref, w_ref, rows):
    taps = w_ref.shape[0]
    xe = pad_ref[...]
    y = None
    for k in range(taps):
        back = taps - 1 - k
        xk = xe if back == 0 else pltpu.roll(xe, back, axis=0)
        term = xk[PAD_ROWS:PAD_ROWS + rows, :] * w_ref[k:k + 1, :]
        y = term if y is None else y + term
    return y


def _mix_init(h0_ref, lb0_ref, sb0_ref, xpad, cpad, hcar):
    width = xpad.shape[1]
    xpad[0:PAD_ROWS, :] = jnp.zeros((PAD_ROWS, width), F32)
    cpad[0:PAD_ROWS, :] = jnp.zeros((PAD_ROWS, width), F32)
    xpad[PAD_ROWS - lb0_ref.shape[1]:PAD_ROWS, :] = lb0_ref[0]
    cpad[PAD_ROWS - sb0_ref.shape[1]:PAD_ROWS, :] = sb0_ref[0]
    hcar[...] = h0_ref[0]


def _mix_state_out(hn_ref, lbn_ref, sbn_ref, xpad, cpad, hcar):
    hn_ref[0] = hcar[...]
    lbn_ref[0] = xpad[PAD_ROWS - lbn_ref.shape[1]:PAD_ROWS, :]
    sbn_ref[0] = cpad[PAD_ROWS - sbn_ref.shape[1]:PAD_ROWS, :]


def _mix_lru_conv(lx_ref, cw_ref, cb_ref, xpad, uscr):
    rows = lx_ref.shape[0]
    kl = cw_ref.shape[0]
    xpad[PAD_ROWS:PAD_ROWS + rows, :] = lx_ref[...].astype(F32)
    uscr[...] = _causal_conv(xpad, cw_ref, rows) + cb_ref[...]
    xpad[PAD_ROWS - (kl - 1):PAD_ROWS, :] = xpad[PAD_ROWS + rows - (kl - 1):PAD_ROWS + rows, :]


def _mix_gate_dots(uscr, wa_ref, ba_ref, wi_ref, bi_ref, acum, bcum):
    n_chunks = wa_ref.shape[0]
    bd = uscr.shape[1] // n_chunks
    for c in range(n_chunks):
        sl = slice(c * bd, (c + 1) * bd)
        ub = uscr[:, sl].astype(BF16)
        acum[:, sl] = jnp.dot(ub, wa_ref[c], preferred_element_type=F32) + ba_ref[:, sl]
        bcum[:, sl] = jnp.dot(ub, wi_ref[c], preferred_element_type=F32) + bi_ref[:, sl]


def _mix_sconv(sb_ref, sc_ref, sh_ref, scw_ref, zc_ref, cpad):
    rows = sc_ref.shape[0]
    ks = scw_ref.shape[0]
    cpad[PAD_ROWS:PAD_ROWS + rows, :] = sc_ref[...].astype(F32) * sh_ref[...].astype(F32)
    cv = _causal_conv(cpad, scw_ref, rows)
    zc_ref[...] = (sb_ref[...].astype(F32) * cv).astype(zc_ref.dtype)
    cpad[PAD_ROWS - (ks - 1):PAD_ROWS, :] = cpad[PAD_ROWS + rows - (ks - 1):PAD_ROWS + rows, :]


def _mix_scan_chunk(c, n_chunks, lam_ref, uscr, ya_s, yi_s, acum, bcum):
    rows, width = uscr.shape
    bd = width // n_chunks
    sl = slice(c * bd, (c + 1) * bd)
    neg_c_sp = -LRU_C * _softplus(-lam_ref[:, sl])
    tr = _tile(rows, SCAN_ROWS)
    groups = tr // SUBLANES
    row = lax.broadcasted_iota(jnp.int32, (groups, SUBLANES, bd), 1)
    for r0 in range(0, rows, tr):
        rs = slice(r0, r0 + tr)
        r = _sigmoid(ya_s[rs, sl])
        ig = _sigmoid(yi_s[rs, sl])
        log_a = neg_c_sp * r
        a = jnp.exp(log_a)
        th = jnp.tanh(log_a)
        mult = jnp.sqrt(-2.0 * th) * lax.rsqrt(1.0 - th)
        b = mult * (ig * uscr[rs, sl])
        a3 = a.reshape(groups, SUBLANES, bd)
        b3 = b.reshape(groups, SUBLANES, bd)
        d = 1
        while d < SUBLANES:
            a_prev = pltpu.roll(a3, d, axis=1)
            b_prev = pltpu.roll(b3, d, axis=1)
            take = row >= d
            b3 = jnp.where(take, a3 * b_prev + b3, b3)
            a3 = jnp.where(take, a3 * a_prev, a3)
            d *= 2
        acum[rs, sl] = a3.reshape(tr, bd)
        bcum[rs, sl] = b3.reshape(tr, bd)


def _mix_carry(lo, hi, gg_ref, zl_ref, acum, bcum, hcar, *, unroll):
    def body(g, h):
        r0 = pl.multiple_of(g * PACKED_ROWS, PACKED_ROWS)
        r1 = pl.multiple_of(r0 + SUBLANES, SUBLANES)
        h1 = acum[pl.ds(r0, SUBLANES), :] * h + bcum[pl.ds(r0, SUBLANES), :]
        h2 = acum[pl.ds(r1, SUBLANES), :] * h1[SUBLANES - 1:SUBLANES, :] + bcum[pl.ds(r1, SUBLANES), :]
        hh = jnp.concatenate([h1, h2], axis=0)
        gate = gg_ref[pl.ds(r0, PACKED_ROWS), :].astype(F32)
        zl_ref[pl.ds(r0, PACKED_ROWS), :] = (hh * gate).astype(zl_ref.dtype)
        return h2[SUBLANES - 1:SUBLANES, :]

    hcar[...] = lax.fori_loop(lo, hi, body, hcar[...], unroll=unroll)


def _mix_kernel(lx_ref, gg_ref, sb_ref, sc_ref, sh_ref, h0_ref, lb0_ref, sb0_ref,
                cw_ref, cb_ref, wa_ref, ba_ref, wi_ref, bi_ref, lam_ref, scw_ref,
                zl_ref, zc_ref, hn_ref, lbn_ref, sbn_ref,
                xpad, cpad, acum, bcum, hcar, uscr):
    @pl.when(pl.program_id(1) == 0)
    def _():
        _mix_init(h0_ref, lb0_ref, sb0_ref, xpad, cpad, hcar)

    _mix_lru_conv(lx_ref, cw_ref, cb_ref, xpad, uscr)
    _mix_gate_dots(uscr, wa_ref, ba_ref, wi_ref, bi_ref, acum, bcum)
    _mix_sconv(sb_ref, sc_ref, sh_ref, scw_ref, zc_ref, cpad)
    n_chunks = wa_ref.shape[0]
    for c in range(n_chunks):
        _mix_scan_chunk(c, n_chunks, lam_ref, uscr, acum, bcum, acum, bcum)
    _mix_carry(0, lx_ref.shape[0] // PACKED_ROWS, gg_ref, zl_ref, acum, bcum, hcar, unroll=False)
    _mix_state_out(hn_ref, lbn_ref, sbn_ref, xpad, cpad, hcar)


def _mix_scratch(rows, width):
    return [pltpu.VMEM((PAD_ROWS + rows, width), F32),
            pltpu.VMEM((PAD_ROWS + rows, width), F32),
            pltpu.VMEM((rows, width), F32),
            pltpu.VMEM((rows, width), F32),
            pltpu.VMEM((1, width), F32),
            pltpu.VMEM((rows, width), F32)]


def _mix_param_specs(wts, l, width):
    kl, ks = wts["lru_conv_w"].shape[1], wts["sconv_w"].shape[1]
    nck = wts["wa_bd"].shape[1]
    bd_spec = _layer_spec(l, (nck, width // nck, width // nck))
    return [_layer_spec(l, (kl, width)), _layer_spec(l, (1, width)),
            bd_spec, _layer_spec(l, (1, width)), bd_spec, _layer_spec(l, (1, width)),
            _layer_spec(l, (1, width)), _layer_spec(l, (ks, width))]


def _mix_param_args(wts):
    return (wts["lru_conv_w"], wts["lru_conv_b"], wts["wa_bd"], wts["lru_b_a"], wts["wi_bd"], wts["lru_b_i"],
            wts["lru_lambda"], wts["sconv_w"])


def _mix(proj, h0, lb0, sb0, state_off, wts, l, *, n_streams, seq, width):
    rows = _tile(seq, MIX_ROWS, PACKED_ROWS)
    nt = seq // rows
    kl, ks = wts["lru_conv_w"].shape[1], wts["sconv_w"].shape[1]

    def col(c):
        return pl.BlockSpec((rows, width), lambda s, t: (s * nt + t, c))

    def state_in(k):
        return pl.BlockSpec((1, k, width), lambda s, t: (state_off + s, 0, 0))

    def state_out(k):
        return pl.BlockSpec((1, k, width), lambda s, t: (s, 0, 0))

    total = n_streams * seq
    return pl.pallas_call(
        _mix_kernel,
        grid=(n_streams, nt),
        in_specs=[col(0), col(1), col(2), col(3), col(4),
                  state_in(1), state_in(kl - 1), state_in(ks - 1)] + _mix_param_specs(wts, l, width),
        out_specs=[pl.BlockSpec((rows, width), lambda s, t: (s * nt + t, 0)),
                   pl.BlockSpec((rows, width), lambda s, t: (s * nt + t, 0)),
                   state_out(1), state_out(kl - 1), state_out(ks - 1)],
        out_shape=[jax.ShapeDtypeStruct((total, width), BF16),
                   jax.ShapeDtypeStruct((total, width), BF16),
                   jax.ShapeDtypeStruct((n_streams, 1, width), F32),
                   jax.ShapeDtypeStruct((n_streams, kl - 1, width), F32),
                   jax.ShapeDtypeStruct((n_streams, ks - 1, width), F32)],
        scratch_shapes=_mix_scratch(rows, width),
        compiler_params=_params(("parallel", "arbitrary")),
        name="mix",
    )(proj, proj, proj, proj, proj, h0, lb0, sb0, *_mix_param_args(wts))


def _attn_kernel(q_ref, k_ref, v_ref, o_ref, *, heads, seq_rows):
    dh = q_ref.shape[1] // heads
    scale = dh ** -0.5
    if len(k_ref.shape) == 4:
        mh = k_ref.shape[1] * heads
        shape = (heads * seq_rows, mh)
        row = lax.broadcasted_iota(jnp.int32, shape, 0)
        row_head = sum((row >= h * seq_rows).astype(jnp.int32) for h in range(1, heads))
        own = row_head == lax.broadcasted_iota(jnp.int32, shape, 1) % heads
        for g in range(k_ref.shape[0]):
            rs = slice(g * seq_rows, (g + 1) * seq_rows)
            kf = k_ref[g].reshape(mh, dh).astype(BF16)
            vf = v_ref[g].reshape(mh, dh).astype(BF16)
            qs = jnp.concatenate([q_ref[rs, h * dh:(h + 1) * dh] for h in range(heads)], axis=0)
            s = lax.dot_general(qs, kf, (((1,), (1,)), ((), ())), preferred_element_type=F32) * scale
            s = jnp.where(own, s, -jnp.inf)
            e = jnp.where(own, jnp.exp(s - jnp.max(s, axis=-1, keepdims=True)), 0.0)
            p = e * (1.0 / jnp.sum(e, axis=-1, keepdims=True))
            o = jnp.dot(p.astype(BF16), vf, preferred_element_type=F32).astype(o_ref.dtype)
            for h in range(heads):
                o_ref[rs, h * dh:(h + 1) * dh] = o[h * seq_rows:(h + 1) * seq_rows, :]
        return
    for g in range(k_ref.shape[0]):
        rs = slice(g * seq_rows, (g + 1) * seq_rows)
        for h in range(heads):
            sl = slice(h * dh, (h + 1) * dh)
            kh = k_ref[g, :, sl].astype(BF16)
            vh = v_ref[g, :, sl].astype(BF16)
            s = lax.dot_general(q_ref[rs, sl], kh, (((1,), (1,)), ((), ())),
                                preferred_element_type=F32) * scale
            e = jnp.exp(s - jnp.max(s, axis=-1, keepdims=True))
            p = e * (1.0 / jnp.sum(e, axis=-1, keepdims=True))
            o_ref[rs, sl] = jnp.dot(p.astype(BF16), vh, preferred_element_type=F32).astype(o_ref.dtype)


def _attn(proj, mem_k, mem_v, kv_off, *, n_streams, seq, width, q_blk, heads):
    mlen = mem_k.shape[1]
    rows = _tile(seq, ATTN_ROWS)
    nt = seq // rows
    per_step = 1
    if nt == 1:
        per_step = _tile(n_streams, max(1, ATTN_KV_BYTES // (mlen * width * 4)), 1)
    assert kv_off % per_step == 0
    blk = per_step * rows
    kv_shape = (per_step,) + mem_k.shape[1:]
    kv_spec = pl.BlockSpec(kv_shape, lambda s, t: (kv_off // per_step + s,) + (0,) * (len(kv_shape) - 1))
    return pl.pallas_call(
        functools.partial(_attn_kernel, heads=heads, seq_rows=rows),
        grid=(n_streams // per_step, nt),
        in_specs=[pl.BlockSpec((blk, width), lambda s, t: (s * nt + t, q_blk)), kv_spec, kv_spec],
        out_specs=pl.BlockSpec((blk, width), lambda s, t: (s * nt + t, 0)),
        out_shape=jax.ShapeDtypeStruct((n_streams * seq, width), BF16),
        compiler_params=_params(("parallel", "arbitrary")),
        name="attn",
    )(proj, mem_k, mem_v)


def _merge_m_tile(cs, zl_ref, zc_ref, zm_ref, g0_ref, g1_ref, g2_ref, wl_ref, wc_ref, wm_ref):
    m = g0_ref[:, cs].astype(F32) * jnp.dot(zl_ref[...], wl_ref[:, cs], preferred_element_type=F32)
    m = m + g1_ref[:, cs].astype(F32) * jnp.dot(zc_ref[...], wc_ref[:, cs], preferred_element_type=F32)
    m = m + g2_ref[:, cs].astype(F32) * jnp.dot(zm_ref[...], wm_ref[:, cs], preferred_element_type=F32)
    return m.astype(BF16)


def _merge_kernel(zl_ref, zc_ref, zm_ref, g0_ref, g1_ref, g2_ref, x_ref,
                  wl_ref, wc_ref, wm_ref, wo_ref, gp_ref, o_ref):
    m = _merge_m_tile(slice(None), zl_ref, zc_ref, zm_ref, g0_ref, g1_ref, g2_ref, wl_ref, wc_ref, wm_ref)
    y = jnp.dot(m, wo_ref[...], preferred_element_type=F32)
    o_ref[...] = x_ref[...] + _rms(y, gp_ref[...])


def _merge_weight_specs(l, width, d):
    once = pl.Buffered(1)
    return [_layer_spec(l, (width, d), once), _layer_spec(l, (width, d), once),
            _layer_spec(l, (width, d), once), _layer_spec(l, (d, d), once),
            _layer_spec(l, (1, d), once)]


def _merge_weight_args(wts):
    return (wts["w_branch_lru"], wts["w_branch_conv"], wts["w_branch_mem"], wts["w_out"], wts["g_mix_post"])


def _merge(zl, zc, zm, proj, x, wts, l, *, width):
    rows, d = x.shape
    tm = _tile(rows, MERGE_TM)
    gate_blk0 = (proj.shape[1] - N_BRANCH * d) // d

    def act(w):
        return pl.BlockSpec((tm, w), lambda i: (i, 0))

    def gate(b):
        return pl.BlockSpec((tm, d), lambda i: (i, gate_blk0 + b))

    return pl.pallas_call(
        _merge_kernel,
        grid=(rows // tm,),
        in_specs=[act(width), act(width), act(width), gate(0), gate(1), gate(2), act(d)]
        + _merge_weight_specs(l, width, d),
        out_specs=act(d),
        out_shape=jax.ShapeDtypeStruct((rows, d), F32),
        compiler_params=_params(("parallel",)),
        name="merge",
    )(zl, zc, zm, proj, proj, proj, x, *_merge_weight_args(wts))


def _mixmerge_kernel(lx_ref, gg_ref, sb_ref, sc_ref, sh_ref, h0_ref, lb0_ref, sb0_ref,
                     cw_ref, cb_ref, wa_ref, ba_ref, wi_ref, bi_ref, lam_ref, scw_ref,
                     zm_ref, g0_ref, g1_ref, g2_ref, x_ref, wl_ref, wc_ref, wm_ref, wo_ref, gp_ref,
                     o_ref, hn_ref, lbn_ref, sbn_ref,
                     xpad, cpad, acum, bcum, hcar, uscr, ya_s, yi_s, zl_s, zc_s, m_f, m_s, y_s,
                     *, blocks_per_stream, n_blocks):
    g = pl.program_id(0)
    kl = cw_ref.shape[0]

    @pl.when(g == 0)
    def _():
        for ref in (cpad, hcar, uscr, ya_s, yi_s):
            ref[...] = jnp.zeros(ref.shape, ref.dtype)

    @pl.when(jnp.logical_and(g % blocks_per_stream == 0, g < n_blocks))
    def _():
        xpad[0:PAD_ROWS, :] = jnp.zeros((PAD_ROWS, xpad.shape[1]), F32)
        xpad[PAD_ROWS - (kl - 1):PAD_ROWS, :] = lb0_ref[0]

    scan_valid = g >= 1

    @pl.when(jnp.logical_and((g - 1) % blocks_per_stream == 0, scan_valid))
    def _():
        cpad[0:PAD_ROWS, :] = jnp.zeros((PAD_ROWS, cpad.shape[1]), F32)
        cpad[PAD_ROWS - sb0_ref.shape[1]:PAD_ROWS, :] = sb0_ref[0]
        hcar[...] = h0_ref[0]

    rows, d = x_ref.shape
    n_chunks = wa_ref.shape[0]
    col_tiles = [slice(j, j + MERGE_TN) for j in range(0, d, MERGE_TN)]

    def gated(gate_ref, z_ref, w_ref, cs):
        return gate_ref[:, cs].astype(F32) * jnp.dot(z_ref[...], w_ref[:, cs], preferred_element_type=F32)

    tiles = rows // PACKED_ROWS
    vector_pieces = [functools.partial(_mix_scan_chunk, c, n_chunks, lam_ref, uscr, ya_s, yi_s, acum, bcum)
                     for c in range(n_chunks)]
    vector_pieces += [functools.partial(_mix_carry, k * tiles // 2, (k + 1) * tiles // 2,
                                        gg_ref, zl_s, acum, bcum, hcar, unroll=True) for k in range(2)]
    _mix_sconv(sb_ref, sc_ref, sh_ref, scw_ref, zc_s, cpad)
    for cs in col_tiles:
        m_f[:, cs] = gated(g2_ref, zm_ref, wm_ref, cs)
        if vector_pieces:
            vector_pieces.pop(0)()
    for cs in col_tiles:
        m_f[:, cs] += gated(g1_ref, zc_s, wc_ref, cs)
        if vector_pieces:
            vector_pieces.pop(0)()
    while vector_pieces:
        vector_pieces.pop(0)()
    for cs in col_tiles:
        m_s[:, cs] = (m_f[:, cs] + gated(g0_ref, zl_s, wl_ref, cs)).astype(BF16)
    for k, cs in enumerate(col_tiles):
        y_s[:, cs] = jnp.dot(m_s[...], wo_ref[:, cs], preferred_element_type=F32)
        if k == 0:
            _mix_lru_conv(lx_ref, cw_ref, cb_ref, xpad, uscr)
    _mix_gate_dots(uscr, wa_ref, ba_ref, wi_ref, bi_ref, ya_s, yi_s)
    o_ref[...] = x_ref[...] + _rms(y_s[...], gp_ref[...])

    @pl.when(g < n_blocks)
    def _():
        lbn_ref[0] = xpad[PAD_ROWS - (kl - 1):PAD_ROWS, :]

    @pl.when(scan_valid)
    def _():
        hn_ref[0] = hcar[...]
        sbn_ref[0] = cpad[PAD_ROWS - sbn_ref.shape[1]:PAD_ROWS, :]


def _mixmerge(proj, zm, x, h0, lb0, sb0, state_off, wts, l, *, n_streams, seq, width):
    rows, d = x.shape
    tm = _tile(seq, MERGE_TM, PACKED_ROWS)
    nt = seq // tm
    n = n_streams * nt
    kl, ks = wts["lru_conv_w"].shape[1], wts["sconv_w"].shape[1]
    gate_blk0 = (proj.shape[1] - N_BRANCH * d) // d

    def blk(lag):
        return lambda g: jnp.clip(g - lag, 0, n - 1)

    conv_blk, scan_blk, merge_blk = blk(0), blk(1), blk(1)

    def col(c, which):
        return pl.BlockSpec((tm, width), lambda g: (which(g), c))

    def state_in(k, which):
        return pl.BlockSpec((1, k, width), lambda g: (state_off + which(g) // nt, 0, 0))

    def state_out(k, which):
        return pl.BlockSpec((1, k, width), lambda g: (which(g) // nt, 0, 0))

    def gate(b):
        return pl.BlockSpec((tm, d), lambda g: (merge_blk(g), gate_blk0 + b))

    def act(w):
        return pl.BlockSpec((tm, w), lambda g: (merge_blk(g), 0))

    kern = functools.partial(_mixmerge_kernel, blocks_per_stream=nt, n_blocks=n)
    act_bf16 = pltpu.VMEM((tm, width), BF16)
    act_f32 = pltpu.VMEM((tm, width), F32)
    return pl.pallas_call(
        kern,
        grid=(n + 1,),
        in_specs=[col(0, conv_blk), col(1, scan_blk), col(2, scan_blk), col(3, scan_blk), col(4, scan_blk),
                  state_in(1, scan_blk), state_in(kl - 1, conv_blk), state_in(ks - 1, scan_blk)]
        + _mix_param_specs(wts, l, width)
        + [act(width), gate(0), gate(1), gate(2), act(d)] + _merge_weight_specs(l, width, d),
        out_specs=[act(d), state_out(1, scan_blk), state_out(kl - 1, conv_blk), state_out(ks - 1, scan_blk)],
        out_shape=[jax.ShapeDtypeStruct((rows, d), F32),
                   jax.ShapeDtypeStruct((n_streams, 1, width), F32),
                   jax.ShapeDtypeStruct((n_streams, kl - 1, width), F32),
                   jax.ShapeDtypeStruct((n_streams, ks - 1, width), F32)],
        scratch_shapes=_mix_scratch(tm, width) + [act_f32, act_f32, act_bf16, act_bf16, pltpu.VMEM((tm, d), F32),
                                                  pltpu.VMEM((tm, d), BF16), pltpu.VMEM((tm, d), F32)],
        compiler_params=_params(("arbitrary",)),
        name="mixmerge",
    )(proj, proj, proj, proj, proj, h0, lb0, sb0, *_mix_param_args(wts),
      zm, proj, proj, proj, x, *_merge_weight_args(wts))


def _mlp_kernel(x_ref, gpre_ref, wu_ref, wd_ref, gpost_ref, *rest, emit_next, cast_next):
    rest = list(rest)
    gnext_ref = rest.pop(0) if emit_next else None
    wnext_ref = rest.pop(0) if cast_next else None
    o_ref = rest.pop(0)
    xnext_ref = rest.pop(0) if emit_next else None
    wnext_out_ref = rest.pop(0) if cast_next else None
    xn_ref, acc_ref = rest
    f = pl.program_id(1)
    last = pl.num_programs(1) - 1
    tm = x_ref.shape[0]

    def partial_down(rows):
        hid = jnp.dot(xn_ref[rows, :], wu_ref[...], preferred_element_type=F32)
        hid = jnp.square(jnp.maximum(hid, 0.0)).astype(BF16)
        return jnp.dot(hid, wd_ref[...], preferred_element_type=F32)

    def side_cast():
        if cast_next:
            wnext_out_ref[...] = wnext_ref[...].astype(wnext_out_ref.dtype)

    @pl.when(f == 0)
    def _():
        side_cast()
        xn_ref[...] = _rms(x_ref[...], gpre_ref[...]).astype(BF16)
        acc_ref[...] = partial_down(slice(None))

    @pl.when(jnp.logical_and(f > 0, f < last))
    def _():
        side_cast()
        acc_ref[...] += partial_down(slice(None))

    @pl.when(f == last)
    def _():
        side_cast()
        for half in (slice(0, tm // 2), slice(tm // 2, tm)):
            y = acc_ref[half, :] + partial_down(half)
            out = x_ref[half, :] + _rms(y, gpost_ref[...])
            o_ref[half, :] = out
            if emit_next:
                xnext_ref[half, :] = _rms(out, gnext_ref[...]).astype(xnext_ref.dtype)


def _mlp(x, wts, l, *, emit_next, w_next=None):
    rows, d = x.shape
    (w_up, up_l), (w_down, down_l) = wts["w_up"], wts["w_down"]
    dff = w_up.shape[2]
    tm, tf = _tile(rows, MLP_TM, 2 * SUBLANES), _tile(dff, MLP_TF, 128)
    assert dff // tf >= 2
    grid = (rows // tm, dff // tf)
    row_blk = pl.BlockSpec((tm, d), lambda i, f: (i, 0))
    in_specs = [row_blk,
                _layer_spec(l, (1, d)),
                pl.BlockSpec((None, d, tf), lambda i, f: (up_l, 0, f)),
                pl.BlockSpec((None, tf, d), lambda i, f: (down_l, f, 0)),
                _layer_spec(l, (1, d))]
    args = [x, wts["g_mlp_pre"], w_up, w_down, wts["g_mlp_post"]]
    out_specs, out_shape = [row_blk], [jax.ShapeDtypeStruct((rows, d), F32)]
    cast_next = w_next is not None
    if emit_next:
        in_specs.append(_layer_spec(l + 1, (1, d)))
        args.append(wts["g_mix_pre"])
        out_specs.append(row_blk)
        out_shape.append(jax.ShapeDtypeStruct((rows, d), BF16))
    if cast_next:
        cast_in, cast_out = _cast_specs(l + 1, w_next.shape[1:], grid, 1)
        in_specs.append(cast_in)
        args.append(w_next)
        out_specs.append(cast_out)
        out_shape.append(jax.ShapeDtypeStruct((1,) + w_next.shape[1:], BF16))
    res = list(pl.pallas_call(
        functools.partial(_mlp_kernel, emit_next=emit_next, cast_next=cast_next),
        grid=grid,
        in_specs=in_specs,
        out_specs=out_specs,
        out_shape=out_shape,
        scratch_shapes=[pltpu.VMEM((tm, d), BF16), pltpu.VMEM((tm, d), F32)],
        compiler_params=_params(("arbitrary", "arbitrary")),
        name="mlp",
    )(*args))
    x_new = res.pop(0)
    xn_next = res.pop(0) if emit_next else None
    w_next_bf16 = res.pop(0) if cast_next else None
    return x_new, xn_next, w_next_bf16


def _block_diag_pack(w):
    depth, heads, dh, _ = w.shape
    per = max(1, min(heads, MXU_DIM // dh))
    n = heads // per
    wg = w.reshape(depth, n, per, dh, dh)
    eye = jnp.eye(per, dtype=w.dtype)
    out = jnp.einsum("lnhij,hg->lnhigj", wg, eye)
    return out.reshape(depth, n, per * dh, per * dh).astype(BF16)


def _run_layer(x, xn, mem_k, mem_v, kv_off, h0, lb0, sb0, state_off, wts, l,
               *, n_streams, seq, width, heads, last, cast=None):
    if cast is not None:
        proj, w_up_bf, w_down_bf = _proj(xn, wts, l, width=width, cast=cast[:2])
        wts = dict(wts, w_up=(w_up_bf, 0), w_down=(w_down_bf, 0))
    else:
        proj, = _proj(xn, wts, l, width=width)
    zm = _attn(proj, mem_k, mem_v, kv_off, n_streams=n_streams, seq=seq, width=width, q_blk=5, heads=heads)
    if seq >= MERGE_TM and seq % MERGE_TM == 0:
        x, hn, lbn, sbn = _mixmerge(proj, zm, x, h0, lb0, sb0, state_off, wts, l,
                                    n_streams=n_streams, seq=seq, width=width)
    else:
        zl, zc, hn, lbn, sbn = _mix(proj, h0, lb0, sb0, state_off, wts, l,
                                    n_streams=n_streams, seq=seq, width=width)
        x = _merge(zl, zc, zm, proj, x, wts, l, width=width)
    w_next = cast[2] if cast is not None and not last else None
    x, xn_next, w_in_next = _mlp(x, wts, l, emit_next=not last, w_next=w_next)
    return x, xn_next, hn, lbn, sbn, wts, w_in_next


def kernel(x_prompt, x_sample, mem_prompt, state_lru_h, state_lru_conv, state_sconv, cache_mem_k, cache_mem_v, g_mix_pre, w_in, b_gate, lru_conv_w, lru_conv_b, lru_w_a, lru_b_a, lru_w_i, lru_b_i, lru_lambda, w_branch_lru, sconv_w, w_branch_conv, g_mem, w_mem_kv, w_branch_mem, w_out, g_mix_post, g_mlp_pre, w_up, w_down, g_mlp_post):
    depth = w_in.shape[0]
    bp, sp, d = x_prompt.shape
    bs, ss, _ = x_sample.shape
    width = lru_lambda.shape[1]
    mlen, heads, dh = cache_mem_k.shape[2:]
    kl, ks = lru_conv_w.shape[1], sconv_w.shape[1]

    def rowvec(p):
        return p.reshape(depth, 1, -1)

    wts = dict(
        g_mix_pre=rowvec(g_mix_pre), w_in=(w_in[0:1].astype(BF16), 0), b_gate=rowvec(b_gate),
        lru_conv_w=lru_conv_w, lru_conv_b=rowvec(lru_conv_b),
        wa_bd=_block_diag_pack(lru_w_a), lru_b_a=rowvec(lru_b_a),
        wi_bd=_block_diag_pack(lru_w_i), lru_b_i=rowvec(lru_b_i),
        lru_lambda=rowvec(lru_lambda), sconv_w=sconv_w,
        w_branch_lru=w_branch_lru.astype(BF16), w_branch_conv=w_branch_conv.astype(BF16),
        w_branch_mem=w_branch_mem.astype(BF16), w_out=w_out.astype(BF16),
        g_mix_post=rowvec(g_mix_post), g_mlp_pre=rowvec(g_mlp_pre),
        g_mlp_post=rowvec(g_mlp_post),
        g_mem=rowvec(g_mem), w_mem_kv=w_mem_kv.astype(BF16))

    hp = x_prompt.reshape(bp * sp, d)
    hs = x_sample.reshape(bs * ss, d)
    mem2d = mem_prompt.reshape(bp * mlen, d)
    p_h0 = jnp.zeros((bp, 1, width), F32)
    p_lb0 = jnp.zeros((bp, kl - 1, width), F32)
    p_sb0 = jnp.zeros((bp, ks - 1, width), F32)
    s_h0 = state_lru_h.reshape(depth * bs, 1, width)
    s_lb0 = state_lru_conv.reshape(depth * bs, kl - 1, width)
    s_sb0 = state_sconv.reshape(depth * bs, ks - 1, width)
    s_mk = cache_mem_k.reshape(depth * bs, mlen, heads, dh)
    s_mv = cache_mem_v.reshape(depth * bs, mlen, heads, dh)

    outs = {k: [] for k in ("p_h", "p_lb", "p_sb", "p_mk", "p_mv", "s_h", "s_lb", "s_sb")}
    hpn = _prenorm(hp, wts, 0)
    hsn = _prenorm(hs, wts, 0)
    for l in range(depth):
        last = l == depth - 1
        kv = _memkv(mem2d, wts, l, width=width)
        mk = kv[0].reshape(bp, mlen, width)
        mv = kv[1].reshape(bp, mlen, width)
        hp, hpn, hn, lbn, sbn, wts, w_in_next = _run_layer(
            hp, hpn, mk, mv, 0, p_h0, p_lb0, p_sb0, 0, wts, l,
            n_streams=bp, seq=sp, width=width, heads=heads, last=last, cast=(w_up, w_down, w_in))
        outs["p_h"].append(hn.reshape(bp, width))
        outs["p_lb"].append(lbn)
        outs["p_sb"].append(sbn)
        outs["p_mk"].append(mk.reshape(bp, mlen, heads, dh))
        outs["p_mv"].append(mv.reshape(bp, mlen, heads, dh))
        hs, hsn, hn, lbn, sbn, _, _ = _run_layer(
            hs, hsn, s_mk, s_mv, l * bs, s_h0, s_lb0, s_sb0, l * bs, wts, l,
            n_streams=bs, seq=ss, width=width, heads=heads, last=last)
        outs["s_h"].append(hn.reshape(bs, width))
        outs["s_lb"].append(lbn)
        outs["s_sb"].append(sbn)
        if not last:
            wts = dict(wts, w_in=(w_in_next, 0))

    return (hp.reshape(bp, sp, d), hs.reshape(bs, ss, d),
            jnp.stack(outs["p_h"]), jnp.stack(outs["p_lb"]), jnp.stack(outs["p_sb"]),
            jnp.stack(outs["p_mk"]), jnp.stack(outs["p_mv"]),
            jnp.stack(outs["s_h"]), jnp.stack(outs["s_lb"]), jnp.stack(outs["s_sb"]))
```

```python
import functools

import jax
import jax.numpy as jnp
from jax import lax
from jax.experimental import pallas as pl
from jax.experimental.pallas import tpu as pltpu

F32 = jnp.float32
BF16 = jnp.bfloat16

EPS = 1e-6
LRU_C = 8.0
N_BRANCH = 3

SUBLANES = 8
PACKED_ROWS = 16
MXU_DIM = 256
VMEM_LIMIT_BYTES = 56 * 1024 * 1024

PROJ_TM = 1024
MIX_ROWS = 512
ATTN_ROWS = 512
ATTN_KV_BYTES = 4 * 1024 * 1024
MERGE_TM = 256
MERGE_TN = 512
MLP_TM, MLP_TF = 512, 1024
PAD_ROWS = SUBLANES
SCAN_ROWS = 32


def _tile(n, pref, mult=SUBLANES):
    if n <= pref:
        return n
    for t in range(pref, 0, -1):
        if n % t == 0 and t % mult == 0:
            return t
    raise ValueError(f"no tile for {n} <= {pref}")


def _params(sem):
    return pltpu.CompilerParams(dimension_semantics=sem, vmem_limit_bytes=VMEM_LIMIT_BYTES)


def _rms(x, g):
    var = jnp.mean(x * x, axis=-1, keepdims=True)
    return (x * lax.rsqrt(var + EPS)) * g


def _sigmoid(x):
    return 0.5 * jnp.tanh(0.5 * x) + 0.5


def _layer_spec(l, shape, pipeline_mode=None):
    nd = len(shape)
    kw = {} if pipeline_mode is None else {"pipeline_mode": pipeline_mode}
    return pl.BlockSpec((None,) + tuple(shape), lambda *_: (l,) + (0,) * nd, **kw)


def _split(n, parts_max, mult):
    for p in range(parts_max, 0, -1):
        if n % p == 0 and (n // p) % mult == 0:
            return p
    raise ValueError(f"cannot split {n} into <= {parts_max} parts of multiples of {mult}")


def _cast_specs(l, shape, grid, row_axis):
    r, c = shape
    col_axis = 1 - row_axis
    pr = _split(r, grid[row_axis], PACKED_ROWS)
    pc = _split(c, grid[col_axis], 128)
    blk = (r // pr, c // pc)

    def idx(*g):
        return jnp.minimum(g[row_axis], pr - 1), jnp.minimum(g[col_axis], pc - 1)

    return (pl.BlockSpec((None,) + blk, lambda *g: (l,) + idx(*g)),
            pl.BlockSpec((None,) + blk, lambda *g: (0,) + idx(*g)))


def _proj_kernel(x_ref, g_ref, w_ref, bg_ref, *rest, gelu_blk, gate_blk0):
    n_cast = (len(rest) - 2) // 2
    cast_in, o_ref, cast_out, xn_ref = rest[:n_cast], rest[n_cast], rest[n_cast + 1:-1], rest[-1]
    j = pl.program_id(1)

    @pl.when(j == 0)
    def _():
        xn_ref[...] = _rms(x_ref[...], g_ref[...]).astype(BF16)

    def dot():
        for src, dst in zip(cast_in, cast_out):
            dst[...] = src[...].astype(dst.dtype)
        return jnp.dot(xn_ref[...], w_ref[...], preferred_element_type=F32)

    @pl.when(j == gelu_blk)
    def _():
        o_ref[...] = jax.nn.gelu(dot()).astype(o_ref.dtype)

    @pl.when(j >= gate_blk0)
    def _():
        o_ref[...] = _sigmoid(dot() + bg_ref[...]).astype(o_ref.dtype)

    @pl.when(jnp.logical_and(j != gelu_blk, j < gate_blk0))
    def _():
        o_ref[...] = dot().astype(o_ref.dtype)


def _proj(x, wts, l, *, width, cast=()):
    rows, d = x.shape
    w_in, w_l = wts["w_in"]
    cols = w_in.shape[2]
    tm, tn = _tile(rows, PROJ_TM), width
    gate_blk0 = (cols - N_BRANCH * d) // tn
    grid = (rows // tm, cols // tn)
    kern = functools.partial(_proj_kernel, gelu_blk=1, gate_blk0=gate_blk0)
    cast_specs = [_cast_specs(l, w.shape[1:], grid, 0) for w in cast]
    return pl.pallas_call(
        kern,
        grid=grid,
        in_specs=[
            pl.BlockSpec((tm, d), lambda i, j: (i, 0)),
            _layer_spec(l, (1, d)),
            pl.BlockSpec((None, d, tn), lambda i, j: (w_l, 0, j)),
            pl.BlockSpec((None, 1, tn), lambda i, j: (l, 0, jnp.maximum(j - gate_blk0, 0))),
        ] + [cs[0] for cs in cast_specs],
        out_specs=[pl.BlockSpec((tm, tn), lambda i, j: (i, j))] + [cs[1] for cs in cast_specs],
        out_shape=[jax.ShapeDtypeStruct((rows, cols), BF16)]
        + [jax.ShapeDtypeStruct((1,) + w.shape[1:], BF16) for w in cast],
        scratch_shapes=[pltpu.VMEM((tm, d), BF16)],
        compiler_params=_params(("arbitrary", "arbitrary")),
        name="proj",
    )(x, wts["g_mix_pre"], w_in, wts["b_gate"], *cast)


def _memkv_kernel(x_ref, g_ref, w_ref, o_ref, xn_ref):
    @pl.when(pl.program_id(1) == 0)
    def _():
        xn_ref[...] = _rms(x_ref[...], g_ref[...]).astype(BF16)

    o_ref[...] = jnp.dot(xn_ref[...], w_ref[...], preferred_element_type=F32)


def _memkv(mem, wts, l, *, width):
    rows, d = mem.shape
    cols = wts["w_mem_kv"].shape[2]
    tm, tn = _tile(rows, PROJ_TM), width
    return pl.pallas_call(
        _memkv_kernel,
        grid=(rows // tm, cols // tn),
        in_specs=[
            pl.BlockSpec((tm, d), lambda i, j: (i, 0)),
            _layer_spec(l, (1, d)),
            pl.BlockSpec((None, d, tn), lambda i, j: (l, 0, j)),
        ],
        out_specs=pl.BlockSpec((None, tm, tn), lambda i, j: (j, i, 0)),
        out_shape=jax.ShapeDtypeStruct((cols // tn, rows, tn), F32),
        scratch_shapes=[pltpu.VMEM((tm, d), BF16)],
        compiler_params=_params(("parallel", "arbitrary")),
        name="memkv",
    )(mem, wts["g_mem"], wts["w_mem_kv"])


def _softplus(x):
    return jnp.maximum(x, 0.0) + jnp.log1p(jnp.exp(-jnp.abs(x)))


def _causal_conv(pad_ref, w_ref, rows):
    taps = w_ref.shape[0]
    xe = pad_ref[...]
    y = None
    for k in range(taps):
        back = taps - 1 - k
        xk = xe if back == 0 else pltpu.roll(xe, back, axis=0)
        term = xk[PAD_ROWS:PAD_ROWS + rows, :] * w_ref[k:k + 1, :]
        y = term if y is None else y + term
    return y


def _mix_init(h0_ref, lb0_ref, sb0_ref, xpad, cpad, hcar):
    width = xpad.shape[1]
    xpad[0:PAD_ROWS, :] = jnp.zeros((PAD_ROWS, width), F32)
    cpad[0:PAD_ROWS, :] = jnp.zeros((PAD_ROWS, width), F32)
    xpad[PAD_ROWS - lb0_ref.shape[1]:PAD_ROWS, :] = lb0_ref[0]
    cpad[PAD_ROWS - sb0_ref.shape[1]:PAD_ROWS, :] = sb0_ref[0]
    hcar[...] = h0_ref[0]


def _mix_state_out(hn_ref, lbn_ref, sbn_ref, xpad, cpad, hcar):
    hn_ref[0] = hcar[...]
    lbn_ref[0] = xpad[PAD_ROWS - lbn_ref.shape[1]:PAD_ROWS, :]
    sbn_ref[0] = cpad[PAD_ROWS - sbn_ref.shape[1]:PAD_ROWS, :]


def _mix_lru_conv(lx_ref, cw_ref, cb_ref, xpad, uscr):
    rows = lx_ref.shape[0]
    kl = cw_ref.shape[0]
    xpad[PAD_ROWS:PAD_ROWS + rows, :] = lx_ref[...].astype(F32)
    uscr[...] = _causal_conv(xpad, cw_ref, rows) + cb_ref[...]
    xpad[PAD_ROWS - (kl - 1):PAD_ROWS, :] = xpad[PAD_ROWS + rows - (kl - 1):PAD_ROWS + rows, :]


def _mix_gate_dots(uscr, wa_ref, ba_ref, wi_ref, bi_ref, acum, bcum):
    n_chunks = wa_ref.shape[0]
    bd = uscr.shape[1] // n_chunks
    for c in range(n_chunks):
        sl = slice(c * bd, (c + 1) * bd)
        ub = uscr[:, sl].astype(BF16)
        acum[:, sl] = jnp.dot(ub, wa_ref[c], preferred_element_type=F32) + ba_ref[:, sl]
        bcum[:, sl] = jnp.dot(ub, wi_ref[c], preferred_element_type=F32) + bi_ref[:, sl]


def _mix_sconv(sb_ref, sc_ref, sh_ref, scw_ref, zc_ref, cpad):
    rows = sc_ref.shape[0]
    ks = scw_ref.shape[0]
    cpad[PAD_ROWS:PAD_ROWS + rows, :] = sc_ref[...].astype(F32) * sh_ref[...].astype(F32)
    cv = _causal_conv(cpad, scw_ref, rows)
    zc_ref[...] = (sb_ref[...].astype(F32) * cv).astype(zc_ref.dtype)
    cpad[PAD_ROWS - (ks - 1):PAD_ROWS, :] = cpad[PAD_ROWS + rows - (ks - 1):PAD_ROWS + rows, :]


def _mix_scan_chunk(c, n_chunks, lam_ref, uscr, ya_s, yi_s, acum, bcum):
    rows, width = uscr.shape
    bd = width // n_chunks
    sl = slice(c * bd, (c + 1) * bd)
    neg_c_sp = -LRU_C * _softplus(-lam_ref[:, sl])
    tr = _tile(rows, SCAN_ROWS)
    groups = tr // SUBLANES
    row = lax.broadcasted_iota(jnp.int32, (groups, SUBLANES, bd), 1)
    for r0 in range(0, rows, tr):
        rs = slice(r0, r0 + tr)
        r = _sigmoid(ya_s[rs, sl])
        ig = _sigmoid(yi_s[rs, sl])
        log_a = neg_c_sp * r
        a = jnp.exp(log_a)
        th = jnp.tanh(log_a)
        mult = jnp.sqrt(-2.0 * th) * lax.rsqrt(1.0 - th)
        b = mult * (ig * uscr[rs, sl])
        a3 = a.reshape(groups, SUBLANES, bd)
        b3 = b.reshape(groups, SUBLANES, bd)
        d = 1
        while d < SUBLANES:
            a_prev = pltpu.roll(a3, d, axis=1)
            b_prev = pltpu.roll(b3, d, axis=1)
            take = row >= d
            b3 = jnp.where(take, a3 * b_prev + b3, b3)
            a3 = jnp.where(take, a3 * a_prev, a3)
            d *= 2
        acum[rs, sl] = a3.reshape(tr, bd)
        bcum[rs, sl] = b3.reshape(tr, bd)


def _mix_carry(lo, hi, gg_ref, zl_ref, acum, bcum, hcar, *, unroll):
    def body(g, h):
        r0 = pl.multiple_of(g * PACKED_ROWS, PACKED_ROWS)
        r1 = pl.multiple_of(r0 + SUBLANES, SUBLANES)
        h1 = acum[pl.ds(r0, SUBLANES), :] * h + bcum[pl.ds(r0, SUBLANES), :]
        h2 = acum[pl.ds(r1, SUBLANES), :] * h1[SUBLANES - 1:SUBLANES, :] + bcum[pl.ds(r1, SUBLANES), :]
        hh = jnp.concatenate([h1, h2], axis=0)
        gate = gg_ref[pl.ds(r0, PACKED_ROWS), :].astype(F32)
        zl_ref[pl.ds(r0, PACKED_ROWS), :] = (hh * gate).astype(zl_ref.dtype)
        return h2[SUBLANES - 1:SUBLANES, :]

    hcar[...] = lax.fori_loop(lo, hi, body, hcar[...], unroll=unroll)


def _mix_kernel(lx_ref, gg_ref, sb_ref, sc_ref, sh_ref, h0_ref, lb0_ref, sb0_ref,
                cw_ref, cb_ref, wa_ref, ba_ref, wi_ref, bi_ref, lam_ref, scw_ref,
                zl_ref, zc_ref, hn_ref, lbn_ref, sbn_ref,
                xpad, cpad, acum, bcum, hcar, uscr):
    @pl.when(pl.program_id(1) == 0)
    def _():
        _mix_init(h0_ref, lb0_ref, sb0_ref, xpad, cpad, hcar)

    _mix_lru_conv(lx_ref, cw_ref, cb_ref, xpad, uscr)
    _mix_gate_dots(uscr, wa_ref, ba_ref, wi_ref, bi_ref, acum, bcum)
    _mix_sconv(sb_ref, sc_ref, sh_ref, scw_ref, zc_ref, cpad)
    n_chunks = wa_ref.shape[0]
    for c in range(n_chunks):
        _mix_scan_chunk(c, n_chunks, lam_ref, uscr, acum, bcum, acum, bcum)
    _mix_carry(0, lx_ref.shape[0] // PACKED_ROWS, gg_ref, zl_ref, acum, bcum, hcar, unroll=False)
    _mix_state_out(hn_ref, lbn_ref, sbn_ref, xpad, cpad, hcar)


def _mix_scratch(rows, width):
    return [pltpu.VMEM((PAD_ROWS + rows, width), F32),
            pltpu.VMEM((PAD_ROWS + rows, width), F32),
            pltpu.VMEM((rows, width), F32),
            pltpu.VMEM((rows, width), F32),
            pltpu.VMEM((1, width), F32),
            pltpu.VMEM((rows, width), F32)]


def _mix_param_specs(wts, l, width):
    kl, ks = wts["lru_conv_w"].shape[1], wts["sconv_w"].shape[1]
    nck = wts["wa_bd"].shape[1]
    bd_spec = _layer_spec(l, (nck, width // nck, width // nck))
    return [_layer_spec(l, (kl, width)), _layer_spec(l, (1, width)),
            bd_spec, _layer_spec(l, (1, width)), bd_spec, _layer_spec(l, (1, width)),
            _layer_spec(l, (1, width)), _layer_spec(l, (ks, width))]


def _mix_param_args(wts):
    return (wts["lru_conv_w"], wts["lru_conv_b"], wts["wa_bd"], wts["lru_b_a"], wts["wi_bd"], wts["lru_b_i"],
            wts["lru_lambda"], wts["sconv_w"])


def _mix(proj, h0, lb0, sb0, state_off, wts, l, *, n_streams, seq, width):
    rows = _tile(seq, MIX_ROWS, PACKED_ROWS)
    nt = seq // rows
    kl, ks = wts["lru_conv_w"].shape[1], wts["sconv_w"].shape[1]

    def col(c):
        return pl.BlockSpec((rows, width), lambda s, t: (s * nt + t, c))

    def state_in(k):
        return pl.BlockSpec((1, k, width), lambda s, t: (state_off + s, 0, 0))

    def state_out(k):
        return pl.BlockSpec((1, k, width), lambda s, t: (s, 0, 0))

    total = n_streams * seq
    return pl.pallas_call(
        _mix_kernel,
        grid=(n_streams, nt),
        in_specs=[col(0), col(1), col(2), col(3), col(4),
                  state_in(1), state_in(kl - 1), state_in(ks - 1)] + _mix_param_specs(wts, l, width),
        out_specs=[pl.BlockSpec((rows, width), lambda s, t: (s * nt + t, 0)),
                   pl.BlockSpec((rows, width), lambda s, t: (s * nt + t, 0)),
                   state_out(1), state_out(kl - 1), state_out(ks - 1)],
        out_shape=[jax.ShapeDtypeStruct((total, width), BF16),
                   jax.ShapeDtypeStruct((total, width), BF16),
                   jax.ShapeDtypeStruct((n_streams, 1, width), F32),
                   jax.ShapeDtypeStruct((n_streams, kl - 1, width), F32),
                   jax.ShapeDtypeStruct((n_streams, ks - 1, width), F32)],
        scratch_shapes=_mix_scratch(rows, width),
        compiler_params=_params(("parallel", "arbitrary")),
        name="mix",
    )(proj, proj, proj, proj, proj, h0, lb0, sb0, *_mix_param_args(wts))


def _attn_kernel(q_ref, k_ref, v_ref, o_ref, *, heads, seq_rows):
    dh = q_ref.shape[1] // heads
    scale = dh ** -0.5
    if len(k_ref.shape) == 4:
        mh = k_ref.shape[1] * heads
        shape = (heads * seq_rows, mh)
        row = lax.broadcasted_iota(jnp.int32, shape, 0)
        row_head = sum((row >= h * seq_rows).astype(jnp.int32) for h in range(1, heads))
        own = row_head == lax.broadcasted_iota(jnp.int32, shape, 1) % heads
        for g in range(k_ref.shape[0]):
            rs = slice(g * seq_rows, (g + 1) * seq_rows)
            kf = k_ref[g].reshape(mh, dh).astype(BF16)
            vf = v_ref[g].reshape(mh, dh).astype(BF16)
            qs = jnp.concatenate([q_ref[rs, h * dh:(h + 1) * dh] for h in range(heads)], axis=0)
            s = lax.dot_general(qs, kf, (((1,), (1,)), ((), ())), preferred_element_type=F32) * scale
            s = jnp.where(own, s, -jnp.inf)
            e = jnp.where(own, jnp.exp(s - jnp.max(s, axis=-1, keepdims=True)), 0.0)
            p = e * (1.0 / jnp.sum(e, axis=-1, keepdims=True))
            o = jnp.dot(p.astype(BF16), vf, preferred_element_type=F32).astype(o_ref.dtype)
            for h in range(heads):
                o_ref[rs, h * dh:(h + 1) * dh] = o[h * seq_rows:(h + 1) * seq_rows, :]
        return
    for g in range(k_ref.shape[0]):
        rs = slice(g * seq_rows, (g + 1) * seq_rows)
        for h in range(heads):
            sl = slice(h * dh, (h + 1) * dh)
            kh = k_ref[g, :, sl].astype(BF16)
            vh = v_ref[g, :, sl].astype(BF16)
            s = lax.dot_general(q_ref[rs, sl], kh, (((1,), (1,)), ((), ())),
                                preferred_element_type=F32) * scale
            e = jnp.exp(s - jnp.max(s, axis=-1, keepdims=True))
            p = e * (1.0 / jnp.sum(e, axis=-1, keepdims=True))
            o_ref[rs, sl] = jnp.dot(p.astype(BF16), vh, preferred_element_type=F32).astype(o_ref.dtype)


def _attn(proj, mem_k, mem_v, kv_off, *, n_streams, seq, width, q_blk, heads):
    mlen = mem_k.shape[1]
    rows = _tile(seq, ATTN_ROWS)
    nt = seq // rows
    per_step = 1
    if nt == 1:
        per_step = _tile(n_streams, max(1, ATTN_KV_BYTES // (mlen * width * 4)), 1)
    assert kv_off % per_step == 0
    blk = per_step * rows
    kv_shape = (per_step,) + mem_k.shape[1:]
    kv_spec = pl.BlockSpec(kv_shape, lambda s, t: (kv_off // per_step + s,) + (0,) * (len(kv_shape) - 1))
    return pl.pallas_call(
        functools.partial(_attn_kernel, heads=heads, seq_rows=rows),
        grid=(n_streams // per_step, nt),
        in_specs=[pl.BlockSpec((blk, width), lambda s, t: (s * nt + t, q_blk)), kv_spec, kv_spec],
        out_specs=pl.BlockSpec((blk, width), lambda s, t: (s * nt + t, 0)),
        out_shape=jax.ShapeDtypeStruct((n_streams * seq, width), BF16),
        compiler_params=_params(("parallel", "arbitrary")),
        name="attn",
    )(proj, mem_k, mem_v)


def _merge_m_tile(cs, zl_ref, zc_ref, zm_ref, g0_ref, g1_ref, g2_ref, wl_ref, wc_ref, wm_ref):
    m = g0_ref[:, cs].astype(F32) * jnp.dot(zl_ref[...], wl_ref[:, cs], preferred_element_type=F32)
    m = m + g1_ref[:, cs].astype(F32) * jnp.dot(zc_ref[...], wc_ref[:, cs], preferred_element_type=F32)
    m = m + g2_ref[:, cs].astype(F32) * jnp.dot(zm_ref[...], wm_ref[:, cs], preferred_element_type=F32)
    return m.astype(BF16)


def _merge_kernel(zl_ref, zc_ref, zm_ref, g0_ref, g1_ref, g2_ref, x_ref,
                  wl_ref, wc_ref, wm_ref, wo_ref, gp_ref, o_ref):
    m = _merge_m_tile(slice(None), zl_ref, zc_ref, zm_ref, g0_ref, g1_ref, g2_ref, wl_ref, wc_ref, wm_ref)
    y = jnp.dot(m, wo_ref[...], preferred_element_type=F32)
    o_ref[...] = x_ref[...] + _rms(y, gp_ref[...])


MERGE_WEIGHTS = ("w_branch_lru", "w_branch_conv", "w_branch_mem", "w_out")


def _merge_weight_specs(wts, l, d):
    once = pl.Buffered(1)
    specs = [_layer_spec(wts[k][1], wts[k][0].shape[1:], once) for k in MERGE_WEIGHTS]
    return specs + [_layer_spec(l, (1, d), once)]


def _merge_weight_args(wts):
    return tuple(wts[k][0] for k in MERGE_WEIGHTS) + (wts["g_mix_post"],)


def _merge(zl, zc, zm, proj, x, wts, l, *, width):
    rows, d = x.shape
    tm = _tile(rows, MERGE_TM)
    gate_blk0 = (proj.shape[1] - N_BRANCH * d) // d

    def act(w):
        return pl.BlockSpec((tm, w), lambda i: (i, 0))

    def gate(b):
        return pl.BlockSpec((tm, d), lambda i: (i, gate_blk0 + b))

    return pl.pallas_call(
        _merge_kernel,
        grid=(rows // tm,),
        in_specs=[act(width), act(width), act(width), gate(0), gate(1), gate(2), act(d)]
        + _merge_weight_specs(wts, l, d),
        out_specs=act(d),
        out_shape=jax.ShapeDtypeStruct((rows, d), F32),
        compiler_params=_params(("parallel",)),
        name="merge",
    )(zl, zc, zm, proj, proj, proj, x, *_merge_weight_args(wts))


def _mixmerge_kernel(lx_ref, gg_ref, sb_ref, sc_ref, sh_ref, h0_ref, lb0_ref, sb0_ref,
                     cw_ref, cb_ref, wa_ref, ba_ref, wi_ref, bi_ref, lam_ref, scw_ref,
                     zm_ref, g0_ref, g1_ref, g2_ref, x_ref, wl_ref, wc_ref, wm_ref, wo_ref, gp_ref,
                     o_ref, hn_ref, lbn_ref, sbn_ref,
                     xpad, cpad, acum, bcum, hcar, uscr, ya_s, yi_s, zl_s, zc_s, m_f, m_s, y_s,
                     *, blocks_per_stream, n_blocks):
    g = pl.program_id(0)
    kl = cw_ref.shape[0]

    @pl.when(g == 0)
    def _():
        for ref in (cpad, hcar, uscr, ya_s, yi_s):
            ref[...] = jnp.zeros(ref.shape, ref.dtype)

    @pl.when(jnp.logical_and(g % blocks_per_stream == 0, g < n_blocks))
    def _():
        xpad[0:PAD_ROWS, :] = jnp.zeros((PAD_ROWS, xpad.shape[1]), F32)
        xpad[PAD_ROWS - (kl - 1):PAD_ROWS, :] = lb0_ref[0]

    scan_valid = g >= 1

    @pl.when(jnp.logical_and((g - 1) % blocks_per_stream == 0, scan_valid))
    def _():
        cpad[0:PAD_ROWS, :] = jnp.zeros((PAD_ROWS, cpad.shape[1]), F32)
        cpad[PAD_ROWS - sb0_ref.shape[1]:PAD_ROWS, :] = sb0_ref[0]
        hcar[...] = h0_ref[0]

    rows, d = x_ref.shape
    n_chunks = wa_ref.shape[0]
    col_tiles = [slice(j, j + MERGE_TN) for j in range(0, d, MERGE_TN)]

    def gated(gate_ref, z_ref, w_ref, cs):
        return gate_ref[:, cs].astype(F32) * jnp.dot(z_ref[...], w_ref[:, cs], preferred_element_type=F32)

    tiles = rows // PACKED_ROWS
    vector_pieces = [functools.partial(_mix_scan_chunk, c, n_chunks, lam_ref, uscr, ya_s, yi_s, acum, bcum)
                     for c in range(n_chunks)]
    vector_pieces += [functools.partial(_mix_carry, k * tiles // 2, (k + 1) * tiles // 2,
                                        gg_ref, zl_s, acum, bcum, hcar, unroll=True) for k in range(2)]
    _mix_sconv(sb_ref, sc_ref, sh_ref, scw_ref, zc_s, cpad)
    for cs in col_tiles:
        m_f[:, cs] = gated(g2_ref, zm_ref, wm_ref, cs)
        if vector_pieces:
            vector_pieces.pop(0)()
    for cs in col_tiles:
        m_f[:, cs] += gated(g1_ref, zc_s, wc_ref, cs)
        if vector_pieces:
            vector_pieces.pop(0)()
    while vector_pieces:
        vector_pieces.pop(0)()
    for cs in col_tiles:
        m_s[:, cs] = (m_f[:, cs] + gated(g0_ref, zl_s, wl_ref, cs)).astype(BF16)
    for k, cs in enumerate(col_tiles):
        y_s[:, cs] = jnp.dot(m_s[...], wo_ref[:, cs], preferred_element_type=F32)
        if k == 0:
            _mix_lru_conv(lx_ref, cw_ref, cb_ref, xpad, uscr)
    _mix_gate_dots(uscr, wa_ref, ba_ref, wi_ref, bi_ref, ya_s, yi_s)
    o_ref[...] = x_ref[...] + _rms(y_s[...], gp_ref[...])

    @pl.when(g < n_blocks)
    def _():
        lbn_ref[0] = xpad[PAD_ROWS - (kl - 1):PAD_ROWS, :]

    @pl.when(scan_valid)
    def _():
        hn_ref[0] = hcar[...]
        sbn_ref[0] = cpad[PAD_ROWS - sbn_ref.shape[1]:PAD_ROWS, :]


def _mixmerge(proj, zm, x, h0, lb0, sb0, state_off, wts, l, *, n_streams, seq, width):
    rows, d = x.shape
    tm = _tile(seq, MERGE_TM, PACKED_ROWS)
    nt = seq // tm
    n = n_streams * nt
    kl, ks = wts["lru_conv_w"].shape[1], wts["sconv_w"].shape[1]
    gate_blk0 = (proj.shape[1] - N_BRANCH * d) // d

    def blk(lag):
        return lambda g: jnp.clip(g - lag, 0, n - 1)

    conv_blk, scan_blk, merge_blk = blk(0), blk(1), blk(1)

    def col(c, which):
        return pl.BlockSpec((tm, width), lambda g: (which(g), c))

    def state_in(k, which):
        return pl.BlockSpec((1, k, width), lambda g: (state_off + which(g) // nt, 0, 0))

    def state_out(k, which):
        return pl.BlockSpec((1, k, width), lambda g: (which(g) // nt, 0, 0))

    def gate(b):
        return pl.BlockSpec((tm, d), lambda g: (merge_blk(g), gate_blk0 + b))

    def act(w):
        return pl.BlockSpec((tm, w), lambda g: (merge_blk(g), 0))

    kern = functools.partial(_mixmerge_kernel, blocks_per_stream=nt, n_blocks=n)
    act_bf16 = pltpu.VMEM((tm, width), BF16)
    act_f32 = pltpu.VMEM((tm, width), F32)
    return pl.pallas_call(
        kern,
        grid=(n + 1,),
        in_specs=[col(0, conv_blk), col(1, scan_blk), col(2, scan_blk), col(3, scan_blk), col(4, scan_blk),
                  state_in(1, scan_blk), state_in(kl - 1, conv_blk), state_in(ks - 1, scan_blk)]
        + _mix_param_specs(wts, l, width)
        + [act(width), gate(0), gate(1), gate(2), act(d)] + _merge_weight_specs(wts, l, d),
        out_specs=[act(d), state_out(1, scan_blk), state_out(kl - 1, conv_blk), state_out(ks - 1, scan_blk)],
        out_shape=[jax.ShapeDtypeStruct((rows, d), F32),
                   jax.ShapeDtypeStruct((n_streams, 1, width), F32),
                   jax.ShapeDtypeStruct((n_streams, kl - 1, width), F32),
                   jax.ShapeDtypeStruct((n_streams, ks - 1, width), F32)],
        scratch_shapes=_mix_scratch(tm, width) + [act_f32, act_f32, act_bf16, act_bf16, pltpu.VMEM((tm, d), F32),
                                                  pltpu.VMEM((tm, d), BF16), pltpu.VMEM((tm, d), F32)],
        compiler_params=_params(("arbitrary",)),
        name="mixmerge",
    )(proj, proj, proj, proj, proj, h0, lb0, sb0, *_mix_param_args(wts),
      zm, proj, proj, proj, x, *_merge_weight_args(wts))


def _mlp_kernel(x_ref, gpre_ref, wu_ref, wd_ref, gpost_ref, *rest, cast_next):
    if cast_next:
        wnext_ref, o_ref, wnext_out_ref, xn_ref, acc_ref = rest
    else:
        o_ref, xn_ref, acc_ref = rest
    f = pl.program_id(1)
    last = pl.num_programs(1) - 1
    tm = x_ref.shape[0]

    def partial_down(rows):
        hid = jnp.dot(xn_ref[rows, :], wu_ref[...], preferred_element_type=F32)
        hid = jnp.square(jnp.maximum(hid, 0.0)).astype(BF16)
        return jnp.dot(hid, wd_ref[...], preferred_element_type=F32)

    def side_cast():
        if cast_next:
            wnext_out_ref[...] = wnext_ref[...].astype(wnext_out_ref.dtype)

    @pl.when(f == 0)
    def _():
        side_cast()
        xn_ref[...] = _rms(x_ref[...], gpre_ref[...]).astype(BF16)
        acc_ref[...] = partial_down(slice(None))

    @pl.when(jnp.logical_and(f > 0, f < last))
    def _():
        side_cast()
        acc_ref[...] += partial_down(slice(None))

    @pl.when(f == last)
    def _():
        side_cast()
        for half in (slice(0, tm // 2), slice(tm // 2, tm)):
            y = acc_ref[half, :] + partial_down(half)
            o_ref[half, :] = x_ref[half, :] + _rms(y, gpost_ref[...])


def _mlp(x, wts, l, *, w_next=None):
    rows, d = x.shape
    (w_up, up_l), (w_down, down_l) = wts["w_up"], wts["w_down"]
    dff = w_up.shape[2]
    tm, tf = _tile(rows, MLP_TM, 2 * SUBLANES), _tile(dff, MLP_TF, 128)
    assert dff // tf >= 2
    grid = (rows // tm, dff // tf)
    row_blk = pl.BlockSpec((tm, d), lambda i, f: (i, 0))
    in_specs = [row_blk,
                _layer_spec(l, (1, d)),
                pl.BlockSpec((None, d, tf), lambda i, f: (up_l, 0, f)),
                pl.BlockSpec((None, tf, d), lambda i, f: (down_l, f, 0)),
                _layer_spec(l, (1, d))]
    args = [x, wts["g_mlp_pre"], w_up, w_down, wts["g_mlp_post"]]
    out_specs, out_shape = [row_blk], [jax.ShapeDtypeStruct((rows, d), F32)]
    cast_next = w_next is not None
    if cast_next:
        cast_in, cast_out = _cast_specs(l + 1, w_next.shape[1:], grid, 1)
        in_specs.append(cast_in)
        args.append(w_next)
        out_specs.append(cast_out)
        out_shape.append(jax.ShapeDtypeStruct((1,) + w_next.shape[1:], BF16))
    res = pl.pallas_call(
        functools.partial(_mlp_kernel, cast_next=cast_next),
        grid=grid,
        in_specs=in_specs,
        out_specs=out_specs,
        out_shape=out_shape,
        scratch_shapes=[pltpu.VMEM((tm, d), BF16), pltpu.VMEM((tm, d), F32)],
        compiler_params=_params(("arbitrary", "arbitrary")),
        name="mlp",
    )(*args)
    return (res[0], res[1]) if cast_next else (res[0], None)


def _block_diag_pack(w):
    depth, heads, dh, _ = w.shape
    per = max(1, min(heads, MXU_DIM // dh))
    n = heads // per
    wg = w.reshape(depth, n, per, dh, dh)
    eye = jnp.eye(per, dtype=w.dtype)
    out = jnp.einsum("lnhij,hg->lnhigj", wg, eye)
    return out.reshape(depth, n, per * dh, per * dh).astype(BF16)


def _run_layer(x, mem_k, mem_v, kv_off, h0, lb0, sb0, state_off, wts, l,
               *, n_streams, seq, width, heads, last, cast=None):
    if cast is not None:
        names = ("w_up", "w_down") + MERGE_WEIGHTS
        proj, *casts = _proj(x, wts, l, width=width, cast=tuple(cast[k] for k in names))
        wts = dict(wts, **{k: (w, 0) for k, w in zip(names, casts)})
    else:
        proj, = _proj(x, wts, l, width=width)
    zm = _attn(proj, mem_k, mem_v, kv_off, n_streams=n_streams, seq=seq, width=width, q_blk=5, heads=heads)
    if seq >= MERGE_TM and seq % MERGE_TM == 0:
        x, hn, lbn, sbn = _mixmerge(proj, zm, x, h0, lb0, sb0, state_off, wts, l,
                                    n_streams=n_streams, seq=seq, width=width)
    else:
        zl, zc, hn, lbn, sbn = _mix(proj, h0, lb0, sb0, state_off, wts, l,
                                    n_streams=n_streams, seq=seq, width=width)
        x = _merge(zl, zc, zm, proj, x, wts, l, width=width)
    w_next = cast["w_in"] if cast is not None and not last else None
    x, w_in_next = _mlp(x, wts, l, w_next=w_next)
    return x, hn, lbn, sbn, wts, w_in_next


def kernel(x_prompt, x_sample, mem_prompt, state_lru_h, state_lru_conv, state_sconv, cache_mem_k, cache_mem_v, g_mix_pre, w_in, b_gate, lru_conv_w, lru_conv_b, lru_w_a, lru_b_a, lru_w_i, lru_b_i, lru_lambda, w_branch_lru, sconv_w, w_branch_conv, g_mem, w_mem_kv, w_branch_mem, w_out, g_mix_post, g_mlp_pre, w_up, w_down, g_mlp_post):
    depth = w_in.shape[0]
    bp, sp, d = x_prompt.shape
    bs, ss, _ = x_sample.shape
    width = lru_lambda.shape[1]
    mlen, heads, dh = cache_mem_k.shape[2:]
    kl, ks = lru_conv_w.shape[1], sconv_w.shape[1]

    def rowvec(p):
        return p.reshape(depth, 1, -1)

    wts = dict(
        g_mix_pre=rowvec(g_mix_pre), w_in=(w_in[0:1].astype(BF16), 0), b_gate=rowvec(b_gate),
        lru_conv_w=lru_conv_w, lru_conv_b=rowvec(lru_conv_b),
        wa_bd=_block_diag_pack(lru_w_a), lru_b_a=rowvec(lru_b_a),
        wi_bd=_block_diag_pack(lru_w_i), lru_b_i=rowvec(lru_b_i),
        lru_lambda=rowvec(lru_lambda), sconv_w=sconv_w,
        g_mix_post=rowvec(g_mix_post), g_mlp_pre=rowvec(g_mlp_pre),
        g_mlp_post=rowvec(g_mlp_post),
        g_mem=rowvec(g_mem), w_mem_kv=w_mem_kv.astype(BF16))

    hp = x_prompt.reshape(bp * sp, d)
    hs = x_sample.reshape(bs * ss, d)
    mem2d = mem_prompt.reshape(bp * mlen, d)
    p_h0 = jnp.zeros((bp, 1, width), F32)
    p_lb0 = jnp.zeros((bp, kl - 1, width), F32)
    p_sb0 = jnp.zeros((bp, ks - 1, width), F32)
    s_h0 = state_lru_h.reshape(depth * bs, 1, width)
    s_lb0 = state_lru_conv.reshape(depth * bs, kl - 1, width)
    s_sb0 = state_sconv.reshape(depth * bs, ks - 1, width)
    s_mk = cache_mem_k.reshape(depth * bs, mlen, heads, dh)
    s_mv = cache_mem_v.reshape(depth * bs, mlen, heads, dh)

    outs = {k: [] for k in ("p_h", "p_lb", "p_sb", "p_mk", "p_mv", "s_h", "s_lb", "s_sb")}
    f32_weights = dict(w_up=w_up, w_down=w_down, w_branch_lru=w_branch_lru, w_branch_conv=w_branch_conv,
                       w_branch_mem=w_branch_mem, w_out=w_out, w_in=w_in)
    for l in range(depth):
        last = l == depth - 1
        kv = _memkv(mem2d, wts, l, width=width)
        mk = kv[0].reshape(bp, mlen, width)
        mv = kv[1].reshape(bp, mlen, width)
        hp, hn, lbn, sbn, wts, w_in_next = _run_layer(
            hp, mk, mv, 0, p_h0, p_lb0, p_sb0, 0, wts, l,
            n_streams=bp, seq=sp, width=width, heads=heads, last=last, cast=f32_weights)
        outs["p_h"].append(hn.reshape(bp, width))
        outs["p_lb"].append(lbn)
        outs["p_sb"].append(sbn)
        outs["p_mk"].append(mk.reshape(bp, mlen, heads, dh))
        outs["p_mv"].append(mv.reshape(bp, mlen, heads, dh))
        hs, hn, lbn, sbn, _, _ = _run_layer(
            hs, s_mk, s_mv, l * bs, s_h0, s_lb0, s_sb0, l * bs, wts, l,
            n_streams=bs, seq=ss, width=width, heads=heads, last=last)
        outs["s_h"].append(hn.reshape(bs, width))
        outs["s_lb"].append(lbn)
        outs["s_sb"].append(sbn)
        if not last:
            wts = dict(wts, w_in=(w_in_next, 0))

    return (hp.reshape(bp, sp, d), hs.reshape(bs, ss, d),
            jnp.stack(outs["p_h"]), jnp.stack(outs["p_lb"]), jnp.stack(outs["p_sb"]),
            jnp.stack(outs["p_mk"]), jnp.stack(outs["p_mv"]),
            jnp.stack(outs["s_h"]), jnp.stack(outs["s_lb"]), jnp.stack(outs["s_sb"]))
```

```python
import functools

import jax
import jax.numpy as jnp
from jax import lax
from jax.experimental import pallas as pl
from jax.experimental.pallas import tpu as pltpu

F32 = jnp.float32
BF16 = jnp.bfloat16

EPS = 1e-6
LRU_C = 8.0
N_BRANCH = 3

SUBLANES = 8
LANES = 128
PACKED_ROWS = 16
MXU_DIM = 256
VMEM_LIMIT_BYTES = 56 * 1024 * 1024

PROJ_TM = 1024
MIX_ROWS = 512
ATTN_ROWS = 1024
ATTN_KV_BYTES = 4 * 1024 * 1024
MERGE_TM = 256
MERGE_TN = 512
MLP_TM, MLP_TF = 512, 1024
PAD_ROWS = SUBLANES
SCAN_ROWS = 32


def _tile(n, pref, mult=SUBLANES):
    if n <= pref:
        return n
    for t in range(pref, 0, -1):
        if n % t == 0 and t % mult == 0:
            return t
    raise ValueError(f"no tile for {n} <= {pref}")


def _params(sem):
    return pltpu.CompilerParams(dimension_semantics=sem, vmem_limit_bytes=VMEM_LIMIT_BYTES)


def _rms(x, g):
    var = jnp.mean(x * x, axis=-1, keepdims=True)
    return (x * lax.rsqrt(var + EPS)) * g


def _sigmoid(x):
    return 0.5 * jnp.tanh(0.5 * x) + 0.5


def _layer_spec(l, shape, pipeline_mode=None):
    nd = len(shape)
    kw = {} if pipeline_mode is None else {"pipeline_mode": pipeline_mode}
    return pl.BlockSpec((None,) + tuple(shape), lambda *_: (l,) + (0,) * nd, **kw)


def _split(n, parts_max, mult):
    for p in range(parts_max, 0, -1):
        if n % p == 0 and (n // p) % mult == 0:
            return p
    raise ValueError(f"cannot split {n} into <= {parts_max} parts of multiples of {mult}")


def _cast_specs(l, shape, grid, row_axis):
    r, c = shape
    col_axis = 1 - row_axis
    pr = _split(r, grid[row_axis], PACKED_ROWS)
    pc = _split(c, grid[col_axis], LANES)
    blk = (r // pr, c // pc)

    def idx(*g):
        return jnp.minimum(g[row_axis], pr - 1), jnp.minimum(g[col_axis], pc - 1)

    return (pl.BlockSpec((None,) + blk, lambda *g: (l,) + idx(*g)),
            pl.BlockSpec((None,) + blk, lambda *g: (0,) + idx(*g)))


def _proj_kernel(x_ref, g_ref, w_ref, bg_ref, *rest, gelu_blk, gate_blk0):
    n_cast = (len(rest) - 2) // 2
    cast_in, o_ref, cast_out, xn_ref = rest[:n_cast], rest[n_cast], rest[n_cast + 1:-1], rest[-1]
    j = pl.program_id(1)

    @pl.when(j == 0)
    def _():
        xn_ref[...] = _rms(x_ref[...], g_ref[...]).astype(BF16)

    def dot():
        for src, dst in zip(cast_in, cast_out):
            dst[...] = src[...].astype(dst.dtype)
        return jnp.dot(xn_ref[...], w_ref[...], preferred_element_type=F32)

    @pl.when(j == gelu_blk)
    def _():
        o_ref[...] = jax.nn.gelu(dot()).astype(o_ref.dtype)

    @pl.when(j >= gate_blk0)
    def _():
        o_ref[...] = _sigmoid(dot() + bg_ref[...]).astype(o_ref.dtype)

    @pl.when(jnp.logical_and(j != gelu_blk, j < gate_blk0))
    def _():
        o_ref[...] = dot().astype(o_ref.dtype)


def _proj(x, wts, l, *, width, cast=()):
    rows, d = x.shape
    w_in, w_l = wts["w_in"]
    cols = w_in.shape[2]
    tm, tn = _tile(rows, PROJ_TM), width
    gate_blk0 = (cols - N_BRANCH * d) // tn
    grid = (rows // tm, cols // tn)
    kern = functools.partial(_proj_kernel, gelu_blk=1, gate_blk0=gate_blk0)
    cast_specs = [_cast_specs(l, w.shape[1:], grid, 0) for w in cast]
    return pl.pallas_call(
        kern,
        grid=grid,
        in_specs=[
            pl.BlockSpec((tm, d), lambda i, j: (i, 0)),
            _layer_spec(l, (1, d)),
            pl.BlockSpec((None, d, tn), lambda i, j: (w_l, 0, j)),
            pl.BlockSpec((None, 1, tn), lambda i, j: (l, 0, jnp.maximum(j - gate_blk0, 0))),
        ] + [cs[0] for cs in cast_specs],
        out_specs=[pl.BlockSpec((tm, tn), lambda i, j: (i, j))] + [cs[1] for cs in cast_specs],
        out_shape=[jax.ShapeDtypeStruct((rows, cols), BF16)]
        + [jax.ShapeDtypeStruct((1,) + w.shape[1:], BF16) for w in cast],
        scratch_shapes=[pltpu.VMEM((tm, d), BF16)],
        compiler_params=_params(("arbitrary", "arbitrary")),
        name="proj",
    )(x, wts["g_mix_pre"], w_in, wts["b_gate"], *cast)


def _memkv_kernel(x_ref, g_ref, w_ref, o_ref, xn_ref):
    @pl.when(pl.program_id(1) == 0)
    def _():
        xn_ref[...] = _rms(x_ref[...], g_ref[...]).astype(BF16)

    o_ref[...] = jnp.dot(xn_ref[...], w_ref[...].astype(BF16), preferred_element_type=F32)


def _memkv(mem, wts, l, *, width):
    rows, d = mem.shape
    cols = wts["w_mem_kv"].shape[2]
    tm, tn = _tile(rows, PROJ_TM), width
    return pl.pallas_call(
        _memkv_kernel,
        grid=(rows // tm, cols // tn),
        in_specs=[
            pl.BlockSpec((tm, d), lambda i, j: (i, 0)),
            _layer_spec(l, (1, d)),
            pl.BlockSpec((None, d, tn), lambda i, j: (l, 0, j)),
        ],
        out_specs=pl.BlockSpec((None, tm, tn), lambda i, j: (j, i, 0)),
        out_shape=jax.ShapeDtypeStruct((cols // tn, rows, tn), F32),
        scratch_shapes=[pltpu.VMEM((tm, d), BF16)],
        compiler_params=_params(("parallel", "arbitrary")),
        name="memkv",
    )(mem, wts["g_mem"], wts["w_mem_kv"])


def _softplus(x):
    return jnp.maximum(x, 0.0) + jnp.log1p(jnp.exp(-jnp.abs(x)))


def _causal_conv(pad_ref, w_ref, rows):
    taps = w_ref.shape[0]
    xe = pad_ref[...]
    y = None
    for k in range(taps):
        back = taps - 1 - k
        xk = xe if back == 0 else pltpu.roll(xe, back, axis=0)
        term = xk[PAD_ROWS:PAD_ROWS + rows, :] * w_ref[k:k + 1, :]
        y = term if y is None else y + term
    return y


def _mix_init(h0_ref, lb0_ref, sb0_ref, xpad, cpad, hcar):
    width = xpad.shape[1]
    xpad[0:PAD_ROWS, :] = jnp.zeros((PAD_ROWS, width), F32)
    cpad[0:PAD_ROWS, :] = jnp.zeros((PAD_ROWS, width), F32)
    xpad[PAD_ROWS - lb0_ref.shape[1]:PAD_ROWS, :] = lb0_ref[0]
    cpad[PAD_ROWS - sb0_ref.shape[1]:PAD_ROWS, :] = sb0_ref[0]
    hcar[...] = h0_ref[0]


def _mix_state_out(hn_ref, lbn_ref, sbn_ref, xpad, cpad, hcar):
    hn_ref[0] = hcar[...]
    lbn_ref[0] = xpad[PAD_ROWS - lbn_ref.shape[1]:PAD_ROWS, :]
    sbn_ref[0] = cpad[PAD_ROWS - sbn_ref.shape[1]:PAD_ROWS, :]


def _mix_lru_conv(lx_ref, cw_ref, cb_ref, xpad, uscr):
    rows = lx_ref.shape[0]
    kl = cw_ref.shape[0]
    xpad[PAD_ROWS:PAD_ROWS + rows, :] = lx_ref[...].astype(F32)
    uscr[...] = _causal_conv(xpad, cw_ref, rows) + cb_ref[...]
    xpad[PAD_ROWS - (kl - 1):PAD_ROWS, :] = xpad[PAD_ROWS + rows - (kl - 1):PAD_ROWS + rows, :]


def _mix_gate_dots(uscr, wa_ref, ba_ref, wi_ref, bi_ref, acum, bcum):
    n_chunks = wa_ref.shape[0]
    bd = uscr.shape[1] // n_chunks
    for c in range(n_chunks):
        sl = slice(c * bd, (c + 1) * bd)
        ub = uscr[:, sl].astype(BF16)
        acum[:, sl] = jnp.dot(ub, wa_ref[c], preferred_element_type=F32) + ba_ref[:, sl]
        bcum[:, sl] = jnp.dot(ub, wi_ref[c], preferred_element_type=F32) + bi_ref[:, sl]


def _mix_sconv(sb_ref, sc_ref, sh_ref, scw_ref, zc_ref, cpad):
    rows = sc_ref.shape[0]
    ks = scw_ref.shape[0]
    cpad[PAD_ROWS:PAD_ROWS + rows, :] = sc_ref[...].astype(F32) * sh_ref[...].astype(F32)
    cv = _causal_conv(cpad, scw_ref, rows)
    zc_ref[...] = (sb_ref[...].astype(F32) * cv).astype(zc_ref.dtype)
    cpad[PAD_ROWS - (ks - 1):PAD_ROWS, :] = cpad[PAD_ROWS + rows - (ks - 1):PAD_ROWS + rows, :]


def _mix_scan_chunk(c, n_chunks, lam_ref, uscr, ya_s, yi_s, acum, bcum):
    rows, width = uscr.shape
    bd = width // n_chunks
    sl = slice(c * bd, (c + 1) * bd)
    neg_c_sp = -LRU_C * _softplus(-lam_ref[:, sl])
    tr = _tile(rows, SCAN_ROWS)
    groups = tr // SUBLANES
    row = lax.broadcasted_iota(jnp.int32, (groups, SUBLANES, bd), 1)
    for r0 in range(0, rows, tr):
        rs = slice(r0, r0 + tr)
        r = _sigmoid(ya_s[rs, sl])
        ig = _sigmoid(yi_s[rs, sl])
        log_a = neg_c_sp * r
        a = jnp.exp(log_a)
        th = jnp.tanh(log_a)
        mult = jnp.sqrt(-2.0 * th) * lax.rsqrt(1.0 - th)
        b = mult * (ig * uscr[rs, sl])
        a3 = a.reshape(groups, SUBLANES, bd)
        b3 = b.reshape(groups, SUBLANES, bd)
        d = 1
        while d < SUBLANES:
            a_prev = pltpu.roll(a3, d, axis=1)
            b_prev = pltpu.roll(b3, d, axis=1)
            take = row >= d
            b3 = jnp.where(take, a3 * b_prev + b3, b3)
            a3 = jnp.where(take, a3 * a_prev, a3)
            d *= 2
        acum[rs, sl] = a3.reshape(tr, bd)
        bcum[rs, sl] = b3.reshape(tr, bd)


def _mix_carry(lo, hi, gg_ref, zl_ref, acum, bcum, hcar, *, unroll):
    def body(g, h):
        r0 = pl.multiple_of(g * PACKED_ROWS, PACKED_ROWS)
        r1 = pl.multiple_of(r0 + SUBLANES, SUBLANES)
        h1 = acum[pl.ds(r0, SUBLANES), :] * h + bcum[pl.ds(r0, SUBLANES), :]
        h2 = acum[pl.ds(r1, SUBLANES), :] * h1[SUBLANES - 1:SUBLANES, :] + bcum[pl.ds(r1, SUBLANES), :]
        hh = jnp.concatenate([h1, h2], axis=0)
        gate = gg_ref[pl.ds(r0, PACKED_ROWS), :].astype(F32)
        zl_ref[pl.ds(r0, PACKED_ROWS), :] = (hh * gate).astype(zl_ref.dtype)
        return h2[SUBLANES - 1:SUBLANES, :]

    hcar[...] = lax.fori_loop(lo, hi, body, hcar[...], unroll=unroll)


def _mix_kernel(lx_ref, gg_ref, sb_ref, sc_ref, sh_ref, h0_ref, lb0_ref, sb0_ref,
                cw_ref, cb_ref, wa_ref, ba_ref, wi_ref, bi_ref, lam_ref, scw_ref,
                zl_ref, zc_ref, hn_ref, lbn_ref, sbn_ref,
                xpad, cpad, acum, bcum, hcar, uscr):
    @pl.when(pl.program_id(1) == 0)
    def _():
        _mix_init(h0_ref, lb0_ref, sb0_ref, xpad, cpad, hcar)

    _mix_lru_conv(lx_ref, cw_ref, cb_ref, xpad, uscr)
    _mix_gate_dots(uscr, wa_ref, ba_ref, wi_ref, bi_ref, acum, bcum)
    _mix_sconv(sb_ref, sc_ref, sh_ref, scw_ref, zc_ref, cpad)
    n_chunks = wa_ref.shape[0]
    for c in range(n_chunks):
        _mix_scan_chunk(c, n_chunks, lam_ref, uscr, acum, bcum, acum, bcum)
    _mix_carry(0, lx_ref.shape[0] // PACKED_ROWS, gg_ref, zl_ref, acum, bcum, hcar, unroll=False)
    _mix_state_out(hn_ref, lbn_ref, sbn_ref, xpad, cpad, hcar)


def _mix_scratch(rows, width):
    return [pltpu.VMEM((PAD_ROWS + rows, width), F32),
            pltpu.VMEM((PAD_ROWS + rows, width), F32),
            pltpu.VMEM((rows, width), F32),
            pltpu.VMEM((rows, width), F32),
            pltpu.VMEM((1, width), F32),
            pltpu.VMEM((rows, width), F32)]


def _mix_param_specs(wts, l, width):
    kl, ks = wts["lru_conv_w"].shape[1], wts["sconv_w"].shape[1]
    nck = wts["wa_bd"].shape[1]
    bd_spec = _layer_spec(l, (nck, width // nck, width // nck))
    return [_layer_spec(l, (kl, width)), _layer_spec(l, (1, width)),
            bd_spec, _layer_spec(l, (1, width)), bd_spec, _layer_spec(l, (1, width)),
            _layer_spec(l, (1, width)), _layer_spec(l, (ks, width))]


def _mix_param_args(wts):
    return (wts["lru_conv_w"], wts["lru_conv_b"], wts["wa_bd"], wts["lru_b_a"], wts["wi_bd"], wts["lru_b_i"],
            wts["lru_lambda"], wts["sconv_w"])


def _mix(proj, h0, lb0, sb0, state_off, wts, l, *, n_streams, seq, width):
    rows = _tile(seq, MIX_ROWS, PACKED_ROWS)
    nt = seq // rows
    kl, ks = wts["lru_conv_w"].shape[1], wts["sconv_w"].shape[1]

    def col(c):
        return pl.BlockSpec((rows, width), lambda s, t: (s * nt + t, c))

    def state_in(k):
        return pl.BlockSpec((1, k, width), lambda s, t: (state_off + s, 0, 0))

    def state_out(k):
        return pl.BlockSpec((1, k, width), lambda s, t: (s, 0, 0))

    total = n_streams * seq
    return pl.pallas_call(
        _mix_kernel,
        grid=(n_streams, nt),
        in_specs=[col(0), col(1), col(2), col(3), col(4),
                  state_in(1), state_in(kl - 1), state_in(ks - 1)] + _mix_param_specs(wts, l, width),
        out_specs=[pl.BlockSpec((rows, width), lambda s, t: (s * nt + t, 0)),
                   pl.BlockSpec((rows, width), lambda s, t: (s * nt + t, 0)),
                   state_out(1), state_out(kl - 1), state_out(ks - 1)],
        out_shape=[jax.ShapeDtypeStruct((total, width), BF16),
                   jax.ShapeDtypeStruct((total, width), BF16),
                   jax.ShapeDtypeStruct((n_streams, 1, width), F32),
                   jax.ShapeDtypeStruct((n_streams, kl - 1, width), F32),
                   jax.ShapeDtypeStruct((n_streams, ks - 1, width), F32)],
        scratch_shapes=_mix_scratch(rows, width),
        compiler_params=_params(("parallel", "arbitrary")),
        name="mix",
    )(proj, proj, proj, proj, proj, h0, lb0, sb0, *_mix_param_args(wts))


def _attn_kernel(q_ref, k_ref, v_ref, o_ref, *, heads, seq_rows):
    dh = q_ref.shape[1] // heads
    scale = dh ** -0.5
    if len(k_ref.shape) == 4:
        mh = k_ref.shape[1] * heads
        shape = (heads * seq_rows, mh)
        row = lax.broadcasted_iota(jnp.int32, shape, 0)
        row_head = sum((row >= h * seq_rows).astype(jnp.int32) for h in range(1, heads))
        own = row_head == lax.broadcasted_iota(jnp.int32, shape, 1) % heads
        for g in range(k_ref.shape[0]):
            rs = slice(g * seq_rows, (g + 1) * seq_rows)
            kf = k_ref[g].reshape(mh, dh).astype(BF16)
            vf = v_ref[g].reshape(mh, dh).astype(BF16)
            qs = jnp.concatenate([q_ref[rs, h * dh:(h + 1) * dh] for h in range(heads)], axis=0)
            s = lax.dot_general(qs, kf, (((1,), (1,)), ((), ())), preferred_element_type=F32) * scale
            s = jnp.where(own, s, -jnp.inf)
            e = jnp.where(own, jnp.exp(s - jnp.max(s, axis=-1, keepdims=True)), 0.0)
            p = e * (1.0 / jnp.sum(e, axis=-1, keepdims=True))
            o = jnp.dot(p.astype(BF16), vf, preferred_element_type=F32).astype(o_ref.dtype)
            for h in range(heads):
                o_ref[rs, h * dh:(h + 1) * dh] = o[h * seq_rows:(h + 1) * seq_rows, :]
        return
    for g in range(k_ref.shape[0]):
        rs = slice(g * seq_rows, (g + 1) * seq_rows)
        for h in range(heads):
            sl = slice(h * dh, (h + 1) * dh)
            kh = k_ref[g, :, sl].astype(BF16)
            vh = v_ref[g, :, sl].astype(BF16)
            s = lax.dot_general(q_ref[rs, sl], kh, (((1,), (1,)), ((), ())),
                                preferred_element_type=F32) * scale
            e = jnp.exp(s - jnp.max(s, axis=-1, keepdims=True))
            p = e * (1.0 / jnp.sum(e, axis=-1, keepdims=True))
            o_ref[rs, sl] = jnp.dot(p.astype(BF16), vh, preferred_element_type=F32).astype(o_ref.dtype)


def _attn(proj, mem_k, mem_v, kv_off, *, n_streams, seq, width, q_blk, heads):
    mlen = mem_k.shape[1]
    rows = _tile(seq, ATTN_ROWS)
    nt = seq // rows
    per_step = 1
    if nt == 1:
        per_step = _tile(n_streams, max(1, ATTN_KV_BYTES // (mlen * width * 4)), 1)
    assert kv_off % per_step == 0
    blk = per_step * rows
    kv_shape = (per_step,) + mem_k.shape[1:]
    kv_spec = pl.BlockSpec(kv_shape, lambda s, t: (kv_off // per_step + s,) + (0,) * (len(kv_shape) - 1))
    return pl.pallas_call(
        functools.partial(_attn_kernel, heads=heads, seq_rows=rows),
        grid=(n_streams // per_step, nt),
        in_specs=[pl.BlockSpec((blk, width), lambda s, t: (s * nt + t, q_blk)), kv_spec, kv_spec],
        out_specs=pl.BlockSpec((blk, width), lambda s, t: (s * nt + t, 0)),
        out_shape=jax.ShapeDtypeStruct((n_streams * seq, width), BF16),
        compiler_params=_params(("parallel", "arbitrary")),
        name="attn",
    )(proj, mem_k, mem_v)


def _merge_m_tile(cs, zl_ref, zc_ref, zm_ref, g0_ref, g1_ref, g2_ref, wl_ref, wc_ref, wm_ref):
    m = g0_ref[:, cs].astype(F32) * jnp.dot(zl_ref[...], wl_ref[:, cs], preferred_element_type=F32)
    m = m + g1_ref[:, cs].astype(F32) * jnp.dot(zc_ref[...], wc_ref[:, cs], preferred_element_type=F32)
    m = m + g2_ref[:, cs].astype(F32) * jnp.dot(zm_ref[...], wm_ref[:, cs], preferred_element_type=F32)
    return m.astype(BF16)


def _merge_kernel(zl_ref, zc_ref, zm_ref, g0_ref, g1_ref, g2_ref, x_ref,
                  wl_ref, wc_ref, wm_ref, wo_ref, gp_ref, o_ref):
    m = _merge_m_tile(slice(None), zl_ref, zc_ref, zm_ref, g0_ref, g1_ref, g2_ref, wl_ref, wc_ref, wm_ref)
    y = jnp.dot(m, wo_ref[...], preferred_element_type=F32)
    o_ref[...] = x_ref[...] + _rms(y, gp_ref[...])


MERGE_WEIGHTS = ("w_branch_lru", "w_branch_conv", "w_branch_mem", "w_out")


def _merge_weight_specs(wts, l, d):
    once = pl.Buffered(1)
    specs = [_layer_spec(wts[k][1], wts[k][0].shape[1:], once) for k in MERGE_WEIGHTS]
    return specs + [_layer_spec(l, (1, d), once)]


def _merge_weight_args(wts):
    return tuple(wts[k][0] for k in MERGE_WEIGHTS) + (wts["g_mix_post"],)


def _merge(zl, zc, zm, proj, x, wts, l, *, width):
    rows, d = x.shape
    tm = _tile(rows, MERGE_TM)
    gate_blk0 = (proj.shape[1] - N_BRANCH * d) // d

    def act(w):
        return pl.BlockSpec((tm, w), lambda i: (i, 0))

    def gate(b):
        return pl.BlockSpec((tm, d), lambda i: (i, gate_blk0 + b))

    return pl.pallas_call(
        _merge_kernel,
        grid=(rows // tm,),
        in_specs=[act(width), act(width), act(width), gate(0), gate(1), gate(2), act(d)]
        + _merge_weight_specs(wts, l, d),
        out_specs=act(d),
        out_shape=jax.ShapeDtypeStruct((rows, d), F32),
        compiler_params=_params(("parallel",)),
        name="merge",
    )(zl, zc, zm, proj, proj, proj, x, *_merge_weight_args(wts))


def _mixmerge_kernel(lx_ref, gg_ref, sb_ref, sc_ref, sh_ref, h0_ref, lb0_ref, sb0_ref,
                     cw_ref, cb_ref, wa_ref, ba_ref, wi_ref, bi_ref, lam_ref, scw_ref,
                     zm_ref, g0_ref, g1_ref, g2_ref, x_ref, wl_ref, wc_ref, wm_ref, wo_ref, gp_ref,
                     o_ref, hn_ref, lbn_ref, sbn_ref,
                     xpad, cpad, acum, bcum, hcar, uscr, ya_s, yi_s, zl_s, zc_s, m_f, m_s, y_s,
                     *, blocks_per_stream, n_blocks):
    g = pl.program_id(0)
    kl = cw_ref.shape[0]

    @pl.when(g == 0)
    def _():
        for ref in (cpad, hcar, uscr, ya_s, yi_s):
            ref[...] = jnp.zeros(ref.shape, ref.dtype)

    @pl.when(jnp.logical_and(g % blocks_per_stream == 0, g < n_blocks))
    def _():
        xpad[0:PAD_ROWS, :] = jnp.zeros((PAD_ROWS, xpad.shape[1]), F32)
        xpad[PAD_ROWS - (kl - 1):PAD_ROWS, :] = lb0_ref[0]

    scan_valid = g >= 1

    @pl.when(jnp.logical_and((g - 1) % blocks_per_stream == 0, scan_valid))
    def _():
        cpad[0:PAD_ROWS, :] = jnp.zeros((PAD_ROWS, cpad.shape[1]), F32)
        cpad[PAD_ROWS - sb0_ref.shape[1]:PAD_ROWS, :] = sb0_ref[0]
        hcar[...] = h0_ref[0]

    rows, d = x_ref.shape
    n_chunks = wa_ref.shape[0]
    col_tiles = [slice(j, j + MERGE_TN) for j in range(0, d, MERGE_TN)]

    def gated(gate_ref, z_ref, w_ref, cs):
        return gate_ref[:, cs].astype(F32) * jnp.dot(z_ref[...], w_ref[:, cs], preferred_element_type=F32)

    tiles = rows // PACKED_ROWS
    vector_pieces = [functools.partial(_mix_scan_chunk, c, n_chunks, lam_ref, uscr, ya_s, yi_s, acum, bcum)
                     for c in range(n_chunks)]
    vector_pieces += [functools.partial(_mix_carry, k * tiles // 2, (k + 1) * tiles // 2,
                                        gg_ref, zl_s, acum, bcum, hcar, unroll=True) for k in range(2)]
    _mix_sconv(sb_ref, sc_ref, sh_ref, scw_ref, zc_s, cpad)
    for cs in col_tiles:
        m_f[:, cs] = gated(g2_ref, zm_ref, wm_ref, cs)
        if vector_pieces:
            vector_pieces.pop(0)()
    for cs in col_tiles:
        m_f[:, cs] += gated(g1_ref, zc_s, wc_ref, cs)
        if vector_pieces:
            vector_pieces.pop(0)()
    while vector_pieces:
        vector_pieces.pop(0)()
    for cs in col_tiles:
        m_s[:, cs] = (m_f[:, cs] + gated(g0_ref, zl_s, wl_ref, cs)).astype(BF16)
    for k, cs in enumerate(col_tiles):
        y_s[:, cs] = jnp.dot(m_s[...], wo_ref[:, cs], preferred_element_type=F32)
        if k == 0:
            _mix_lru_conv(lx_ref, cw_ref, cb_ref, xpad, uscr)
    _mix_gate_dots(uscr, wa_ref, ba_ref, wi_ref, bi_ref, ya_s, yi_s)
    o_ref[...] = x_ref[...] + _rms(y_s[...], gp_ref[...])

    @pl.when(g < n_blocks)
    def _():
        lbn_ref[0] = xpad[PAD_ROWS - (kl - 1):PAD_ROWS, :]

    @pl.when(scan_valid)
    def _():
        hn_ref[0] = hcar[...]
        sbn_ref[0] = cpad[PAD_ROWS - sbn_ref.shape[1]:PAD_ROWS, :]


def _mixmerge(proj, zm, x, h0, lb0, sb0, state_off, wts, l, *, n_streams, seq, width):
    rows, d = x.shape
    tm = _tile(seq, MERGE_TM, PACKED_ROWS)
    nt = seq // tm
    n = n_streams * nt
    kl, ks = wts["lru_conv_w"].shape[1], wts["sconv_w"].shape[1]
    gate_blk0 = (proj.shape[1] - N_BRANCH * d) // d

    def blk(lag):
        return lambda g: jnp.clip(g - lag, 0, n - 1)

    conv_blk, scan_blk, merge_blk = blk(0), blk(1), blk(1)

    def col(c, which):
        return pl.BlockSpec((tm, width), lambda g: (which(g), c))

    def state_in(k, which):
        return pl.BlockSpec((1, k, width), lambda g: (state_off + which(g) // nt, 0, 0))

    def state_out(k, which):
        return pl.BlockSpec((1, k, width), lambda g: (which(g) // nt, 0, 0))

    def gate(b):
        return pl.BlockSpec((tm, d), lambda g: (merge_blk(g), gate_blk0 + b))

    def act(w):
        return pl.BlockSpec((tm, w), lambda g: (merge_blk(g), 0))

    kern = functools.partial(_mixmerge_kernel, blocks_per_stream=nt, n_blocks=n)
    act_bf16 = pltpu.VMEM((tm, width), BF16)
    act_f32 = pltpu.VMEM((tm, width), F32)
    return pl.pallas_call(
        kern,
        grid=(n + 1,),
        in_specs=[col(0, conv_blk), col(1, scan_blk), col(2, scan_blk), col(3, scan_blk), col(4, scan_blk),
                  state_in(1, scan_blk), state_in(kl - 1, conv_blk), state_in(ks - 1, scan_blk)]
        + _mix_param_specs(wts, l, width)
        + [act(width), gate(0), gate(1), gate(2), act(d)] + _merge_weight_specs(wts, l, d),
        out_specs=[act(d), state_out(1, scan_blk), state_out(kl - 1, conv_blk), state_out(ks - 1, scan_blk)],
        out_shape=[jax.ShapeDtypeStruct((rows, d), F32),
                   jax.ShapeDtypeStruct((n_streams, 1, width), F32),
                   jax.ShapeDtypeStruct((n_streams, kl - 1, width), F32),
                   jax.ShapeDtypeStruct((n_streams, ks - 1, width), F32)],
        scratch_shapes=_mix_scratch(tm, width) + [act_f32, act_f32, act_bf16, act_bf16, pltpu.VMEM((tm, d), F32),
                                                  pltpu.VMEM((tm, d), BF16), pltpu.VMEM((tm, d), F32)],
        compiler_params=_params(("arbitrary",)),
        name="mixmerge",
    )(proj, proj, proj, proj, proj, h0, lb0, sb0, *_mix_param_args(wts),
      zm, proj, proj, proj, x, *_merge_weight_args(wts))


def _mlp_kernel(x_ref, gpre_ref, wu_ref, wd_ref, gpost_ref, *rest, cast_next):
    if cast_next:
        wnext_ref, o_ref, wnext_out_ref, xn_ref, acc_ref = rest
    else:
        o_ref, xn_ref, acc_ref = rest
    f = pl.program_id(1)
    last = pl.num_programs(1) - 1
    tm = x_ref.shape[0]

    def partial_down(rows):
        hid = jnp.dot(xn_ref[rows, :], wu_ref[...], preferred_element_type=F32)
        hid = jnp.square(jnp.maximum(hid, 0.0)).astype(BF16)
        return jnp.dot(hid, wd_ref[...], preferred_element_type=F32)

    def side_cast():
        if cast_next:
            wnext_out_ref[...] = wnext_ref[...].astype(wnext_out_ref.dtype)

    @pl.when(f == 0)
    def _():
        side_cast()
        xn_ref[...] = _rms(x_ref[...], gpre_ref[...]).astype(BF16)
        acc_ref[...] = partial_down(slice(None))

    @pl.when(jnp.logical_and(f > 0, f < last))
    def _():
        side_cast()
        acc_ref[...] += partial_down(slice(None))

    @pl.when(f == last)
    def _():
        side_cast()
        for half in (slice(0, tm // 2), slice(tm // 2, tm)):
            y = acc_ref[half, :] + partial_down(half)
            o_ref[half, :] = x_ref[half, :] + _rms(y, gpost_ref[...])


def _mlp(x, wts, l, *, w_next=None):
    rows, d = x.shape
    (w_up, up_l), (w_down, down_l) = wts["w_up"], wts["w_down"]
    dff = w_up.shape[2]
    tm, tf = _tile(rows, MLP_TM, 2 * SUBLANES), _tile(dff, MLP_TF, LANES)
    assert dff // tf >= 2
    grid = (rows // tm, dff // tf)
    row_blk = pl.BlockSpec((tm, d), lambda i, f: (i, 0))
    in_specs = [row_blk,
                _layer_spec(l, (1, d)),
                pl.BlockSpec((None, d, tf), lambda i, f: (up_l, 0, f)),
                pl.BlockSpec((None, tf, d), lambda i, f: (down_l, f, 0)),
                _layer_spec(l, (1, d))]
    args = [x, wts["g_mlp_pre"], w_up, w_down, wts["g_mlp_post"]]
    out_specs, out_shape = [row_blk], [jax.ShapeDtypeStruct((rows, d), F32)]
    cast_next = w_next is not None
    if cast_next:
        cast_in, cast_out = _cast_specs(l + 1, w_next.shape[1:], grid, 1)
        in_specs.append(cast_in)
        args.append(w_next)
        out_specs.append(cast_out)
        out_shape.append(jax.ShapeDtypeStruct((1,) + w_next.shape[1:], BF16))
    res = pl.pallas_call(
        functools.partial(_mlp_kernel, cast_next=cast_next),
        grid=grid,
        in_specs=in_specs,
        out_specs=out_specs,
        out_shape=out_shape,
        scratch_shapes=[pltpu.VMEM((tm, d), BF16), pltpu.VMEM((tm, d), F32)],
        compiler_params=_params(("arbitrary", "arbitrary")),
        name="mlp",
    )(*args)
    return (res[0], res[1]) if cast_next else (res[0], None)


def _block_diag_pack(w):
    depth, heads, dh, _ = w.shape
    per = max(1, min(heads, MXU_DIM // dh))
    n = heads // per
    wg = w.reshape(depth, n, per, dh, dh)
    eye = jnp.eye(per, dtype=w.dtype)
    out = jnp.einsum("lnhij,hg->lnhigj", wg, eye)
    return out.reshape(depth, n, per * dh, per * dh).astype(BF16)


def _run_layer(x, mem_k, mem_v, kv_off, h0, lb0, sb0, state_off, wts, l,
               *, n_streams, seq, width, heads, last, cast=None):
    if cast is not None:
        names = ("w_up", "w_down") + MERGE_WEIGHTS
        proj, *casts = _proj(x, wts, l, width=width, cast=tuple(cast[k] for k in names))
        wts = dict(wts, **{k: (w, 0) for k, w in zip(names, casts)})
    else:
        proj, = _proj(x, wts, l, width=width)
    zm = _attn(proj, mem_k, mem_v, kv_off, n_streams=n_streams, seq=seq, width=width, q_blk=5, heads=heads)
    if seq >= MERGE_TM and seq % MERGE_TM == 0:
        x, hn, lbn, sbn = _mixmerge(proj, zm, x, h0, lb0, sb0, state_off, wts, l,
                                    n_streams=n_streams, seq=seq, width=width)
    else:
        zl, zc, hn, lbn, sbn = _mix(proj, h0, lb0, sb0, state_off, wts, l,
                                    n_streams=n_streams, seq=seq, width=width)
        x = _merge(zl, zc, zm, proj, x, wts, l, width=width)
    w_next = cast["w_in"] if cast is not None and not last else None
    x, w_in_next = _mlp(x, wts, l, w_next=w_next)
    return x, hn, lbn, sbn, wts, w_in_next


def kernel(x_prompt, x_sample, mem_prompt, state_lru_h, state_lru_conv, state_sconv, cache_mem_k, cache_mem_v, g_mix_pre, w_in, b_gate, lru_conv_w, lru_conv_b, lru_w_a, lru_b_a, lru_w_i, lru_b_i, lru_lambda, w_branch_lru, sconv_w, w_branch_conv, g_mem, w_mem_kv, w_branch_mem, w_out, g_mix_post, g_mlp_pre, w_up, w_down, g_mlp_post):
    depth = w_in.shape[0]
    bp, sp, d = x_prompt.shape
    bs, ss, _ = x_sample.shape
    width = lru_lambda.shape[1]
    mlen, heads, dh = cache_mem_k.shape[2:]
    kl, ks = lru_conv_w.shape[1], sconv_w.shape[1]

    def rowvec(p):
        return p.reshape(depth, 1, -1)

    wts = dict(
        g_mix_pre=rowvec(g_mix_pre), w_in=(w_in[0:1].astype(BF16), 0), b_gate=rowvec(b_gate),
        lru_conv_w=lru_conv_w, lru_conv_b=rowvec(lru_conv_b),
        wa_bd=_block_diag_pack(lru_w_a), lru_b_a=rowvec(lru_b_a),
        wi_bd=_block_diag_pack(lru_w_i), lru_b_i=rowvec(lru_b_i),
        lru_lambda=rowvec(lru_lambda), sconv_w=sconv_w,
        g_mix_post=rowvec(g_mix_post), g_mlp_pre=rowvec(g_mlp_pre),
        g_mlp_post=rowvec(g_mlp_post),
        g_mem=rowvec(g_mem), w_mem_kv=w_mem_kv)

    hp = x_prompt.reshape(bp * sp, d)
    hs = x_sample.reshape(bs * ss, d)
    mem2d = mem_prompt.reshape(bp * mlen, d)
    p_h0 = jnp.zeros((bp, 1, width), F32)
    p_lb0 = jnp.zeros((bp, kl - 1, width), F32)
    p_sb0 = jnp.zeros((bp, ks - 1, width), F32)
    s_h0 = state_lru_h.reshape(depth * bs, 1, width)
    s_lb0 = state_lru_conv.reshape(depth * bs, kl - 1, width)
    s_sb0 = state_sconv.reshape(depth * bs, ks - 1, width)
    s_mk = cache_mem_k.reshape(depth * bs, mlen, heads, dh)
    s_mv = cache_mem_v.reshape(depth * bs, mlen, heads, dh)

    outs = {k: [] for k in ("p_h", "p_lb", "p_sb", "p_mk", "p_mv", "s_h", "s_lb", "s_sb")}
    f32_weights = dict(w_up=w_up, w_down=w_down, w_branch_lru=w_branch_lru, w_branch_conv=w_branch_conv,
                       w_branch_mem=w_branch_mem, w_out=w_out, w_in=w_in)
    for l in range(depth):
        last = l == depth - 1
        kv = _memkv(mem2d, wts, l, width=width)
        mk = kv[0].reshape(bp, mlen, width)
        mv = kv[1].reshape(bp, mlen, width)
        hp, hn, lbn, sbn, wts, w_in_next = _run_layer(
            hp, mk, mv, 0, p_h0, p_lb0, p_sb0, 0, wts, l,
            n_streams=bp, seq=sp, width=width, heads=heads, last=last, cast=f32_weights)
        outs["p_h"].append(hn.reshape(bp, width))
        outs["p_lb"].append(lbn)
        outs["p_sb"].append(sbn)
        outs["p_mk"].append(mk.reshape(bp, mlen, heads, dh))
        outs["p_mv"].append(mv.reshape(bp, mlen, heads, dh))
        hs, hn, lbn, sbn, _, _ = _run_layer(
            hs, s_mk, s_mv, l * bs, s_h0, s_lb0, s_sb0, l * bs, wts, l,
            n_streams=bs, seq=ss, width=width, heads=heads, last=last)
        outs["s_h"].append(hn.reshape(bs, width))
        outs["s_lb"].append(lbn)
        outs["s_sb"].append(sbn)
        if not last:
            wts = dict(wts, w_in=(w_in_next, 0))

    return (hp.reshape(bp, sp, d), hs.reshape(bs, ss, d),
            jnp.stack(outs["p_h"]), jnp.stack(outs["p_lb"]), jnp.stack(outs["p_sb"]),
            jnp.stack(outs["p_mk"]), jnp.stack(outs["p_mv"]),
            jnp.stack(outs["s_h"]), jnp.stack(outs["s_lb"]), jnp.stack(outs["s_sb"]))
```

```python
import functools

import jax
import jax.numpy as jnp
from jax import lax
from jax.experimental import pallas as pl
from jax.experimental.pallas import tpu as pltpu

F32 = jnp.float32
BF16 = jnp.bfloat16

EPS = 1e-6
LRU_C = 8.0
N_BRANCH = 3

SUBLANES = 8
LANES = 128
PACKED_ROWS = 16
MXU_DIM = 256
VMEM_LIMIT_BYTES = 56 * 1024 * 1024

PROJ_TM = 1024
MIX_ROWS = 512
ATTN_ROWS = 1024
ATTN_KV_BYTES = 4 * 1024 * 1024
MERGE_TM = 256
MERGE_TN = 512
MLP_TM, MLP_TF = 512, 1024
PAD_ROWS = SUBLANES
SCAN_ROWS = 32


def _tile(n, pref, mult=SUBLANES):
    if n <= pref:
        return n
    for t in range(pref, 0, -1):
        if n % t == 0 and t % mult == 0:
            return t
    raise ValueError(f"no tile for {n} <= {pref}")


def _params(sem):
    return pltpu.CompilerParams(dimension_semantics=sem, vmem_limit_bytes=VMEM_LIMIT_BYTES)


def _rms(x, g):
    var = jnp.mean(x * x, axis=-1, keepdims=True)
    return (x * lax.rsqrt(var + EPS)) * g


def _sigmoid(x):
    return 0.5 * jnp.tanh(0.5 * x) + 0.5


def _layer_spec(l, shape, pipeline_mode=None):
    nd = len(shape)
    kw = {} if pipeline_mode is None else {"pipeline_mode": pipeline_mode}
    return pl.BlockSpec((None,) + tuple(shape), lambda *_: (l,) + (0,) * nd, **kw)


def _split(n, parts_max, mult):
    for p in range(parts_max, 0, -1):
        if n % p == 0 and (n // p) % mult == 0:
            return p
    raise ValueError(f"cannot split {n} into <= {parts_max} parts of multiples of {mult}")


def _cast_specs(l, shape, grid, row_axis):
    r, c = shape
    col_axis = 1 - row_axis
    pr = _split(r, grid[row_axis], PACKED_ROWS)
    pc = _split(c, grid[col_axis], LANES)
    blk = (r // pr, c // pc)

    def idx(*g):
        return jnp.minimum(g[row_axis], pr - 1), jnp.minimum(g[col_axis], pc - 1)

    return (pl.BlockSpec((None,) + blk, lambda *g: (l,) + idx(*g)),
            pl.BlockSpec((None,) + blk, lambda *g: (0,) + idx(*g)))


def _proj_kernel(x_ref, g_ref, w_ref, bg_ref, *rest, gelu_blk, gate_blk0):
    n_cast = (len(rest) - 2) // 2
    cast_in, o_ref, cast_out, xn_ref = rest[:n_cast], rest[n_cast], rest[n_cast + 1:-1], rest[-1]
    j = pl.program_id(1)

    def side_casts():
        for src, dst in zip(cast_in, cast_out):
            dst[...] = src[...].astype(dst.dtype)

    @pl.when(j == 0)
    def _():
        side_casts()
        tm = x_ref.shape[0]
        for half in (slice(0, tm // 2), slice(tm // 2, tm)):
            xn_ref[half, :] = _rms(x_ref[half, :], g_ref[...]).astype(BF16)
            o_ref[half, :] = jnp.dot(xn_ref[half, :], w_ref[...], preferred_element_type=F32).astype(o_ref.dtype)

    def dot():
        side_casts()
        return jnp.dot(xn_ref[...], w_ref[...], preferred_element_type=F32)

    @pl.when(j == gelu_blk)
    def _():
        o_ref[...] = jax.nn.gelu(dot()).astype(o_ref.dtype)

    @pl.when(j >= gate_blk0)
    def _():
        o_ref[...] = _sigmoid(dot() + bg_ref[...]).astype(o_ref.dtype)

    @pl.when(jnp.logical_and(jnp.logical_and(j != 0, j != gelu_blk), j < gate_blk0))
    def _():
        o_ref[...] = dot().astype(o_ref.dtype)


def _proj(x, wts, l, *, width, cast=()):
    rows, d = x.shape
    w_in, w_l = wts["w_in"]
    cols = w_in.shape[2]
    tm, tn = _tile(rows, PROJ_TM, 2 * PACKED_ROWS), width
    gate_blk0 = (cols - N_BRANCH * d) // tn
    assert tm % (2 * PACKED_ROWS) == 0 and gate_blk0 > 1
    grid = (rows // tm, cols // tn)
    kern = functools.partial(_proj_kernel, gelu_blk=1, gate_blk0=gate_blk0)
    cast_specs = [_cast_specs(l, w.shape[1:], grid, 0) for w in cast]
    return pl.pallas_call(
        kern,
        grid=grid,
        in_specs=[
            pl.BlockSpec((tm, d), lambda i, j: (i, 0)),
            _layer_spec(l, (1, d)),
            pl.BlockSpec((None, d, tn), lambda i, j: (w_l, 0, j)),
            pl.BlockSpec((None, 1, tn), lambda i, j: (l, 0, jnp.maximum(j - gate_blk0, 0))),
        ] + [cs[0] for cs in cast_specs],
        out_specs=[pl.BlockSpec((tm, tn), lambda i, j: (i, j))] + [cs[1] for cs in cast_specs],
        out_shape=[jax.ShapeDtypeStruct((rows, cols), BF16)]
        + [jax.ShapeDtypeStruct((1,) + w.shape[1:], BF16) for w in cast],
        scratch_shapes=[pltpu.VMEM((tm, d), BF16)],
        compiler_params=_params(("arbitrary", "arbitrary")),
        name="proj",
    )(x, wts["g_mix_pre"], w_in, wts["b_gate"], *cast)


def _memkv_kernel(x_ref, g_ref, w_ref, o_ref, xn_ref):
    @pl.when(pl.program_id(1) == 0)
    def _():
        xn_ref[...] = _rms(x_ref[...], g_ref[...]).astype(BF16)

    o_ref[...] = jnp.dot(xn_ref[...], w_ref[...].astype(BF16), preferred_element_type=F32)


def _memkv(mem, wts, l, *, width):
    rows, d = mem.shape
    cols = wts["w_mem_kv"].shape[2]
    tm, tn = _tile(rows, PROJ_TM), width
    return pl.pallas_call(
        _memkv_kernel,
        grid=(rows // tm, cols // tn),
        in_specs=[
            pl.BlockSpec((tm, d), lambda i, j: (i, 0)),
            _layer_spec(l, (1, d)),
            pl.BlockSpec((None, d, tn), lambda i, j: (l, 0, j)),
        ],
        out_specs=pl.BlockSpec((None, tm, tn), lambda i, j: (j, i, 0)),
        out_shape=jax.ShapeDtypeStruct((cols // tn, rows, tn), F32),
        scratch_shapes=[pltpu.VMEM((tm, d), BF16)],
        compiler_params=_params(("parallel", "arbitrary")),
        name="memkv",
    )(mem, wts["g_mem"], wts["w_mem_kv"])


def _softplus(x):
    return jnp.maximum(x, 0.0) + jnp.log1p(jnp.exp(-jnp.abs(x)))


def _causal_conv(pad_ref, w_ref, rows):
    taps = w_ref.shape[0]
    xe = pad_ref[...]
    y = None
    for k in range(taps):
        back = taps - 1 - k
        xk = xe if back == 0 else pltpu.roll(xe, back, axis=0)
        term = xk[PAD_ROWS:PAD_ROWS + rows, :] * w_ref[k:k + 1, :]
        y = term if y is None else y + term
    return y


def _mix_init(h0_ref, lb0_ref, sb0_ref, xpad, cpad, hcar):
    width = xpad.shape[1]
    xpad[0:PAD_ROWS, :] = jnp.zeros((PAD_ROWS, width), F32)
    cpad[0:PAD_ROWS, :] = jnp.zeros((PAD_ROWS, width), F32)
    xpad[PAD_ROWS - lb0_ref.shape[1]:PAD_ROWS, :] = lb0_ref[0]
    cpad[PAD_ROWS - sb0_ref.shape[1]:PAD_ROWS, :] = sb0_ref[0]
    hcar[...] = h0_ref[0]


def _mix_state_out(hn_ref, lbn_ref, sbn_ref, xpad, cpad, hcar):
    hn_ref[0] = hcar[...]
    lbn_ref[0] = xpad[PAD_ROWS - lbn_ref.shape[1]:PAD_ROWS, :]
    sbn_ref[0] = cpad[PAD_ROWS - sbn_ref.shape[1]:PAD_ROWS, :]


def _mix_lru_conv(lx_ref, cw_ref, cb_ref, xpad, uscr):
    rows = lx_ref.shape[0]
    kl = cw_ref.shape[0]
    xpad[PAD_ROWS:PAD_ROWS + rows, :] = lx_ref[...].astype(F32)
    uscr[...] = _causal_conv(xpad, cw_ref, rows) + cb_ref[...]
    xpad[PAD_ROWS - (kl - 1):PAD_ROWS, :] = xpad[PAD_ROWS + rows - (kl - 1):PAD_ROWS + rows, :]


def _mix_gate_dots(uscr, wa_ref, ba_ref, wi_ref, bi_ref, acum, bcum):
    n_chunks = wa_ref.shape[0]
    bd = uscr.shape[1] // n_chunks
    for c in range(n_chunks):
        sl = slice(c * bd, (c + 1) * bd)
        ub = uscr[:, sl].astype(BF16)
        acum[:, sl] = jnp.dot(ub, wa_ref[c], preferred_element_type=F32) + ba_ref[:, sl]
        bcum[:, sl] = jnp.dot(ub, wi_ref[c], preferred_element_type=F32) + bi_ref[:, sl]


def _mix_sconv(sb_ref, sc_ref, sh_ref, scw_ref, zc_ref, cpad):
    rows = sc_ref.shape[0]
    ks = scw_ref.shape[0]
    cpad[PAD_ROWS:PAD_ROWS + rows, :] = sc_ref[...].astype(F32) * sh_ref[...].astype(F32)
    cv = _causal_conv(cpad, scw_ref, rows)
    zc_ref[...] = (sb_ref[...].astype(F32) * cv).astype(zc_ref.dtype)
    cpad[PAD_ROWS - (ks - 1):PAD_ROWS, :] = cpad[PAD_ROWS + rows - (ks - 1):PAD_ROWS + rows, :]


def _mix_scan_chunk(c, n_chunks, lam_ref, uscr, ya_s, yi_s, acum, bcum):
    rows, width = uscr.shape
    bd = width // n_chunks
    sl = slice(c * bd, (c + 1) * bd)
    neg_c_sp = -LRU_C * _softplus(-lam_ref[:, sl])
    tr = _tile(rows, SCAN_ROWS)
    groups = tr // SUBLANES
    row = lax.broadcasted_iota(jnp.int32, (groups, SUBLANES, bd), 1)
    for r0 in range(0, rows, tr):
        rs = slice(r0, r0 + tr)
        r = _sigmoid(ya_s[rs, sl])
        ig = _sigmoid(yi_s[rs, sl])
        log_a = neg_c_sp * r
        a = jnp.exp(log_a)
        th = jnp.tanh(log_a)
        mult = jnp.sqrt(-2.0 * th) * lax.rsqrt(1.0 - th)
        b = mult * (ig * uscr[rs, sl])
        a3 = a.reshape(groups, SUBLANES, bd)
        b3 = b.reshape(groups, SUBLANES, bd)
        d = 1
        while d < SUBLANES:
            a_prev = pltpu.roll(a3, d, axis=1)
            b_prev = pltpu.roll(b3, d, axis=1)
            take = row >= d
            b3 = jnp.where(take, a3 * b_prev + b3, b3)
            a3 = jnp.where(take, a3 * a_prev, a3)
            d *= 2
        acum[rs, sl] = a3.reshape(tr, bd)
        bcum[rs, sl] = b3.reshape(tr, bd)


def _mix_carry(lo, hi, gg_ref, zl_ref, acum, bcum, hcar, *, unroll):
    def body(g, h):
        r0 = pl.multiple_of(g * PACKED_ROWS, PACKED_ROWS)
        r1 = pl.multiple_of(r0 + SUBLANES, SUBLANES)
        h1 = acum[pl.ds(r0, SUBLANES), :] * h + bcum[pl.ds(r0, SUBLANES), :]
        h2 = acum[pl.ds(r1, SUBLANES), :] * h1[SUBLANES - 1:SUBLANES, :] + bcum[pl.ds(r1, SUBLANES), :]
        hh = jnp.concatenate([h1, h2], axis=0)
        gate = gg_ref[pl.ds(r0, PACKED_ROWS), :].astype(F32)
        zl_ref[pl.ds(r0, PACKED_ROWS), :] = (hh * gate).astype(zl_ref.dtype)
        return h2[SUBLANES - 1:SUBLANES, :]

    hcar[...] = lax.fori_loop(lo, hi, body, hcar[...], unroll=unroll)


def _mix_kernel(lx_ref, gg_ref, sb_ref, sc_ref, sh_ref, h0_ref, lb0_ref, sb0_ref,
                cw_ref, cb_ref, wa_ref, ba_ref, wi_ref, bi_ref, lam_ref, scw_ref,
                zl_ref, zc_ref, hn_ref, lbn_ref, sbn_ref,
                xpad, cpad, acum, bcum, hcar, uscr):
    @pl.when(pl.program_id(1) == 0)
    def _():
        _mix_init(h0_ref, lb0_ref, sb0_ref, xpad, cpad, hcar)

    _mix_lru_conv(lx_ref, cw_ref, cb_ref, xpad, uscr)
    _mix_gate_dots(uscr, wa_ref, ba_ref, wi_ref, bi_ref, acum, bcum)
    _mix_sconv(sb_ref, sc_ref, sh_ref, scw_ref, zc_ref, cpad)
    n_chunks = wa_ref.shape[0]
    for c in range(n_chunks):
        _mix_scan_chunk(c, n_chunks, lam_ref, uscr, acum, bcum, acum, bcum)
    _mix_carry(0, lx_ref.shape[0] // PACKED_ROWS, gg_ref, zl_ref, acum, bcum, hcar, unroll=False)
    _mix_state_out(hn_ref, lbn_ref, sbn_ref, xpad, cpad, hcar)


def _mix_scratch(rows, width):
    return [pltpu.VMEM((PAD_ROWS + rows, width), F32),
            pltpu.VMEM((PAD_ROWS + rows, width), F32),
            pltpu.VMEM((rows, width), F32),
            pltpu.VMEM((rows, width), F32),
            pltpu.VMEM((1, width), F32),
            pltpu.VMEM((rows, width), F32)]


def _mix_param_specs(wts, l, width):
    kl, ks = wts["lru_conv_w"].shape[1], wts["sconv_w"].shape[1]
    nck = wts["wa_bd"].shape[1]
    bd_spec = _layer_spec(l, (nck, width // nck, width // nck))
    return [_layer_spec(l, (kl, width)), _layer_spec(l, (1, width)),
            bd_spec, _layer_spec(l, (1, width)), bd_spec, _layer_spec(l, (1, width)),
            _layer_spec(l, (1, width)), _layer_spec(l, (ks, width))]


def _mix_param_args(wts):
    return (wts["lru_conv_w"], wts["lru_conv_b"], wts["wa_bd"], wts["lru_b_a"], wts["wi_bd"], wts["lru_b_i"],
            wts["lru_lambda"], wts["sconv_w"])


def _mix(proj, h0, lb0, sb0, state_off, wts, l, *, n_streams, seq, width):
    rows = _tile(seq, MIX_ROWS, PACKED_ROWS)
    nt = seq // rows
    kl, ks = wts["lru_conv_w"].shape[1], wts["sconv_w"].shape[1]

    def col(c):
        return pl.BlockSpec((rows, width), lambda s, t: (s * nt + t, c))

    def state_in(k):
        return pl.BlockSpec((1, k, width), lambda s, t: (state_off + s, 0, 0))

    def state_out(k):
        return pl.BlockSpec((1, k, width), lambda s, t: (s, 0, 0))

    total = n_streams * seq
    return pl.pallas_call(
        _mix_kernel,
        grid=(n_streams, nt),
        in_specs=[col(0), col(1), col(2), col(3), col(4),
                  state_in(1), state_in(kl - 1), state_in(ks - 1)] + _mix_param_specs(wts, l, width),
        out_specs=[pl.BlockSpec((rows, width), lambda s, t: (s * nt + t, 0)),
                   pl.BlockSpec((rows, width), lambda s, t: (s * nt + t, 0)),
                   state_out(1), state_out(kl - 1), state_out(ks - 1)],
        out_shape=[jax.ShapeDtypeStruct((total, width), BF16),
                   jax.ShapeDtypeStruct((total, width), BF16),
                   jax.ShapeDtypeStruct((n_streams, 1, width), F32),
                   jax.ShapeDtypeStruct((n_streams, kl - 1, width), F32),
                   jax.ShapeDtypeStruct((n_streams, ks - 1, width), F32)],
        scratch_shapes=_mix_scratch(rows, width),
        compiler_params=_params(("parallel", "arbitrary")),
        name="mix",
    )(proj, proj, proj, proj, proj, h0, lb0, sb0, *_mix_param_args(wts))


def _attn_kernel(q_ref, k_ref, v_ref, o_ref, *, heads, seq_rows):
    dh = q_ref.shape[1] // heads
    scale = dh ** -0.5
    if len(k_ref.shape) == 4:
        mh = k_ref.shape[1] * heads
        shape = (heads * seq_rows, mh)
        row = lax.broadcasted_iota(jnp.int32, shape, 0)
        row_head = sum((row >= h * seq_rows).astype(jnp.int32) for h in range(1, heads))
        own = row_head == lax.broadcasted_iota(jnp.int32, shape, 1) % heads
        for g in range(k_ref.shape[0]):
            rs = slice(g * seq_rows, (g + 1) * seq_rows)
            kf = k_ref[g].reshape(mh, dh).astype(BF16)
            vf = v_ref[g].reshape(mh, dh).astype(BF16)
            qs = jnp.concatenate([q_ref[rs, h * dh:(h + 1) * dh] for h in range(heads)], axis=0)
            s = lax.dot_general(qs, kf, (((1,), (1,)), ((), ())), preferred_element_type=F32) * scale
            s = jnp.where(own, s, -jnp.inf)
            e = jnp.where(own, jnp.exp(s - jnp.max(s, axis=-1, keepdims=True)), 0.0)
            p = e * (1.0 / jnp.sum(e, axis=-1, keepdims=True))
            o = jnp.dot(p.astype(BF16), vf, preferred_element_type=F32).astype(o_ref.dtype)
            for h in range(heads):
                o_ref[rs, h * dh:(h + 1) * dh] = o[h * seq_rows:(h + 1) * seq_rows, :]
        return
    for g in range(k_ref.shape[0]):
        rs = slice(g * seq_rows, (g + 1) * seq_rows)
        for h in range(heads):
            sl = slice(h * dh, (h + 1) * dh)
            kh = k_ref[g, :, sl].astype(BF16)
            vh = v_ref[g, :, sl].astype(BF16)
            s = lax.dot_general(q_ref[rs, sl], kh, (((1,), (1,)), ((), ())),
                                preferred_element_type=F32) * scale
            e = jnp.exp(s - jnp.max(s, axis=-1, keepdims=True))
            p = e * (1.0 / jnp.sum(e, axis=-1, keepdims=True))
            o_ref[rs, sl] = jnp.dot(p.astype(BF16), vh, preferred_element_type=F32).astype(o_ref.dtype)


def _attn(proj, mem_k, mem_v, kv_off, *, n_streams, seq, width, q_blk, heads):
    mlen = mem_k.shape[1]
    rows = _tile(seq, ATTN_ROWS)
    nt = seq // rows
    per_step = 1
    if nt == 1:
        per_step = _tile(n_streams, max(1, ATTN_KV_BYTES // (mlen * width * 4)), 1)
    assert kv_off % per_step == 0
    blk = per_step * rows
    kv_shape = (per_step,) + mem_k.shape[1:]
    kv_spec = pl.BlockSpec(kv_shape, lambda s, t: (kv_off // per_step + s,) + (0,) * (len(kv_shape) - 1))
    return pl.pallas_call(
        functools.partial(_attn_kernel, heads=heads, seq_rows=rows),
        grid=(n_streams // per_step, nt),
        in_specs=[pl.BlockSpec((blk, width), lambda s, t: (s * nt + t, q_blk)), kv_spec, kv_spec],
        out_specs=pl.BlockSpec((blk, width), lambda s, t: (s * nt + t, 0)),
        out_shape=jax.ShapeDtypeStruct((n_streams * seq, width), BF16),
        compiler_params=_params(("parallel", "arbitrary")),
        name="attn",
    )(proj, mem_k, mem_v)


def _merge_m_tile(cs, zl_ref, zc_ref, zm_ref, g0_ref, g1_ref, g2_ref, wl_ref, wc_ref, wm_ref):
    m = g0_ref[:, cs].astype(F32) * jnp.dot(zl_ref[...], wl_ref[:, cs], preferred_element_type=F32)
    m = m + g1_ref[:, cs].astype(F32) * jnp.dot(zc_ref[...], wc_ref[:, cs], preferred_element_type=F32)
    m = m + g2_ref[:, cs].astype(F32) * jnp.dot(zm_ref[...], wm_ref[:, cs], preferred_element_type=F32)
    return m.astype(BF16)


def _merge_kernel(zl_ref, zc_ref, zm_ref, g0_ref, g1_ref, g2_ref, x_ref,
                  wl_ref, wc_ref, wm_ref, wo_ref, gp_ref, o_ref):
    m = _merge_m_tile(slice(None), zl_ref, zc_ref, zm_ref, g0_ref, g1_ref, g2_ref, wl_ref, wc_ref, wm_ref)
    y = jnp.dot(m, wo_ref[...], preferred_element_type=F32)
    o_ref[...] = x_ref[...] + _rms(y, gp_ref[...])


MERGE_WEIGHTS = ("w_branch_lru", "w_branch_conv", "w_branch_mem", "w_out")


def _merge_weight_specs(wts, l, d):
    once = pl.Buffered(1)
    specs = [_layer_spec(wts[k][1], wts[k][0].shape[1:], once) for k in MERGE_WEIGHTS]
    return specs + [_layer_spec(l, (1, d), once)]


def _merge_weight_args(wts):
    return tuple(wts[k][0] for k in MERGE_WEIGHTS) + (wts["g_mix_post"],)


def _merge(zl, zc, zm, proj, x, wts, l, *, width):
    rows, d = x.shape
    tm = _tile(rows, MERGE_TM)
    gate_blk0 = (proj.shape[1] - N_BRANCH * d) // d

    def act(w):
        return pl.BlockSpec((tm, w), lambda i: (i, 0))

    def gate(b):
        return pl.BlockSpec((tm, d), lambda i: (i, gate_blk0 + b))

    return pl.pallas_call(
        _merge_kernel,
        grid=(rows // tm,),
        in_specs=[act(width), act(width), act(width), gate(0), gate(1), gate(2), act(d)]
        + _merge_weight_specs(wts, l, d),
        out_specs=act(d),
        out_shape=jax.ShapeDtypeStruct((rows, d), F32),
        compiler_params=_params(("parallel",)),
        name="merge",
    )(zl, zc, zm, proj, proj, proj, x, *_merge_weight_args(wts))


def _mixmerge_kernel(lx_ref, gg_ref, sb_ref, sc_ref, sh_ref, h0_ref, lb0_ref, sb0_ref,
                     cw_ref, cb_ref, wa_ref, ba_ref, wi_ref, bi_ref, lam_ref, scw_ref,
                     zm_ref, g0_ref, g1_ref, g2_ref, x_ref, wl_ref, wc_ref, wm_ref, wo_ref, gp_ref,
                     o_ref, hn_ref, lbn_ref, sbn_ref,
                     xpad, cpad, acum, bcum, hcar, uscr, ya_s, yi_s, zl_s, zc_s, m_f, m_s, y_s,
                     *, blocks_per_stream, n_blocks):
    g = pl.program_id(0)
    kl = cw_ref.shape[0]

    @pl.when(g == 0)
    def _():
        for ref in (cpad, hcar, uscr, ya_s, yi_s):
            ref[...] = jnp.zeros(ref.shape, ref.dtype)

    @pl.when(jnp.logical_and(g % blocks_per_stream == 0, g < n_blocks))
    def _():
        xpad[0:PAD_ROWS, :] = jnp.zeros((PAD_ROWS, xpad.shape[1]), F32)
        xpad[PAD_ROWS - (kl - 1):PAD_ROWS, :] = lb0_ref[0]

    scan_valid = g >= 1

    @pl.when(jnp.logical_and((g - 1) % blocks_per_stream == 0, scan_valid))
    def _():
        cpad[0:PAD_ROWS, :] = jnp.zeros((PAD_ROWS, cpad.shape[1]), F32)
        cpad[PAD_ROWS - sb0_ref.shape[1]:PAD_ROWS, :] = sb0_ref[0]
        hcar[...] = h0_ref[0]

    rows, d = x_ref.shape
    n_chunks = wa_ref.shape[0]
    col_tiles = [slice(j, j + MERGE_TN) for j in range(0, d, MERGE_TN)]

    def gated(gate_ref, z_ref, w_ref, cs):
        return gate_ref[:, cs].astype(F32) * jnp.dot(z_ref[...], w_ref[:, cs], preferred_element_type=F32)

    tiles = rows // PACKED_ROWS
    vector_pieces = [functools.partial(_mix_scan_chunk, c, n_chunks, lam_ref, uscr, ya_s, yi_s, acum, bcum)
                     for c in range(n_chunks)]
    vector_pieces += [functools.partial(_mix_carry, k * tiles // 2, (k + 1) * tiles // 2,
                                        gg_ref, zl_s, acum, bcum, hcar, unroll=True) for k in range(2)]
    _mix_sconv(sb_ref, sc_ref, sh_ref, scw_ref, zc_s, cpad)
    for cs in col_tiles:
        m_f[:, cs] = gated(g2_ref, zm_ref, wm_ref, cs)
        if vector_pieces:
            vector_pieces.pop(0)()
    for cs in col_tiles:
        m_f[:, cs] += gated(g1_ref, zc_s, wc_ref, cs)
        if vector_pieces:
            vector_pieces.pop(0)()
    while vector_pieces:
        vector_pieces.pop(0)()
    for cs in col_tiles:
        m_s[:, cs] = (m_f[:, cs] + gated(g0_ref, zl_s, wl_ref, cs)).astype(BF16)
    for k, cs in enumerate(col_tiles):
        y_s[:, cs] = jnp.dot(m_s[...], wo_ref[:, cs], preferred_element_type=F32)
        if k == 0:
            _mix_lru_conv(lx_ref, cw_ref, cb_ref, xpad, uscr)
    _mix_gate_dots(uscr, wa_ref, ba_ref, wi_ref, bi_ref, ya_s, yi_s)
    o_ref[...] = x_ref[...] + _rms(y_s[...], gp_ref[...])

    @pl.when(g < n_blocks)
    def _():
        lbn_ref[0] = xpad[PAD_ROWS - (kl - 1):PAD_ROWS, :]

    @pl.when(scan_valid)
    def _():
        hn_ref[0] = hcar[...]
        sbn_ref[0] = cpad[PAD_ROWS - sbn_ref.shape[1]:PAD_ROWS, :]


def _mixmerge(proj, zm, x, h0, lb0, sb0, state_off, wts, l, *, n_streams, seq, width):
    rows, d = x.shape
    tm = _tile(seq, MERGE_TM, PACKED_ROWS)
    nt = seq // tm
    n = n_streams * nt
    kl, ks = wts["lru_conv_w"].shape[1], wts["sconv_w"].shape[1]
    gate_blk0 = (proj.shape[1] - N_BRANCH * d) // d

    def blk(lag):
        return lambda g: jnp.clip(g - lag, 0, n - 1)

    conv_blk, scan_blk, merge_blk = blk(0), blk(1), blk(1)

    def col(c, which):
        return pl.BlockSpec((tm, width), lambda g: (which(g), c))

    def state_in(k, which):
        return pl.BlockSpec((1, k, width), lambda g: (state_off + which(g) // nt, 0, 0))

    def state_out(k, which):
        return pl.BlockSpec((1, k, width), lambda g: (which(g) // nt, 0, 0))

    def gate(b):
        return pl.BlockSpec((tm, d), lambda g: (merge_blk(g), gate_blk0 + b))

    def act(w):
        return pl.BlockSpec((tm, w), lambda g: (merge_blk(g), 0))

    kern = functools.partial(_mixmerge_kernel, blocks_per_stream=nt, n_blocks=n)
    act_bf16 = pltpu.VMEM((tm, width), BF16)
    act_f32 = pltpu.VMEM((tm, width), F32)
    return pl.pallas_call(
        kern,
        grid=(n + 1,),
        in_specs=[col(0, conv_blk), col(1, scan_blk), col(2, scan_blk), col(3, scan_blk), col(4, scan_blk),
                  state_in(1, scan_blk), state_in(kl - 1, conv_blk), state_in(ks - 1, scan_blk)]
        + _mix_param_specs(wts, l, width)
        + [act(width), gate(0), gate(1), gate(2), act(d)] + _merge_weight_specs(wts, l, d),
        out_specs=[act(d), state_out(1, scan_blk), state_out(kl - 1, conv_blk), state_out(ks - 1, scan_blk)],
        out_shape=[jax.ShapeDtypeStruct((rows, d), F32),
                   jax.ShapeDtypeStruct((n_streams, 1, width), F32),
                   jax.ShapeDtypeStruct((n_streams, kl - 1, width), F32),
                   jax.ShapeDtypeStruct((n_streams, ks - 1, width), F32)],
        scratch_shapes=_mix_scratch(tm, width) + [act_f32, act_f32, act_bf16, act_bf16, pltpu.VMEM((tm, d), F32),
                                                  pltpu.VMEM((tm, d), BF16), pltpu.VMEM((tm, d), F32)],
        compiler_params=_params(("arbitrary",)),
        name="mixmerge",
    )(proj, proj, proj, proj, proj, h0, lb0, sb0, *_mix_param_args(wts),
      zm, proj, proj, proj, x, *_merge_weight_args(wts))


def _mlp_kernel(x_ref, gpre_ref, wu_ref, wd_ref, gpost_ref, *rest, cast_next):
    if cast_next:
        wnext_ref, o_ref, wnext_out_ref, xn_ref, acc_ref = rest
    else:
        o_ref, xn_ref, acc_ref = rest
    f = pl.program_id(1)
    last = pl.num_programs(1) - 1
    tm = x_ref.shape[0]

    def partial_down(rows):
        hid = jnp.dot(xn_ref[rows, :], wu_ref[...], preferred_element_type=F32)
        hid = jnp.square(jnp.maximum(hid, 0.0)).astype(BF16)
        return jnp.dot(hid, wd_ref[...], preferred_element_type=F32)

    def side_cast():
        if cast_next:
            wnext_out_ref[...] = wnext_ref[...].astype(wnext_out_ref.dtype)

    @pl.when(f == 0)
    def _():
        side_cast()
        xn_ref[...] = _rms(x_ref[...], gpre_ref[...]).astype(BF16)
        acc_ref[...] = partial_down(slice(None))

    @pl.when(jnp.logical_and(f > 0, f < last))
    def _():
        side_cast()
        acc_ref[...] += partial_down(slice(None))

    @pl.when(f == last)
    def _():
        side_cast()
        for half in (slice(0, tm // 2), slice(tm // 2, tm)):
            y = acc_ref[half, :] + partial_down(half)
            o_ref[half, :] = x_ref[half, :] + _rms(y, gpost_ref[...])


def _mlp(x, wts, l, *, w_next=None):
    rows, d = x.shape
    (w_up, up_l), (w_down, down_l) = wts["w_up"], wts["w_down"]
    dff = w_up.shape[2]
    tm, tf = _tile(rows, MLP_TM, 2 * SUBLANES), _tile(dff, MLP_TF, LANES)
    assert dff // tf >= 2
    grid = (rows // tm, dff // tf)
    row_blk = pl.BlockSpec((tm, d), lambda i, f: (i, 0))
    in_specs = [row_blk,
                _layer_spec(l, (1, d)),
                pl.BlockSpec((None, d, tf), lambda i, f: (up_l, 0, f)),
                pl.BlockSpec((None, tf, d), lambda i, f: (down_l, f, 0)),
                _layer_spec(l, (1, d))]
    args = [x, wts["g_mlp_pre"], w_up, w_down, wts["g_mlp_post"]]
    out_specs, out_shape = [row_blk], [jax.ShapeDtypeStruct((rows, d), F32)]
    cast_next = w_next is not None
    if cast_next:
        cast_in, cast_out = _cast_specs(l + 1, w_next.shape[1:], grid, 1)
        in_specs.append(cast_in)
        args.append(w_next)
        out_specs.append(cast_out)
        out_shape.append(jax.ShapeDtypeStruct((1,) + w_next.shape[1:], BF16))
    res = pl.pallas_call(
        functools.partial(_mlp_kernel, cast_next=cast_next),
        grid=grid,
        in_specs=in_specs,
        out_specs=out_specs,
        out_shape=out_shape,
        scratch_shapes=[pltpu.VMEM((tm, d), BF16), pltpu.VMEM((tm, d), F32)],
        compiler_params=_params(("arbitrary", "arbitrary")),
        name="mlp",
    )(*args)
    return (res[0], res[1]) if cast_next else (res[0], None)


def _block_diag_pack(w):
    depth, heads, dh, _ = w.shape
    per = max(1, min(heads, MXU_DIM // dh))
    n = heads // per
    wg = w.reshape(depth, n, per, dh, dh)
    eye = jnp.eye(per, dtype=w.dtype)
    out = jnp.einsum("lnhij,hg->lnhigj", wg, eye)
    return out.reshape(depth, n, per * dh, per * dh).astype(BF16)


def _run_layer(x, mem_k, mem_v, kv_off, h0, lb0, sb0, state_off, wts, l,
               *, n_streams, seq, width, heads, last, cast=None):
    if cast is not None:
        names = ("w_up", "w_down") + MERGE_WEIGHTS
        proj, *casts = _proj(x, wts, l, width=width, cast=tuple(cast[k] for k in names))
        wts = dict(wts, **{k: (w, 0) for k, w in zip(names, casts)})
    else:
        proj, = _proj(x, wts, l, width=width)
    zm = _attn(proj, mem_k, mem_v, kv_off, n_streams=n_streams, seq=seq, width=width, q_blk=5, heads=heads)
    if seq >= MERGE_TM and seq % MERGE_TM == 0:
        x, hn, lbn, sbn = _mixmerge(proj, zm, x, h0, lb0, sb0, state_off, wts, l,
                                    n_streams=n_streams, seq=seq, width=width)
    else:
        zl, zc, hn, lbn, sbn = _mix(proj, h0, lb0, sb0, state_off, wts, l,
                                    n_streams=n_streams, seq=seq, width=width)
        x = _merge(zl, zc, zm, proj, x, wts, l, width=width)
    w_next = cast["w_in"] if cast is not None and not last else None
    x, w_in_next = _mlp(x, wts, l, w_next=w_next)
    return x, hn, lbn, sbn, wts, w_in_next


def kernel(x_prompt, x_sample, mem_prompt, state_lru_h, state_lru_conv, state_sconv, cache_mem_k, cache_mem_v, g_mix_pre, w_in, b_gate, lru_conv_w, lru_conv_b, lru_w_a, lru_b_a, lru_w_i, lru_b_i, lru_lambda, w_branch_lru, sconv_w, w_branch_conv, g_mem, w_mem_kv, w_branch_mem, w_out, g_mix_post, g_mlp_pre, w_up, w_down, g_mlp_post):
    depth = w_in.shape[0]
    bp, sp, d = x_prompt.shape
    bs, ss, _ = x_sample.shape
    width = lru_lambda.shape[1]
    mlen, heads, dh = cache_mem_k.shape[2:]
    kl, ks = lru_conv_w.shape[1], sconv_w.shape[1]

    def rowvec(p):
        return p.reshape(depth, 1, -1)

    wts = dict(
        g_mix_pre=rowvec(g_mix_pre), w_in=(w_in[0:1].astype(BF16), 0), b_gate=rowvec(b_gate),
        lru_conv_w=lru_conv_w, lru_conv_b=rowvec(lru_conv_b),
        wa_bd=_block_diag_pack(lru_w_a), lru_b_a=rowvec(lru_b_a),
        wi_bd=_block_diag_pack(lru_w_i), lru_b_i=rowvec(lru_b_i),
        lru_lambda=rowvec(lru_lambda), sconv_w=sconv_w,
        g_mix_post=rowvec(g_mix_post), g_mlp_pre=rowvec(g_mlp_pre),
        g_mlp_post=rowvec(g_mlp_post),
        g_mem=rowvec(g_mem), w_mem_kv=w_mem_kv)

    hp = x_prompt.reshape(bp * sp, d)
    hs = x_sample.reshape(bs * ss, d)
    mem2d = mem_prompt.reshape(bp * mlen, d)
    p_h0 = jnp.zeros((bp, 1, width), F32)
    p_lb0 = jnp.zeros((bp, kl - 1, width), F32)
    p_sb0 = jnp.zeros((bp, ks - 1, width), F32)
    s_h0 = state_lru_h.reshape(depth * bs, 1, width)
    s_lb0 = state_lru_conv.reshape(depth * bs, kl - 1, width)
    s_sb0 = state_sconv.reshape(depth * bs, ks - 1, width)
    s_mk = cache_mem_k.reshape(depth * bs, mlen, heads, dh)
    s_mv = cache_mem_v.reshape(depth * bs, mlen, heads, dh)

    outs = {k: [] for k in ("p_h", "p_lb", "p_sb", "p_mk", "p_mv", "s_h", "s_lb", "s_sb")}
    f32_weights = dict(w_up=w_up, w_down=w_down, w_branch_lru=w_branch_lru, w_branch_conv=w_branch_conv,
                       w_branch_mem=w_branch_mem, w_out=w_out, w_in=w_in)
    for l in range(depth):
        last = l == depth - 1
        kv = _memkv(mem2d, wts, l, width=width)
        mk = kv[0].reshape(bp, mlen, width)
        mv = kv[1].reshape(bp, mlen, width)
        hp, hn, lbn, sbn, wts, w_in_next = _run_layer(
            hp, mk, mv, 0, p_h0, p_lb0, p_sb0, 0, wts, l,
            n_streams=bp, seq=sp, width=width, heads=heads, last=last, cast=f32_weights)
        outs["p_h"].append(hn.reshape(bp, width))
        outs["p_lb"].append(lbn)
        outs["p_sb"].append(sbn)
        outs["p_mk"].append(mk.reshape(bp, mlen, heads, dh))
        outs["p_mv"].append(mv.reshape(bp, mlen, heads, dh))
        hs, hn, lbn, sbn, _, _ = _run_layer(
            hs, s_mk, s_mv, l * bs, s_h0, s_lb0, s_sb0, l * bs, wts, l,
            n_streams=bs, seq=ss, width=width, heads=heads, last=last)
        outs["s_h"].append(hn.reshape(bs, width))
        outs["s_lb"].append(lbn)
        outs["s_sb"].append(sbn)
        if not last:
            wts = dict(wts, w_in=(w_in_next, 0))

    return (hp.reshape(bp, sp, d), hs.reshape(bs, ss, d),
            jnp.stack(outs["p_h"]), jnp.stack(outs["p_lb"]), jnp.stack(outs["p_sb"]),
            jnp.stack(outs["p_mk"]), jnp.stack(outs["p_mv"]),
            jnp.stack(outs["s_h"]), jnp.stack(outs["s_lb"]), jnp.stack(outs["s_sb"]))
```

```python
import functools

import jax
import jax.numpy as jnp
from jax import lax
from jax.experimental import pallas as pl
from jax.experimental.pallas import tpu as pltpu

F32 = jnp.float32
BF16 = jnp.bfloat16

EPS = 1e-6
LRU_C = 8.0
N_BRANCH = 3

SUBLANES = 8
LANES = 128
PACKED_ROWS = 16
MXU_DIM = 256
VMEM_LIMIT_BYTES = 56 * 1024 * 1024

PROJ_TM = 1024
MIX_ROWS = 512
ATTN_ROWS = 1024
ATTN_KV_BYTES = 4 * 1024 * 1024
MERGE_TM = 256
MERGE_TN = 512
MLP_TM, MLP_TF = 512, 1024
PAD_ROWS = SUBLANES
SCAN_ROWS = 32


def _tile(n, pref, mult=SUBLANES):
    if n <= pref:
        return n
    for t in range(pref, 0, -1):
        if n % t == 0 and t % mult == 0:
            return t
    raise ValueError(f"no tile for {n} <= {pref}")


def _params(sem):
    return pltpu.CompilerParams(dimension_semantics=sem, vmem_limit_bytes=VMEM_LIMIT_BYTES)


def _rms(x, g):
    var = jnp.mean(x * x, axis=-1, keepdims=True)
    return (x * lax.rsqrt(var + EPS)) * g


def _sigmoid(x):
    return 0.5 * jnp.tanh(0.5 * x) + 0.5


def _layer_spec(l, shape, pipeline_mode=None):
    nd = len(shape)
    kw = {} if pipeline_mode is None else {"pipeline_mode": pipeline_mode}
    return pl.BlockSpec((None,) + tuple(shape), lambda *_: (l,) + (0,) * nd, **kw)


def _split(n, parts_max, mult):
    for p in range(parts_max, 0, -1):
        if n % p == 0 and (n // p) % mult == 0:
            return p
    raise ValueError(f"cannot split {n} into <= {parts_max} parts of multiples of {mult}")


def _cast_specs(l, shape, grid, row_axis):
    r, c = shape
    col_axis = 1 - row_axis
    pr = _split(r, grid[row_axis], PACKED_ROWS)
    pc = _split(c, grid[col_axis], LANES)
    blk = (r // pr, c // pc)

    def idx(*g):
        return jnp.minimum(g[row_axis], pr - 1), jnp.minimum(g[col_axis], pc - 1)

    return (pl.BlockSpec((None,) + blk, lambda *g: (l,) + idx(*g)),
            pl.BlockSpec((None,) + blk, lambda *g: (0,) + idx(*g)))


def _proj_kernel(x_ref, g_ref, w_ref, bg_ref, *rest, gelu_blk, gate_blk0):
    n_cast = (len(rest) - 2) // 2
    cast_in, o_ref, cast_out, xn_ref = rest[:n_cast], rest[n_cast], rest[n_cast + 1:-1], rest[-1]
    j = pl.program_id(1)

    def side_casts():
        for src, dst in zip(cast_in, cast_out):
            dst[...] = src[...].astype(dst.dtype)

    @pl.when(j == 0)
    def _():
        side_casts()
        tm = x_ref.shape[0]
        for half in (slice(0, tm // 2), slice(tm // 2, tm)):
            xn_ref[half, :] = _rms(x_ref[half, :], g_ref[...]).astype(BF16)
            o_ref[half, :] = jnp.dot(xn_ref[half, :], w_ref[...], preferred_element_type=F32).astype(o_ref.dtype)

    def dot():
        side_casts()
        return jnp.dot(xn_ref[...], w_ref[...], preferred_element_type=F32)

    @pl.when(j == gelu_blk)
    def _():
        o_ref[...] = jax.nn.gelu(dot()).astype(o_ref.dtype)

    @pl.when(j >= gate_blk0)
    def _():
        o_ref[...] = _sigmoid(dot() + bg_ref[...]).astype(o_ref.dtype)

    @pl.when(jnp.logical_and(jnp.logical_and(j != 0, j != gelu_blk), j < gate_blk0))
    def _():
        o_ref[...] = dot().astype(o_ref.dtype)


def _proj(x, wts, l, *, width, cast=()):
    rows, d = x.shape
    w_in, w_l = wts["w_in"]
    cols = w_in.shape[2]
    tm, tn = _tile(rows, PROJ_TM, 2 * PACKED_ROWS), width
    gate_blk0 = (cols - N_BRANCH * d) // tn
    assert tm % (2 * PACKED_ROWS) == 0 and gate_blk0 > 1
    grid = (rows // tm, cols // tn)
    kern = functools.partial(_proj_kernel, gelu_blk=1, gate_blk0=gate_blk0)
    cast_specs = [_cast_specs(l, w.shape[1:], grid, 0) for w in cast]
    return pl.pallas_call(
        kern,
        grid=grid,
        in_specs=[
            pl.BlockSpec((tm, d), lambda i, j: (i, 0)),
            _layer_spec(l, (1, d)),
            pl.BlockSpec((None, d, tn), lambda i, j: (w_l, 0, j)),
            pl.BlockSpec((None, 1, tn), lambda i, j: (l, 0, jnp.maximum(j - gate_blk0, 0))),
        ] + [cs[0] for cs in cast_specs],
        out_specs=[pl.BlockSpec((tm, tn), lambda i, j: (i, j))] + [cs[1] for cs in cast_specs],
        out_shape=[jax.ShapeDtypeStruct((rows, cols), BF16)]
        + [jax.ShapeDtypeStruct((1,) + w.shape[1:], BF16) for w in cast],
        scratch_shapes=[pltpu.VMEM((tm, d), BF16)],
        compiler_params=_params(("arbitrary", "arbitrary")),
        name="proj",
    )(x, wts["g_mix_pre"], w_in, wts["b_gate"], *cast)


def _memkv_kernel(x_ref, g_ref, w_ref, o_ref, xn_ref):
    @pl.when(pl.program_id(1) == 0)
    def _():
        xn_ref[...] = _rms(x_ref[...], g_ref[...]).astype(BF16)

    o_ref[...] = jnp.dot(xn_ref[...], w_ref[...].astype(BF16), preferred_element_type=F32)


def _memkv(mem, wts, l, *, width):
    rows, d = mem.shape
    cols = wts["w_mem_kv"].shape[2]
    tm, tn = _tile(rows, PROJ_TM), width
    return pl.pallas_call(
        _memkv_kernel,
        grid=(rows // tm, cols // tn),
        in_specs=[
            pl.BlockSpec((tm, d), lambda i, j: (i, 0)),
            _layer_spec(l, (1, d)),
            pl.BlockSpec((None, d, tn), lambda i, j: (l, 0, j)),
        ],
        out_specs=pl.BlockSpec((None, tm, tn), lambda i, j: (j, i, 0)),
        out_shape=jax.ShapeDtypeStruct((cols // tn, rows, tn), F32),
        scratch_shapes=[pltpu.VMEM((tm, d), BF16)],
        compiler_params=_params(("parallel", "arbitrary")),
        name="memkv",
    )(mem, wts["g_mem"], wts["w_mem_kv"])


def _softplus(x):
    return jnp.maximum(x, 0.0) + jnp.log1p(jnp.exp(-jnp.abs(x)))


def _causal_conv(pad_ref, w_ref, rows):
    taps = w_ref.shape[0]
    xe = pad_ref[...]
    y = None
    for k in range(taps):
        back = taps - 1 - k
        xk = xe if back == 0 else pltpu.roll(xe, back, axis=0)
        term = xk[PAD_ROWS:PAD_ROWS + rows, :] * w_ref[k:k + 1, :]
        y = term if y is None else y + term
    return y


def _mix_init(h0_ref, lb0_ref, sb0_ref, xpad, cpad, hcar):
    width = xpad.shape[1]
    xpad[0:PAD_ROWS, :] = jnp.zeros((PAD_ROWS, width), F32)
    cpad[0:PAD_ROWS, :] = jnp.zeros((PAD_ROWS, width), F32)
    xpad[PAD_ROWS - lb0_ref.shape[1]:PAD_ROWS, :] = lb0_ref[0]
    cpad[PAD_ROWS - sb0_ref.shape[1]:PAD_ROWS, :] = sb0_ref[0]
    hcar[...] = h0_ref[0]


def _mix_state_out(hn_ref, lbn_ref, sbn_ref, xpad, cpad, hcar):
    hn_ref[0] = hcar[...]
    lbn_ref[0] = xpad[PAD_ROWS - lbn_ref.shape[1]:PAD_ROWS, :]
    sbn_ref[0] = cpad[PAD_ROWS - sbn_ref.shape[1]:PAD_ROWS, :]


def _mix_lru_conv(lx_ref, cw_ref, cb_ref, xpad, uscr):
    rows = lx_ref.shape[0]
    kl = cw_ref.shape[0]
    xpad[PAD_ROWS:PAD_ROWS + rows, :] = lx_ref[...].astype(F32)
    uscr[...] = _causal_conv(xpad, cw_ref, rows) + cb_ref[...]
    xpad[PAD_ROWS - (kl - 1):PAD_ROWS, :] = xpad[PAD_ROWS + rows - (kl - 1):PAD_ROWS + rows, :]


def _mix_gate_dots(uscr, wa_ref, ba_ref, wi_ref, bi_ref, acum, bcum):
    n_chunks = wa_ref.shape[0]
    bd = uscr.shape[1] // n_chunks
    for c in range(n_chunks):
        sl = slice(c * bd, (c + 1) * bd)
        ub = uscr[:, sl].astype(BF16)
        acum[:, sl] = jnp.dot(ub, wa_ref[c], preferred_element_type=F32) + ba_ref[:, sl]
        bcum[:, sl] = jnp.dot(ub, wi_ref[c], preferred_element_type=F32) + bi_ref[:, sl]


def _mix_sconv(sb_ref, sc_ref, sh_ref, scw_ref, zc_ref, cpad):
    rows = sc_ref.shape[0]
    ks = scw_ref.shape[0]
    cpad[PAD_ROWS:PAD_ROWS + rows, :] = sc_ref[...].astype(F32) * sh_ref[...].astype(F32)
    cv = _causal_conv(cpad, scw_ref, rows)
    zc_ref[...] = (sb_ref[...].astype(F32) * cv).astype(zc_ref.dtype)
    cpad[PAD_ROWS - (ks - 1):PAD_ROWS, :] = cpad[PAD_ROWS + rows - (ks - 1):PAD_ROWS + rows, :]


def _mix_scan_chunk(c, n_chunks, lam_ref, uscr, ya_s, yi_s, acum, bcum):
    rows, width = uscr.shape
    bd = width // n_chunks
    sl = slice(c * bd, (c + 1) * bd)
    half_c = (-0.5 * LRU_C) * _softplus(-lam_ref[:, sl])
    tr = _tile(rows, SCAN_ROWS)
    groups = tr // SUBLANES
    row = lax.broadcasted_iota(jnp.int32, (groups, SUBLANES, bd), 1)
    for r0 in range(0, rows, tr):
        rs = slice(r0, r0 + tr)
        ta = jnp.tanh(0.5 * ya_s[rs, sl])
        ti = jnp.tanh(0.5 * yi_s[rs, sl])
        log_a = half_c * ta + half_c
        a = jnp.exp(log_a)
        th = jnp.tanh(log_a)
        u = -0.5 * th
        v = 1.0 - th
        half_mult = jnp.where(u > 0.0, u * lax.rsqrt(u * v), 0.0)
        b = half_mult * ((ti + 1.0) * uscr[rs, sl])
        a3 = a.reshape(groups, SUBLANES, bd)
        b3 = b.reshape(groups, SUBLANES, bd)
        d = 1
        while d < SUBLANES:
            a_prev = pltpu.roll(a3, d, axis=1)
            b_prev = pltpu.roll(b3, d, axis=1)
            take = row >= d
            b3 = jnp.where(take, a3 * b_prev + b3, b3)
            a3 = jnp.where(take, a3 * a_prev, a3)
            d *= 2
        acum[rs, sl] = a3.reshape(tr, bd)
        bcum[rs, sl] = b3.reshape(tr, bd)


def _mix_carry(lo, hi, gg_ref, zl_ref, acum, bcum, hcar, *, unroll):
    def body(g, h):
        r0 = pl.multiple_of(g * PACKED_ROWS, PACKED_ROWS)
        r1 = pl.multiple_of(r0 + SUBLANES, SUBLANES)
        h1 = acum[pl.ds(r0, SUBLANES), :] * h + bcum[pl.ds(r0, SUBLANES), :]
        h2 = acum[pl.ds(r1, SUBLANES), :] * h1[SUBLANES - 1:SUBLANES, :] + bcum[pl.ds(r1, SUBLANES), :]
        hh = jnp.concatenate([h1, h2], axis=0)
        gate = gg_ref[pl.ds(r0, PACKED_ROWS), :].astype(F32)
        zl_ref[pl.ds(r0, PACKED_ROWS), :] = (hh * gate).astype(zl_ref.dtype)
        return h2[SUBLANES - 1:SUBLANES, :]

    hcar[...] = lax.fori_loop(lo, hi, body, hcar[...], unroll=unroll)


def _mix_kernel(lx_ref, gg_ref, sb_ref, sc_ref, sh_ref, h0_ref, lb0_ref, sb0_ref,
                cw_ref, cb_ref, wa_ref, ba_ref, wi_ref, bi_ref, lam_ref, scw_ref,
                zl_ref, zc_ref, hn_ref, lbn_ref, sbn_ref,
                xpad, cpad, acum, bcum, hcar, uscr):
    @pl.when(pl.program_id(1) == 0)
    def _():
        _mix_init(h0_ref, lb0_ref, sb0_ref, xpad, cpad, hcar)

    _mix_lru_conv(lx_ref, cw_ref, cb_ref, xpad, uscr)
    _mix_gate_dots(uscr, wa_ref, ba_ref, wi_ref, bi_ref, acum, bcum)
    _mix_sconv(sb_ref, sc_ref, sh_ref, scw_ref, zc_ref, cpad)
    n_chunks = wa_ref.shape[0]
    for c in range(n_chunks):
        _mix_scan_chunk(c, n_chunks, lam_ref, uscr, acum, bcum, acum, bcum)
    _mix_carry(0, lx_ref.shape[0] // PACKED_ROWS, gg_ref, zl_ref, acum, bcum, hcar, unroll=False)
    _mix_state_out(hn_ref, lbn_ref, sbn_ref, xpad, cpad, hcar)


def _mix_scratch(rows, width):
    return [pltpu.VMEM((PAD_ROWS + rows, width), F32),
            pltpu.VMEM((PAD_ROWS + rows, width), F32),
            pltpu.VMEM((rows, width), F32),
            pltpu.VMEM((rows, width), F32),
            pltpu.VMEM((1, width), F32),
            pltpu.VMEM((rows, width), F32)]


def _mix_param_specs(wts, l, width):
    kl, ks = wts["lru_conv_w"].shape[1], wts["sconv_w"].shape[1]
    nck = wts["wa_bd"].shape[1]
    bd_spec = _layer_spec(l, (nck, width // nck, width // nck))
    return [_layer_spec(l, (kl, width)), _layer_spec(l, (1, width)),
            bd_spec, _layer_spec(l, (1, width)), bd_spec, _layer_spec(l, (1, width)),
            _layer_spec(l, (1, width)), _layer_spec(l, (ks, width))]


def _mix_param_args(wts):
    return (wts["lru_conv_w"], wts["lru_conv_b"], wts["wa_bd"], wts["lru_b_a"], wts["wi_bd"], wts["lru_b_i"],
            wts["lru_lambda"], wts["sconv_w"])


def _mix(proj, h0, lb0, sb0, state_off, wts, l, *, n_streams, seq, width):
    rows = _tile(seq, MIX_ROWS, PACKED_ROWS)
    nt = seq // rows
    kl, ks = wts["lru_conv_w"].shape[1], wts["sconv_w"].shape[1]

    def col(c):
        return pl.BlockSpec((rows, width), lambda s, t: (s * nt + t, c))

    def state_in(k):
        return pl.BlockSpec((1, k, width), lambda s, t: (state_off + s, 0, 0))

    def state_out(k):
        return pl.BlockSpec((1, k, width), lambda s, t: (s, 0, 0))

    total = n_streams * seq
    return pl.pallas_call(
        _mix_kernel,
        grid=(n_streams, nt),
        in_specs=[col(0), col(1), col(2), col(3), col(4),
                  state_in(1), state_in(kl - 1), state_in(ks - 1)] + _mix_param_specs(wts, l, width),
        out_specs=[pl.BlockSpec((rows, width), lambda s, t: (s * nt + t, 0)),
                   pl.BlockSpec((rows, width), lambda s, t: (s * nt + t, 0)),
                   state_out(1), state_out(kl - 1), state_out(ks - 1)],
        out_shape=[jax.ShapeDtypeStruct((total, width), BF16),
                   jax.ShapeDtypeStruct((total, width), BF16),
                   jax.ShapeDtypeStruct((n_streams, 1, width), F32),
                   jax.ShapeDtypeStruct((n_streams, kl - 1, width), F32),
                   jax.ShapeDtypeStruct((n_streams, ks - 1, width), F32)],
        scratch_shapes=_mix_scratch(rows, width),
        compiler_params=_params(("parallel", "arbitrary")),
        name="mix",
    )(proj, proj, proj, proj, proj, h0, lb0, sb0, *_mix_param_args(wts))


def _attn_kernel(q_ref, k_ref, v_ref, o_ref, *, heads, seq_rows):
    dh = q_ref.shape[1] // heads
    scale = dh ** -0.5
    if len(k_ref.shape) == 4:
        mh = k_ref.shape[1] * heads
        shape = (heads * seq_rows, mh)
        row = lax.broadcasted_iota(jnp.int32, shape, 0)
        row_head = sum((row >= h * seq_rows).astype(jnp.int32) for h in range(1, heads))
        own = row_head == lax.broadcasted_iota(jnp.int32, shape, 1) % heads
        for g in range(k_ref.shape[0]):
            rs = slice(g * seq_rows, (g + 1) * seq_rows)
            kf = k_ref[g].reshape(mh, dh).astype(BF16)
            vf = v_ref[g].reshape(mh, dh).astype(BF16)
            qs = jnp.concatenate([q_ref[rs, h * dh:(h + 1) * dh] for h in range(heads)], axis=0)
            s = lax.dot_general(qs, kf, (((1,), (1,)), ((), ())), preferred_element_type=F32) * scale
            s = jnp.where(own, s, -jnp.inf)
            e = jnp.where(own, jnp.exp(s - jnp.max(s, axis=-1, keepdims=True)), 0.0)
            p = e * (1.0 / jnp.sum(e, axis=-1, keepdims=True))
            o = jnp.dot(p.astype(BF16), vf, preferred_element_type=F32).astype(o_ref.dtype)
            for h in range(heads):
                o_ref[rs, h * dh:(h + 1) * dh] = o[h * seq_rows:(h + 1) * seq_rows, :]
        return
    for g in range(k_ref.shape[0]):
        rs = slice(g * seq_rows, (g + 1) * seq_rows)
        for h in range(heads):
            sl = slice(h * dh, (h + 1) * dh)
            kh = k_ref[g, :, sl].astype(BF16)
            vh = v_ref[g, :, sl].astype(BF16)
            s = lax.dot_general(q_ref[rs, sl], kh, (((1,), (1,)), ((), ())),
                                preferred_element_type=F32) * scale
            e = jnp.exp(s - jnp.max(s, axis=-1, keepdims=True))
            p = e * (1.0 / jnp.sum(e, axis=-1, keepdims=True))
            o_ref[rs, sl] = jnp.dot(p.astype(BF16), vh, preferred_element_type=F32).astype(o_ref.dtype)


def _attn(proj, mem_k, mem_v, kv_off, *, n_streams, seq, width, q_blk, heads):
    mlen = mem_k.shape[1]
    rows = _tile(seq, ATTN_ROWS)
    nt = seq // rows
    per_step = 1
    if nt == 1:
        per_step = _tile(n_streams, max(1, ATTN_KV_BYTES // (mlen * width * 4)), 1)
    assert kv_off % per_step == 0
    blk = per_step * rows
    kv_shape = (per_step,) + mem_k.shape[1:]
    kv_spec = pl.BlockSpec(kv_shape, lambda s, t: (kv_off // per_step + s,) + (0,) * (len(kv_shape) - 1))
    return pl.pallas_call(
        functools.partial(_attn_kernel, heads=heads, seq_rows=rows),
        grid=(n_streams // per_step, nt),
        in_specs=[pl.BlockSpec((blk, width), lambda s, t: (s * nt + t, q_blk)), kv_spec, kv_spec],
        out_specs=pl.BlockSpec((blk, width), lambda s, t: (s * nt + t, 0)),
        out_shape=jax.ShapeDtypeStruct((n_streams * seq, width), BF16),
        compiler_params=_params(("parallel", "arbitrary")),
        name="attn",
    )(proj, mem_k, mem_v)


def _merge_m_tile(cs, zl_ref, zc_ref, zm_ref, g0_ref, g1_ref, g2_ref, wl_ref, wc_ref, wm_ref):
    m = g0_ref[:, cs].astype(F32) * jnp.dot(zl_ref[...], wl_ref[:, cs], preferred_element_type=F32)
    m = m + g1_ref[:, cs].astype(F32) * jnp.dot(zc_ref[...], wc_ref[:, cs], preferred_element_type=F32)
    m = m + g2_ref[:, cs].astype(F32) * jnp.dot(zm_ref[...], wm_ref[:, cs], preferred_element_type=F32)
    return m.astype(BF16)


def _merge_kernel(zl_ref, zc_ref, zm_ref, g0_ref, g1_ref, g2_ref, x_ref,
                  wl_ref, wc_ref, wm_ref, wo_ref, gp_ref, o_ref):
    m = _merge_m_tile(slice(None), zl_ref, zc_ref, zm_ref, g0_ref, g1_ref, g2_ref, wl_ref, wc_ref, wm_ref)
    y = jnp.dot(m, wo_ref[...], preferred_element_type=F32)
    o_ref[...] = x_ref[...] + _rms(y, gp_ref[...])


MERGE_WEIGHTS = ("w_branch_lru", "w_branch_conv", "w_branch_mem", "w_out")


def _merge_weight_specs(wts, l, d):
    once = pl.Buffered(1)
    specs = [_layer_spec(wts[k][1], wts[k][0].shape[1:], once) for k in MERGE_WEIGHTS]
    return specs + [_layer_spec(l, (1, d), once)]


def _merge_weight_args(wts):
    return tuple(wts[k][0] for k in MERGE_WEIGHTS) + (wts["g_mix_post"],)


def _merge(zl, zc, zm, proj, x, wts, l, *, width):
    rows, d = x.shape
    tm = _tile(rows, MERGE_TM)
    gate_blk0 = (proj.shape[1] - N_BRANCH * d) // d

    def act(w):
        return pl.BlockSpec((tm, w), lambda i: (i, 0))

    def gate(b):
        return pl.BlockSpec((tm, d), lambda i: (i, gate_blk0 + b))

    return pl.pallas_call(
        _merge_kernel,
        grid=(rows // tm,),
        in_specs=[act(width), act(width), act(width), gate(0), gate(1), gate(2), act(d)]
        + _merge_weight_specs(wts, l, d),
        out_specs=act(d),
        out_shape=jax.ShapeDtypeStruct((rows, d), F32),
        compiler_params=_params(("parallel",)),
        name="merge",
    )(zl, zc, zm, proj, proj, proj, x, *_merge_weight_args(wts))


def _mixmerge_kernel(lx_ref, gg_ref, sb_ref, sc_ref, sh_ref, h0_ref, lb0_ref, sb0_ref,
                     cw_ref, cb_ref, wa_ref, ba_ref, wi_ref, bi_ref, lam_ref, scw_ref,
                     zm_ref, g0_ref, g1_ref, g2_ref, x_ref, wl_ref, wc_ref, wm_ref, wo_ref, gp_ref,
                     o_ref, hn_ref, lbn_ref, sbn_ref,
                     xpad, cpad, acum, bcum, hcar, uscr, ya_s, yi_s, zl_s, zc_s, m_f, m_s, y_s,
                     *, blocks_per_stream, n_blocks):
    g = pl.program_id(0)
    kl = cw_ref.shape[0]

    @pl.when(g == 0)
    def _():
        for ref in (cpad, hcar, uscr, ya_s, yi_s):
            ref[...] = jnp.zeros(ref.shape, ref.dtype)

    @pl.when(jnp.logical_and(g % blocks_per_stream == 0, g < n_blocks))
    def _():
        xpad[0:PAD_ROWS, :] = jnp.zeros((PAD_ROWS, xpad.shape[1]), F32)
        xpad[PAD_ROWS - (kl - 1):PAD_ROWS, :] = lb0_ref[0]

    scan_valid = g >= 1

    @pl.when(jnp.logical_and((g - 1) % blocks_per_stream == 0, scan_valid))
    def _():
        cpad[0:PAD_ROWS, :] = jnp.zeros((PAD_ROWS, cpad.shape[1]), F32)
        cpad[PAD_ROWS - sb0_ref.shape[1]:PAD_ROWS, :] = sb0_ref[0]
        hcar[...] = h0_ref[0]

    rows, d = x_ref.shape
    n_chunks = wa_ref.shape[0]
    col_tiles = [slice(j, j + MERGE_TN) for j in range(0, d, MERGE_TN)]

    def gated(gate_ref, z_ref, w_ref, cs):
        return gate_ref[:, cs].astype(F32) * jnp.dot(z_ref[...], w_ref[:, cs], preferred_element_type=F32)

    tiles = rows // PACKED_ROWS
    vector_pieces = [functools.partial(_mix_scan_chunk, c, n_chunks, lam_ref, uscr, ya_s, yi_s, acum, bcum)
                     for c in range(n_chunks)]
    vector_pieces += [functools.partial(_mix_carry, k * tiles // 2, (k + 1) * tiles // 2,
                                        gg_ref, zl_s, acum, bcum, hcar, unroll=True) for k in range(2)]
    _mix_sconv(sb_ref, sc_ref, sh_ref, scw_ref, zc_s, cpad)
    for cs in col_tiles:
        m_f[:, cs] = gated(g2_ref, zm_ref, wm_ref, cs)
        if vector_pieces:
            vector_pieces.pop(0)()
    for cs in col_tiles:
        m_f[:, cs] += gated(g1_ref, zc_s, wc_ref, cs)
        if vector_pieces:
            vector_pieces.pop(0)()
    while vector_pieces:
        vector_pieces.pop(0)()
    for cs in col_tiles:
        m_s[:, cs] = (m_f[:, cs] + gated(g0_ref, zl_s, wl_ref, cs)).astype(BF16)
    for k, cs in enumerate(col_tiles):
        y_s[:, cs] = jnp.dot(m_s[...], wo_ref[:, cs], preferred_element_type=F32)
        if k == 0:
            _mix_lru_conv(lx_ref, cw_ref, cb_ref, xpad, uscr)
    _mix_gate_dots(uscr, wa_ref, ba_ref, wi_ref, bi_ref, ya_s, yi_s)
    o_ref[...] = x_ref[...] + _rms(y_s[...], gp_ref[...])

    @pl.when(g < n_blocks)
    def _():
        lbn_ref[0] = xpad[PAD_ROWS - (kl - 1):PAD_ROWS, :]

    @pl.when(scan_valid)
    def _():
        hn_ref[0] = hcar[...]
        sbn_ref[0] = cpad[PAD_ROWS - sbn_ref.shape[1]:PAD_ROWS, :]


def _mixmerge(proj, zm, x, h0, lb0, sb0, state_off, wts, l, *, n_streams, seq, width):
    rows, d = x.shape
    tm = _tile(seq, MERGE_TM, PACKED_ROWS)
    nt = seq // tm
    n = n_streams * nt
    kl, ks = wts["lru_conv_w"].shape[1], wts["sconv_w"].shape[1]
    gate_blk0 = (proj.shape[1] - N_BRANCH * d) // d

    def blk(lag):
        return lambda g: jnp.clip(g - lag, 0, n - 1)

    conv_blk, scan_blk, merge_blk = blk(0), blk(1), blk(1)

    def col(c, which):
        return pl.BlockSpec((tm, width), lambda g: (which(g), c))

    def state_in(k, which):
        return pl.BlockSpec((1, k, width), lambda g: (state_off + which(g) // nt, 0, 0))

    def state_out(k, which):
        return pl.BlockSpec((1, k, width), lambda g: (which(g) // nt, 0, 0))

    def gate(b):
        return pl.BlockSpec((tm, d), lambda g: (merge_blk(g), gate_blk0 + b))

    def act(w):
        return pl.BlockSpec((tm, w), lambda g: (merge_blk(g), 0))

    kern = functools.partial(_mixmerge_kernel, blocks_per_stream=nt, n_blocks=n)
    act_bf16 = pltpu.VMEM((tm, width), BF16)
    act_f32 = pltpu.VMEM((tm, width), F32)
    return pl.pallas_call(
        kern,
        grid=(n + 1,),
        in_specs=[col(0, conv_blk), col(1, scan_blk), col(2, scan_blk), col(3, scan_blk), col(4, scan_blk),
                  state_in(1, scan_blk), state_in(kl - 1, conv_blk), state_in(ks - 1, scan_blk)]
        + _mix_param_specs(wts, l, width)
        + [act(width), gate(0), gate(1), gate(2), act(d)] + _merge_weight_specs(wts, l, d),
        out_specs=[act(d), state_out(1, scan_blk), state_out(kl - 1, conv_blk), state_out(ks - 1, scan_blk)],
        out_shape=[jax.ShapeDtypeStruct((rows, d), F32),
                   jax.ShapeDtypeStruct((n_streams, 1, width), F32),
                   jax.ShapeDtypeStruct((n_streams, kl - 1, width), F32),
                   jax.ShapeDtypeStruct((n_streams, ks - 1, width), F32)],
        scratch_shapes=_mix_scratch(tm, width) + [act_f32, act_f32, act_bf16, act_bf16, pltpu.VMEM((tm, d), F32),
                                                  pltpu.VMEM((tm, d), BF16), pltpu.VMEM((tm, d), F32)],
        compiler_params=_params(("arbitrary",)),
        name="mixmerge",
    )(proj, proj, proj, proj, proj, h0, lb0, sb0, *_mix_param_args(wts),
      zm, proj, proj, proj, x, *_merge_weight_args(wts))


def _mlp_kernel(x_ref, gpre_ref, wu_ref, wd_ref, gpost_ref, *rest, cast_next):
    if cast_next:
        wnext_ref, o_ref, wnext_out_ref, xn_ref, acc_ref = rest
    else:
        o_ref, xn_ref, acc_ref = rest
    f = pl.program_id(1)
    last = pl.num_programs(1) - 1
    tm = x_ref.shape[0]

    def partial_down(rows):
        hid = jnp.dot(xn_ref[rows, :], wu_ref[...], preferred_element_type=F32)
        hid = jnp.square(jnp.maximum(hid, 0.0)).astype(BF16)
        return jnp.dot(hid, wd_ref[...], preferred_element_type=F32)

    def side_cast():
        if cast_next:
            wnext_out_ref[...] = wnext_ref[...].astype(wnext_out_ref.dtype)

    @pl.when(f == 0)
    def _():
        side_cast()
        xn_ref[...] = _rms(x_ref[...], gpre_ref[...]).astype(BF16)
        acc_ref[...] = partial_down(slice(None))

    @pl.when(jnp.logical_and(f > 0, f < last))
    def _():
        side_cast()
        acc_ref[...] += partial_down(slice(None))

    @pl.when(f == last)
    def _():
        side_cast()
        for half in (slice(0, tm // 2), slice(tm // 2, tm)):
            y = acc_ref[half, :] + partial_down(half)
            o_ref[half, :] = x_ref[half, :] + _rms(y, gpost_ref[...])


def _mlp(x, wts, l, *, w_next=None):
    rows, d = x.shape
    (w_up, up_l), (w_down, down_l) = wts["w_up"], wts["w_down"]
    dff = w_up.shape[2]
    tm, tf = _tile(rows, MLP_TM, 2 * SUBLANES), _tile(dff, MLP_TF, LANES)
    assert dff // tf >= 2
    grid = (rows // tm, dff // tf)
    row_blk = pl.BlockSpec((tm, d), lambda i, f: (i, 0))
    in_specs = [row_blk,
                _layer_spec(l, (1, d)),
                pl.BlockSpec((None, d, tf), lambda i, f: (up_l, 0, f)),
                pl.BlockSpec((None, tf, d), lambda i, f: (down_l, f, 0)),
                _layer_spec(l, (1, d))]
    args = [x, wts["g_mlp_pre"], w_up, w_down, wts["g_mlp_post"]]
    out_specs, out_shape = [row_blk], [jax.ShapeDtypeStruct((rows, d), F32)]
    cast_next = w_next is not None
    if cast_next:
        cast_in, cast_out = _cast_specs(l + 1, w_next.shape[1:], grid, 1)
        in_specs.append(cast_in)
        args.append(w_next)
        out_specs.append(cast_out)
        out_shape.append(jax.ShapeDtypeStruct((1,) + w_next.shape[1:], BF16))
    res = pl.pallas_call(
        functools.partial(_mlp_kernel, cast_next=cast_next),
        grid=grid,
        in_specs=in_specs,
        out_specs=out_specs,
        out_shape=out_shape,
        scratch_shapes=[pltpu.VMEM((tm, d), BF16), pltpu.VMEM((tm, d), F32)],
        compiler_params=_params(("arbitrary", "arbitrary")),
        name="mlp",
    )(*args)
    return (res[0], res[1]) if cast_next else (res[0], None)


def _block_diag_pack(w):
    depth, heads, dh, _ = w.shape
    per = max(1, min(heads, MXU_DIM // dh))
    n = heads // per
    wg = w.reshape(depth, n, per, dh, dh)
    eye = jnp.eye(per, dtype=w.dtype)
    out = jnp.einsum("lnhij,hg->lnhigj", wg, eye)
    return out.reshape(depth, n, per * dh, per * dh).astype(BF16)


def _run_layer(x, mem_k, mem_v, kv_off, h0, lb0, sb0, state_off, wts, l,
               *, n_streams, seq, width, heads, last, cast=None):
    if cast is not None:
        names = ("w_up", "w_down") + MERGE_WEIGHTS
        proj, *casts = _proj(x, wts, l, width=width, cast=tuple(cast[k] for k in names))
        wts = dict(wts, **{k: (w, 0) for k, w in zip(names, casts)})
    else:
        proj, = _proj(x, wts, l, width=width)
    zm = _attn(proj, mem_k, mem_v, kv_off, n_streams=n_streams, seq=seq, width=width, q_blk=5, heads=heads)
    if seq >= MERGE_TM and seq % MERGE_TM == 0:
        x, hn, lbn, sbn = _mixmerge(proj, zm, x, h0, lb0, sb0, state_off, wts, l,
                                    n_streams=n_streams, seq=seq, width=width)
    else:
        zl, zc, hn, lbn, sbn = _mix(proj, h0, lb0, sb0, state_off, wts, l,
                                    n_streams=n_streams, seq=seq, width=width)
        x = _merge(zl, zc, zm, proj, x, wts, l, width=width)
    w_next = cast["w_in"] if cast is not None and not last else None
    x, w_in_next = _mlp(x, wts, l, w_next=w_next)
    return x, hn, lbn, sbn, wts, w_in_next


def kernel(x_prompt, x_sample, mem_prompt, state_lru_h, state_lru_conv, state_sconv, cache_mem_k, cache_mem_v, g_mix_pre, w_in, b_gate, lru_conv_w, lru_conv_b, lru_w_a, lru_b_a, lru_w_i, lru_b_i, lru_lambda, w_branch_lru, sconv_w, w_branch_conv, g_mem, w_mem_kv, w_branch_mem, w_out, g_mix_post, g_mlp_pre, w_up, w_down, g_mlp_post):
    depth = w_in.shape[0]
    bp, sp, d = x_prompt.shape
    bs, ss, _ = x_sample.shape
    width = lru_lambda.shape[1]
    mlen, heads, dh = cache_mem_k.shape[2:]
    kl, ks = lru_conv_w.shape[1], sconv_w.shape[1]

    def rowvec(p):
        return p.reshape(depth, 1, -1)

    wts = dict(
        g_mix_pre=rowvec(g_mix_pre), w_in=(w_in[0:1].astype(BF16), 0), b_gate=rowvec(b_gate),
        lru_conv_w=lru_conv_w, lru_conv_b=rowvec(lru_conv_b),
        wa_bd=_block_diag_pack(lru_w_a), lru_b_a=rowvec(lru_b_a),
        wi_bd=_block_diag_pack(lru_w_i), lru_b_i=rowvec(lru_b_i),
        lru_lambda=rowvec(lru_lambda), sconv_w=sconv_w,
        g_mix_post=rowvec(g_mix_post), g_mlp_pre=rowvec(g_mlp_pre),
        g_mlp_post=rowvec(g_mlp_post),
        g_mem=rowvec(g_mem), w_mem_kv=w_mem_kv)

    hp = x_prompt.reshape(bp * sp, d)
    hs = x_sample.reshape(bs * ss, d)
    mem2d = mem_prompt.reshape(bp * mlen, d)
    p_h0 = jnp.zeros((bp, 1, width), F32)
    p_lb0 = jnp.zeros((bp, kl - 1, width), F32)
    p_sb0 = jnp.zeros((bp, ks - 1, width), F32)
    s_h0 = state_lru_h.reshape(depth * bs, 1, width)
    s_lb0 = state_lru_conv.reshape(depth * bs, kl - 1, width)
    s_sb0 = state_sconv.reshape(depth * bs, ks - 1, width)
    s_mk = cache_mem_k.reshape(depth * bs, mlen, heads, dh)
    s_mv = cache_mem_v.reshape(depth * bs, mlen, heads, dh)

    outs = {k: [] for k in ("p_h", "p_lb", "p_sb", "p_mk", "p_mv", "s_h", "s_lb", "s_sb")}
    f32_weights = dict(w_up=w_up, w_down=w_down, w_branch_lru=w_branch_lru, w_branch_conv=w_branch_conv,
                       w_branch_mem=w_branch_mem, w_out=w_out, w_in=w_in)
    for l in range(depth):
        last = l == depth - 1
        kv = _memkv(mem2d, wts, l, width=width)
        mk = kv[0].reshape(bp, mlen, width)
        mv = kv[1].reshape(bp, mlen, width)
        hp, hn, lbn, sbn, wts, w_in_next = _run_layer(
            hp, mk, mv, 0, p_h0, p_lb0, p_sb0, 0, wts, l,
            n_streams=bp, seq=sp, width=width, heads=heads, last=last, cast=f32_weights)
        outs["p_h"].append(hn.reshape(bp, width))
        outs["p_lb"].append(lbn)
        outs["p_sb"].append(sbn)
        outs["p_mk"].append(mk.reshape(bp, mlen, heads, dh))
        outs["p_mv"].append(mv.reshape(bp, mlen, heads, dh))
        hs, hn, lbn, sbn, _, _ = _run_layer(
            hs, s_mk, s_mv, l * bs, s_h0, s_lb0, s_sb0, l * bs, wts, l,
            n_streams=bs, seq=ss, width=width, heads=heads, last=last)
        outs["s_h"].append(hn.reshape(bs, width))
        outs["s_lb"].append(lbn)
        outs["s_sb"].append(sbn)
        if not last:
            wts = dict(wts, w_in=(w_in_next, 0))

    return (hp.reshape(bp, sp, d), hs.reshape(bs, ss, d),
            jnp.stack(outs["p_h"]), jnp.stack(outs["p_lb"]), jnp.stack(outs["p_sb"]),
            jnp.stack(outs["p_mk"]), jnp.stack(outs["p_mv"]),
            jnp.stack(outs["s_h"]), jnp.stack(outs["s_lb"]), jnp.stack(outs["s_sb"]))
```

```python
import functools

import jax
import jax.numpy as jnp
from jax import lax
from jax.experimental import pallas as pl
from jax.experimental.pallas import tpu as pltpu

F32 = jnp.float32
BF16 = jnp.bfloat16

EPS = 1e-6
LRU_C = 8.0
N_BRANCH = 3

SUBLANES = 8
LANES = 128
PACKED_ROWS = 16
MXU_DIM = 256
VMEM_LIMIT_BYTES = 56 * 1024 * 1024

PROJ_TM = 1024
MIX_ROWS = 512
ATTN_ROWS = 1024
ATTN_KV_BYTES = 4 * 1024 * 1024
MERGE_TM = 256
MERGE_TN = 512
MLP_TM, MLP_TF = 512, 1024
PAD_ROWS = SUBLANES
SCAN_ROWS = 32


def _tile(n, pref, mult=SUBLANES):
    if n <= pref:
        return n
    for t in range(pref, 0, -1):
        if n % t == 0 and t % mult == 0:
            return t
    raise ValueError(f"no tile for {n} <= {pref}")


def _params(sem):
    return pltpu.CompilerParams(dimension_semantics=sem, vmem_limit_bytes=VMEM_LIMIT_BYTES)


def _rms(x, g):
    var = jnp.mean(x * x, axis=-1, keepdims=True)
    return (x * lax.rsqrt(var + EPS)) * g


def _sigmoid(x):
    return 0.5 * jnp.tanh(0.5 * x) + 0.5


def _layer_spec(l, shape, pipeline_mode=None):
    nd = len(shape)
    kw = {} if pipeline_mode is None else {"pipeline_mode": pipeline_mode}
    return pl.BlockSpec((None,) + tuple(shape), lambda *_: (l,) + (0,) * nd, **kw)


def _split(n, parts_max, mult):
    for p in range(parts_max, 0, -1):
        if n % p == 0 and (n // p) % mult == 0:
            return p
    raise ValueError(f"cannot split {n} into <= {parts_max} parts of multiples of {mult}")


def _cast_specs(l, shape, grid, row_axis):
    r, c = shape
    col_axis = 1 - row_axis
    pr = _split(r, grid[row_axis], PACKED_ROWS)
    pc = _split(c, grid[col_axis], LANES)
    blk = (r // pr, c // pc)

    def idx(*g):
        return jnp.minimum(g[row_axis], pr - 1), jnp.minimum(g[col_axis], pc - 1)

    return (pl.BlockSpec((None,) + blk, lambda *g: (l,) + idx(*g)),
            pl.BlockSpec((None,) + blk, lambda *g: (0,) + idx(*g)))


def _proj_kernel(x_ref, g_ref, w_ref, bg_ref, *rest, gelu_blk, gate_blk0):
    n_cast = (len(rest) - 2) // 2
    cast_in, o_ref, cast_out, xn_ref = rest[:n_cast], rest[n_cast], rest[n_cast + 1:-1], rest[-1]
    j = pl.program_id(1)

    def side_casts():
        for src, dst in zip(cast_in, cast_out):
            dst[...] = src[...].astype(dst.dtype)

    @pl.when(j == 0)
    def _():
        side_casts()
        tm = x_ref.shape[0]
        for half in (slice(0, tm // 2), slice(tm // 2, tm)):
            xn_ref[half, :] = _rms(x_ref[half, :], g_ref[...]).astype(BF16)
            o_ref[half, :] = jnp.dot(xn_ref[half, :], w_ref[...], preferred_element_type=F32).astype(o_ref.dtype)

    def dot():
        side_casts()
        return jnp.dot(xn_ref[...], w_ref[...], preferred_element_type=F32)

    @pl.when(j == gelu_blk)
    def _():
        o_ref[...] = jax.nn.gelu(dot()).astype(o_ref.dtype)

    @pl.when(j >= gate_blk0)
    def _():
        o_ref[...] = _sigmoid(dot() + bg_ref[...]).astype(o_ref.dtype)

    @pl.when(jnp.logical_and(jnp.logical_and(j != 0, j != gelu_blk), j < gate_blk0))
    def _():
        o_ref[...] = dot().astype(o_ref.dtype)


def _proj(x, wts, l, *, width, cast=()):
    rows, d = x.shape
    w_in, w_l = wts["w_in"]
    cols = w_in.shape[2]
    tm, tn = _tile(rows, PROJ_TM, 2 * PACKED_ROWS), width
    gate_blk0 = (cols - N_BRANCH * d) // tn
    assert tm % (2 * PACKED_ROWS) == 0 and gate_blk0 > 1
    grid = (rows // tm, cols // tn)
    kern = functools.partial(_proj_kernel, gelu_blk=1, gate_blk0=gate_blk0)
    cast_specs = [_cast_specs(l, w.shape[1:], grid, 0) for w in cast]
    return pl.pallas_call(
        kern,
        grid=grid,
        in_specs=[
            pl.BlockSpec((tm, d), lambda i, j: (i, 0)),
            _layer_spec(l, (1, d)),
            pl.BlockSpec((None, d, tn), lambda i, j: (w_l, 0, j)),
            pl.BlockSpec((None, 1, tn), lambda i, j: (l, 0, jnp.maximum(j - gate_blk0, 0))),
        ] + [cs[0] for cs in cast_specs],
        out_specs=[pl.BlockSpec((tm, tn), lambda i, j: (i, j))] + [cs[1] for cs in cast_specs],
        out_shape=[jax.ShapeDtypeStruct((rows, cols), BF16)]
        + [jax.ShapeDtypeStruct((1,) + w.shape[1:], BF16) for w in cast],
        scratch_shapes=[pltpu.VMEM((tm, d), BF16)],
        compiler_params=_params(("arbitrary", "arbitrary")),
        name="proj",
    )(x, wts["g_mix_pre"], w_in, wts["b_gate"], *cast)


def _memkv_kernel(x_ref, g_ref, w_ref, o_ref, xn_ref):
    @pl.when(pl.program_id(1) == 0)
    def _():
        xn_ref[...] = _rms(x_ref[...], g_ref[...]).astype(BF16)

    o_ref[...] = jnp.dot(xn_ref[...], w_ref[...].astype(BF16), preferred_element_type=F32)


def _memkv(mem, wts, l, *, width):
    rows, d = mem.shape
    cols = wts["w_mem_kv"].shape[2]
    tm, tn = _tile(rows, PROJ_TM), width
    return pl.pallas_call(
        _memkv_kernel,
        grid=(rows // tm, cols // tn),
        in_specs=[
            pl.BlockSpec((tm, d), lambda i, j: (i, 0)),
            _layer_spec(l, (1, d)),
            pl.BlockSpec((None, d, tn), lambda i, j: (l, 0, j)),
        ],
        out_specs=pl.BlockSpec((None, tm, tn), lambda i, j: (j, i, 0)),
        out_shape=jax.ShapeDtypeStruct((cols // tn, rows, tn), F32),
        scratch_shapes=[pltpu.VMEM((tm, d), BF16)],
        compiler_params=_params(("parallel", "arbitrary")),
        name="memkv",
    )(mem, wts["g_mem"], wts["w_mem_kv"])


def _softplus(x):
    return jnp.maximum(x, 0.0) + jnp.log1p(jnp.exp(-jnp.abs(x)))


def _causal_conv(pad_ref, w_ref, rows):
    taps = w_ref.shape[0]
    xe = pad_ref[...]
    y = None
    for k in range(taps):
        back = taps - 1 - k
        xk = xe if back == 0 else pltpu.roll(xe, back, axis=0)
        term = xk[PAD_ROWS:PAD_ROWS + rows, :] * w_ref[k:k + 1, :]
        y = term if y is None else y + term
    return y


def _mix_init(h0_ref, lb0_ref, sb0_ref, xpad, cpad, hcar):
    width = xpad.shape[1]
    xpad[0:PAD_ROWS, :] = jnp.zeros((PAD_ROWS, width), F32)
    cpad[0:PAD_ROWS, :] = jnp.zeros((PAD_ROWS, width), F32)
    xpad[PAD_ROWS - lb0_ref.shape[1]:PAD_ROWS, :] = lb0_ref[0]
    cpad[PAD_ROWS - sb0_ref.shape[1]:PAD_ROWS, :] = sb0_ref[0]
    hcar[...] = h0_ref[0]


def _mix_state_out(hn_ref, lbn_ref, sbn_ref, xpad, cpad, hcar):
    hn_ref[0] = hcar[...]
    lbn_ref[0] = xpad[PAD_ROWS - lbn_ref.shape[1]:PAD_ROWS, :]
    sbn_ref[0] = cpad[PAD_ROWS - sbn_ref.shape[1]:PAD_ROWS, :]


def _mix_lru_conv(lx_ref, cw_ref, cb_ref, xpad, uscr):
    rows = lx_ref.shape[0]
    kl = cw_ref.shape[0]
    xpad[PAD_ROWS:PAD_ROWS + rows, :] = lx_ref[...].astype(F32)
    uscr[...] = _causal_conv(xpad, cw_ref, rows) + cb_ref[...]
    xpad[PAD_ROWS - (kl - 1):PAD_ROWS, :] = xpad[PAD_ROWS + rows - (kl - 1):PAD_ROWS + rows, :]


def _mix_gate_dots(uscr, wa_ref, ba_ref, wi_ref, bi_ref, acum, bcum):
    n_chunks = wa_ref.shape[0]
    bd = uscr.shape[1] // n_chunks
    for c in range(n_chunks):
        sl = slice(c * bd, (c + 1) * bd)
        ub = uscr[:, sl].astype(BF16)
        acum[:, sl] = jnp.dot(ub, wa_ref[c], preferred_element_type=F32) + ba_ref[:, sl]
        bcum[:, sl] = jnp.dot(ub, wi_ref[c], preferred_element_type=F32) + bi_ref[:, sl]


def _mix_sconv(sb_ref, sc_ref, sh_ref, scw_ref, zc_ref, cpad):
    rows = sc_ref.shape[0]
    ks = scw_ref.shape[0]
    cpad[PAD_ROWS:PAD_ROWS + rows, :] = sc_ref[...].astype(F32) * sh_ref[...].astype(F32)
    cv = _causal_conv(cpad, scw_ref, rows)
    zc_ref[...] = (sb_ref[...].astype(F32) * cv).astype(zc_ref.dtype)
    cpad[PAD_ROWS - (ks - 1):PAD_ROWS, :] = cpad[PAD_ROWS + rows - (ks - 1):PAD_ROWS + rows, :]


def _mix_scan_chunk(c, n_chunks, lam_ref, uscr, ya_s, yi_s, acum, bcum):
    rows, width = uscr.shape
    bd = width // n_chunks
    sl = slice(c * bd, (c + 1) * bd)
    half_c = (-0.5 * LRU_C) * _softplus(-lam_ref[:, sl])
    tr = _tile(rows, SCAN_ROWS)
    groups = tr // SUBLANES
    row = lax.broadcasted_iota(jnp.int32, (groups, SUBLANES, bd), 1)
    for r0 in range(0, rows, tr):
        rs = slice(r0, r0 + tr)
        ta = jnp.tanh(0.5 * ya_s[rs, sl])
        ti = jnp.tanh(0.5 * yi_s[rs, sl])
        log_a = half_c * ta + half_c
        a = jnp.exp(log_a)
        th = jnp.tanh(log_a)
        u = -0.5 * th
        v = 1.0 - th
        half_mult = jnp.where(u > 0.0, u * lax.rsqrt(u * v), 0.0)
        b = half_mult * ((ti + 1.0) * uscr[rs, sl])
        a3 = a.reshape(groups, SUBLANES, bd)
        b3 = b.reshape(groups, SUBLANES, bd)
        d = 1
        while d < SUBLANES:
            a_prev = pltpu.roll(a3, d, axis=1)
            b_prev = pltpu.roll(b3, d, axis=1)
            take = row >= d
            b3 = jnp.where(take, a3 * b_prev + b3, b3)
            a3 = jnp.where(take, a3 * a_prev, a3)
            d *= 2
        acum[rs, sl] = a3.reshape(tr, bd)
        bcum[rs, sl] = b3.reshape(tr, bd)


def _mix_carry(lo, hi, gg_ref, zl_ref, acum, bcum, hcar, *, unroll):
    def body(g, h):
        r0 = pl.multiple_of(g * PACKED_ROWS, PACKED_ROWS)
        r1 = pl.multiple_of(r0 + SUBLANES, SUBLANES)
        h1 = acum[pl.ds(r0, SUBLANES), :] * h + bcum[pl.ds(r0, SUBLANES), :]
        h2 = acum[pl.ds(r1, SUBLANES), :] * h1[SUBLANES - 1:SUBLANES, :] + bcum[pl.ds(r1, SUBLANES), :]
        hh = jnp.concatenate([h1, h2], axis=0)
        gate = gg_ref[pl.ds(r0, PACKED_ROWS), :].astype(F32)
        zl_ref[pl.ds(r0, PACKED_ROWS), :] = (hh * gate).astype(zl_ref.dtype)
        return h2[SUBLANES - 1:SUBLANES, :]

    hcar[...] = lax.fori_loop(lo, hi, body, hcar[...], unroll=unroll)


def _mix_kernel(lx_ref, gg_ref, sb_ref, sc_ref, sh_ref, h0_ref, lb0_ref, sb0_ref,
                cw_ref, cb_ref, wa_ref, ba_ref, wi_ref, bi_ref, lam_ref, scw_ref,
                zl_ref, zc_ref, hn_ref, lbn_ref, sbn_ref,
                xpad, cpad, acum, bcum, hcar, uscr):
    @pl.when(pl.program_id(1) == 0)
    def _():
        _mix_init(h0_ref, lb0_ref, sb0_ref, xpad, cpad, hcar)

    _mix_lru_conv(lx_ref, cw_ref, cb_ref, xpad, uscr)
    _mix_gate_dots(uscr, wa_ref, ba_ref, wi_ref, bi_ref, acum, bcum)
    _mix_sconv(sb_ref, sc_ref, sh_ref, scw_ref, zc_ref, cpad)
    n_chunks = wa_ref.shape[0]
    for c in range(n_chunks):
        _mix_scan_chunk(c, n_chunks, lam_ref, uscr, acum, bcum, acum, bcum)
    _mix_carry(0, lx_ref.shape[0] // PACKED_ROWS, gg_ref, zl_ref, acum, bcum, hcar, unroll=False)
    _mix_state_out(hn_ref, lbn_ref, sbn_ref, xpad, cpad, hcar)


def _mix_scratch(rows, width):
    return [pltpu.VMEM((PAD_ROWS + rows, width), F32),
            pltpu.VMEM((PAD_ROWS + rows, width), F32),
            pltpu.VMEM((rows, width), F32),
            pltpu.VMEM((rows, width), F32),
            pltpu.VMEM((1, width), F32),
            pltpu.VMEM((rows, width), F32)]


def _mix_param_specs(wts, l, width):
    kl, ks = wts["lru_conv_w"].shape[1], wts["sconv_w"].shape[1]
    nck = wts["wa_bd"].shape[1]
    bd_spec = _layer_spec(l, (nck, width // nck, width // nck))
    return [_layer_spec(l, (kl, width)), _layer_spec(l, (1, width)),
            bd_spec, _layer_spec(l, (1, width)), bd_spec, _layer_spec(l, (1, width)),
            _layer_spec(l, (1, width)), _layer_spec(l, (ks, width))]


def _mix_param_args(wts):
    return (wts["lru_conv_w"], wts["lru_conv_b"], wts["wa_bd"], wts["lru_b_a"], wts["wi_bd"], wts["lru_b_i"],
            wts["lru_lambda"], wts["sconv_w"])


def _mix(proj, h0, lb0, sb0, state_off, wts, l, *, n_streams, seq, width):
    rows = _tile(seq, MIX_ROWS, PACKED_ROWS)
    nt = seq // rows
    kl, ks = wts["lru_conv_w"].shape[1], wts["sconv_w"].shape[1]

    def col(c):
        return pl.BlockSpec((rows, width), lambda s, t: (s * nt + t, c))

    def state_in(k):
        return pl.BlockSpec((1, k, width), lambda s, t: (state_off + s, 0, 0))

    def state_out(k):
        return pl.BlockSpec((1, k, width), lambda s, t: (s, 0, 0))

    total = n_streams * seq
    return pl.pallas_call(
        _mix_kernel,
        grid=(n_streams, nt),
        in_specs=[col(0), col(1), col(2), col(3), col(4),
                  state_in(1), state_in(kl - 1), state_in(ks - 1)] + _mix_param_specs(wts, l, width),
        out_specs=[pl.BlockSpec((rows, width), lambda s, t: (s * nt + t, 0)),
                   pl.BlockSpec((rows, width), lambda s, t: (s * nt + t, 0)),
                   state_out(1), state_out(kl - 1), state_out(ks - 1)],
        out_shape=[jax.ShapeDtypeStruct((total, width), BF16),
                   jax.ShapeDtypeStruct((total, width), BF16),
                   jax.ShapeDtypeStruct((n_streams, 1, width), F32),
                   jax.ShapeDtypeStruct((n_streams, kl - 1, width), F32),
                   jax.ShapeDtypeStruct((n_streams, ks - 1, width), F32)],
        scratch_shapes=_mix_scratch(rows, width),
        compiler_params=_params(("parallel", "arbitrary")),
        name="mix",
    )(proj, proj, proj, proj, proj, h0, lb0, sb0, *_mix_param_args(wts))


def _attn_kernel(q_ref, k_ref, v_ref, o_ref, *, heads, seq_rows):
    dh = q_ref.shape[1] // heads
    scale = dh ** -0.5
    if len(k_ref.shape) == 4:
        mh = k_ref.shape[1] * heads
        shape = (heads * seq_rows, mh)
        row = lax.broadcasted_iota(jnp.int32, shape, 0)
        row_head = sum((row >= h * seq_rows).astype(jnp.int32) for h in range(1, heads))
        own = row_head == lax.broadcasted_iota(jnp.int32, shape, 1) % heads
        for g in range(k_ref.shape[0]):
            rs = slice(g * seq_rows, (g + 1) * seq_rows)
            kf = k_ref[g].reshape(mh, dh).astype(BF16)
            vf = v_ref[g].reshape(mh, dh).astype(BF16)
            qs = jnp.concatenate([q_ref[rs, h * dh:(h + 1) * dh] for h in range(heads)], axis=0)
            s = lax.dot_general(qs, kf, (((1,), (1,)), ((), ())), preferred_element_type=F32) * scale
            s = jnp.where(own, s, -jnp.inf)
            e = jnp.where(own, jnp.exp(s - jnp.max(s, axis=-1, keepdims=True)), 0.0)
            p = e * (1.0 / jnp.sum(e, axis=-1, keepdims=True))
            o = jnp.dot(p.astype(BF16), vf, preferred_element_type=F32).astype(o_ref.dtype)
            for h in range(heads):
                o_ref[rs, h * dh:(h + 1) * dh] = o[h * seq_rows:(h + 1) * seq_rows, :]
        return
    for g in range(k_ref.shape[0]):
        rs = slice(g * seq_rows, (g + 1) * seq_rows)
        for h in range(heads):
            sl = slice(h * dh, (h + 1) * dh)
            kh = k_ref[g, :, sl].astype(BF16)
            vh = v_ref[g, :, sl].astype(BF16)
            s = lax.dot_general(q_ref[rs, sl], kh, (((1,), (1,)), ((), ())),
                                preferred_element_type=F32) * scale
            e = jnp.exp(s - jnp.max(s, axis=-1, keepdims=True))
            p = e * (1.0 / jnp.sum(e, axis=-1, keepdims=True))
            o_ref[rs, sl] = jnp.dot(p.astype(BF16), vh, preferred_element_type=F32).astype(o_ref.dtype)


def _attn(proj, mem_k, mem_v, kv_off, *, n_streams, seq, width, q_blk, heads):
    mlen = mem_k.shape[1]
    rows = _tile(seq, ATTN_ROWS)
    nt = seq // rows
    per_step = 1
    if nt == 1:
        per_step = _tile(n_streams, max(1, ATTN_KV_BYTES // (mlen * width * 4)), 1)
    assert kv_off % per_step == 0
    blk = per_step * rows
    kv_shape = (per_step,) + mem_k.shape[1:]
    kv_spec = pl.BlockSpec(kv_shape, lambda s, t: (kv_off // per_step + s,) + (0,) * (len(kv_shape) - 1))
    return pl.pallas_call(
        functools.partial(_attn_kernel, heads=heads, seq_rows=rows),
        grid=(n_streams // per_step, nt),
        in_specs=[pl.BlockSpec((blk, width), lambda s, t: (s * nt + t, q_blk)), kv_spec, kv_spec],
        out_specs=pl.BlockSpec((blk, width), lambda s, t: (s * nt + t, 0)),
        out_shape=jax.ShapeDtypeStruct((n_streams * seq, width), BF16),
        compiler_params=_params(("parallel", "arbitrary")),
        name="attn",
    )(proj, mem_k, mem_v)


def _merge_m_tile(cs, zl_ref, zc_ref, zm_ref, g0_ref, g1_ref, g2_ref, wl_ref, wc_ref, wm_ref):
    m = g0_ref[:, cs].astype(F32) * jnp.dot(zl_ref[...], wl_ref[:, cs], preferred_element_type=F32)
    m = m + g1_ref[:, cs].astype(F32) * jnp.dot(zc_ref[...], wc_ref[:, cs], preferred_element_type=F32)
    m = m + g2_ref[:, cs].astype(F32) * jnp.dot(zm_ref[...], wm_ref[:, cs], preferred_element_type=F32)
    return m.astype(BF16)


def _merge_kernel(zl_ref, zc_ref, zm_ref, g0_ref, g1_ref, g2_ref, x_ref,
                  wl_ref, wc_ref, wm_ref, wo_ref, gp_ref, o_ref):
    m = _merge_m_tile(slice(None), zl_ref, zc_ref, zm_ref, g0_ref, g1_ref, g2_ref, wl_ref, wc_ref, wm_ref)
    y = jnp.dot(m, wo_ref[...], preferred_element_type=F32)
    o_ref[...] = x_ref[...] + _rms(y, gp_ref[...])


MERGE_WEIGHTS = ("w_branch_lru", "w_branch_conv", "w_branch_mem", "w_out")


def _merge_weight_specs(wts, l, d):
    once = pl.Buffered(1)
    specs = [_layer_spec(wts[k][1], wts[k][0].shape[1:], once) for k in MERGE_WEIGHTS]
    return specs + [_layer_spec(l, (1, d), once)]


def _merge_weight_args(wts):
    return tuple(wts[k][0] for k in MERGE_WEIGHTS) + (wts["g_mix_post"],)


def _merge(zl, zc, zm, proj, x, wts, l, *, width):
    rows, d = x.shape
    tm = _tile(rows, MERGE_TM)
    gate_blk0 = (proj.shape[1] - N_BRANCH * d) // d

    def act(w):
        return pl.BlockSpec((tm, w), lambda i: (i, 0))

    def gate(b):
        return pl.BlockSpec((tm, d), lambda i: (i, gate_blk0 + b))

    return pl.pallas_call(
        _merge_kernel,
        grid=(rows // tm,),
        in_specs=[act(width), act(width), act(width), gate(0), gate(1), gate(2), act(d)]
        + _merge_weight_specs(wts, l, d),
        out_specs=act(d),
        out_shape=jax.ShapeDtypeStruct((rows, d), F32),
        compiler_params=_params(("parallel",)),
        name="merge",
    )(zl, zc, zm, proj, proj, proj, x, *_merge_weight_args(wts))


def _mixmerge_kernel(lx_ref, gg_ref, sb_ref, sc_ref, sh_ref, h0_ref, lb0_ref, sb0_ref,
                     cw_ref, cb_ref, wa_ref, ba_ref, wi_ref, bi_ref, lam_ref, scw_ref,
                     zm_ref, g0_ref, g1_ref, g2_ref, x_ref, wl_ref, wc_ref, wm_ref, wo_ref, gp_ref,
                     o_ref, hn_ref, lbn_ref, sbn_ref,
                     xpad, cpad, acum, bcum, hcar, uscr, ya_s, yi_s, zl_s, zc_s, m_f, m_s, y_s,
                     *, blocks_per_stream, n_blocks):
    g = pl.program_id(0)
    kl = cw_ref.shape[0]

    @pl.when(g == 0)
    def _():
        for ref in (cpad, hcar, uscr, ya_s, yi_s):
            ref[...] = jnp.zeros(ref.shape, ref.dtype)

    @pl.when(jnp.logical_and(g % blocks_per_stream == 0, g < n_blocks))
    def _():
        xpad[0:PAD_ROWS, :] = jnp.zeros((PAD_ROWS, xpad.shape[1]), F32)
        xpad[PAD_ROWS - (kl - 1):PAD_ROWS, :] = lb0_ref[0]

    scan_valid = g >= 1

    @pl.when(jnp.logical_and((g - 1) % blocks_per_stream == 0, scan_valid))
    def _():
        cpad[0:PAD_ROWS, :] = jnp.zeros((PAD_ROWS, cpad.shape[1]), F32)
        cpad[PAD_ROWS - sb0_ref.shape[1]:PAD_ROWS, :] = sb0_ref[0]
        hcar[...] = h0_ref[0]

    rows, d = x_ref.shape
    n_chunks = wa_ref.shape[0]
    col_tiles = [slice(j, j + MERGE_TN) for j in range(0, d, MERGE_TN)]

    def gated(gate_ref, z_ref, w_ref, cs):
        return gate_ref[:, cs].astype(F32) * jnp.dot(z_ref[...], w_ref[:, cs], preferred_element_type=F32)

    tiles = rows // PACKED_ROWS
    vector_pieces = [functools.partial(_mix_scan_chunk, c, n_chunks, lam_ref, uscr, ya_s, yi_s, acum, bcum)
                     for c in range(n_chunks)]
    vector_pieces += [functools.partial(_mix_carry, k * tiles // 2, (k + 1) * tiles // 2,
                                        gg_ref, zl_s, acum, bcum, hcar, unroll=True) for k in range(2)]
    _mix_sconv(sb_ref, sc_ref, sh_ref, scw_ref, zc_s, cpad)
    for cs in col_tiles:
        m_f[:, cs] = gated(g2_ref, zm_ref, wm_ref, cs)
        if vector_pieces:
            vector_pieces.pop(0)()
    for cs in col_tiles:
        m_f[:, cs] += gated(g1_ref, zc_s, wc_ref, cs)
        if vector_pieces:
            vector_pieces.pop(0)()
    while vector_pieces:
        vector_pieces.pop(0)()
    for cs in col_tiles:
        m_s[:, cs] = (m_f[:, cs] + gated(g0_ref, zl_s, wl_ref, cs)).astype(BF16)
    for k, cs in enumerate(col_tiles):
        y_s[:, cs] = jnp.dot(m_s[...], wo_ref[:, cs], preferred_element_type=F32)
        if k == 0:
            _mix_lru_conv(lx_ref, cw_ref, cb_ref, xpad, uscr)
    _mix_gate_dots(uscr, wa_ref, ba_ref, wi_ref, bi_ref, ya_s, yi_s)
    o_ref[...] = x_ref[...] + _rms(y_s[...], gp_ref[...])

    @pl.when(g < n_blocks)
    def _():
        lbn_ref[0] = xpad[PAD_ROWS - (kl - 1):PAD_ROWS, :]

    @pl.when(scan_valid)
    def _():
        hn_ref[0] = hcar[...]
        sbn_ref[0] = cpad[PAD_ROWS - sbn_ref.shape[1]:PAD_ROWS, :]


def _mixmerge(proj, zm, x, h0, lb0, sb0, state_off, wts, l, *, n_streams, seq, width):
    rows, d = x.shape
    tm = _tile(seq, MERGE_TM, PACKED_ROWS)
    nt = seq // tm
    n = n_streams * nt
    kl, ks = wts["lru_conv_w"].shape[1], wts["sconv_w"].shape[1]
    gate_blk0 = (proj.shape[1] - N_BRANCH * d) // d

    def blk(lag):
        return lambda g: jnp.clip(g - lag, 0, n - 1)

    conv_blk, scan_blk, merge_blk = blk(0), blk(1), blk(1)

    def col(c, which):
        return pl.BlockSpec((tm, width), lambda g: (which(g), c))

    def state_in(k, which):
        return pl.BlockSpec((1, k, width), lambda g: (state_off + which(g) // nt, 0, 0))

    def state_out(k, which):
        return pl.BlockSpec((1, k, width), lambda g: (which(g) // nt, 0, 0))

    def gate(b):
        return pl.BlockSpec((tm, d), lambda g: (merge_blk(g), gate_blk0 + b))

    def act(w):
        return pl.BlockSpec((tm, w), lambda g: (merge_blk(g), 0))

    kern = functools.partial(_mixmerge_kernel, blocks_per_stream=nt, n_blocks=n)
    act_bf16 = pltpu.VMEM((tm, width), BF16)
    act_f32 = pltpu.VMEM((tm, width), F32)
    return pl.pallas_call(
        kern,
        grid=(n + 1,),
        in_specs=[col(0, conv_blk), col(1, scan_blk), col(2, scan_blk), col(3, scan_blk), col(4, scan_blk),
                  state_in(1, scan_blk), state_in(kl - 1, conv_blk), state_in(ks - 1, scan_blk)]
        + _mix_param_specs(wts, l, width)
        + [act(width), gate(0), gate(1), gate(2), act(d)] + _merge_weight_specs(wts, l, d),
        out_specs=[act(d), state_out(1, scan_blk), state_out(kl - 1, conv_blk), state_out(ks - 1, scan_blk)],
        out_shape=[jax.ShapeDtypeStruct((rows, d), F32),
                   jax.ShapeDtypeStruct((n_streams, 1, width), F32),
                   jax.ShapeDtypeStruct((n_streams, kl - 1, width), F32),
                   jax.ShapeDtypeStruct((n_streams, ks - 1, width), F32)],
        scratch_shapes=_mix_scratch(tm, width) + [act_f32, act_f32, act_bf16, act_bf16, pltpu.VMEM((tm, d), F32),
                                                  pltpu.VMEM((tm, d), BF16), pltpu.VMEM((tm, d), F32)],
        compiler_params=_params(("arbitrary",)),
        name="mixmerge",
    )(proj, proj, proj, proj, proj, h0, lb0, sb0, *_mix_param_args(wts),
      zm, proj, proj, proj, x, *_merge_weight_args(wts))


def _mlp_body(idx, x_ref, gpre_ref, wu_ref, wd_ref, gpost_ref, *rest, cast_next, n_f, xn_ref, acc_ref):
    if cast_next:
        wnext_ref, o_ref, wnext_out_ref = rest
    else:
        o_ref, = rest
    f = (idx if isinstance(idx, tuple) else idx.index)[1]
    last = n_f - 1
    tm = x_ref.shape[0]

    def partial_down(rows):
        hid = jnp.dot(xn_ref[rows, :], wu_ref[...], preferred_element_type=F32)
        hid = jnp.square(jnp.maximum(hid, 0.0)).astype(BF16)
        return jnp.dot(hid, wd_ref[...], preferred_element_type=F32)

    def side_cast():
        if cast_next:
            wnext_out_ref[...] = wnext_ref[...].astype(wnext_out_ref.dtype)

    @pl.when(f == 0)
    def _():
        side_cast()
        xn_ref[...] = _rms(x_ref[...], gpre_ref[...]).astype(BF16)
        acc_ref[...] = partial_down(slice(None))

    @pl.when(jnp.logical_and(f > 0, f < last))
    def _():
        side_cast()
        acc_ref[...] += partial_down(slice(None))

    @pl.when(f == last)
    def _():
        side_cast()
        for half in (slice(0, tm // 2), slice(tm // 2, tm)):
            y = acc_ref[half, :] + partial_down(half)
            o_ref[half, :] = x_ref[half, :] + _rms(y, gpost_ref[...])


def _mlp_kernel(*refs, grid, in_specs, out_specs, cast_next):
    n_in, n_out = len(in_specs), len(out_specs)
    xn_ref, acc_ref = refs[n_in + n_out:]
    body = functools.partial(_mlp_body, cast_next=cast_next, n_f=grid[1], xn_ref=xn_ref, acc_ref=acc_ref)
    pltpu.emit_pipeline(body, grid=grid, in_specs=in_specs, out_specs=out_specs,
                        _explicit_indices=True)(*refs[:n_in + n_out])


def _mlp(x, wts, l, *, w_next=None):
    rows, d = x.shape
    (w_up, up_l), (w_down, down_l) = wts["w_up"], wts["w_down"]
    dff = w_up.shape[2]
    tm, tf = _tile(rows, MLP_TM, 2 * SUBLANES), _tile(dff, MLP_TF, LANES)
    assert dff // tf >= 2
    grid = (rows // tm, dff // tf)
    row_blk = pl.BlockSpec((tm, d), lambda i, f: (i, 0))
    in_specs = [pl.BlockSpec((tm, d), lambda i, f: (i, 0), pipeline_mode=pl.Buffered(2, use_lookahead=True)),
                _layer_spec(l, (1, d)),
                pl.BlockSpec((None, d, tf), lambda i, f: (up_l, 0, f)),
                pl.BlockSpec((None, tf, d), lambda i, f: (down_l, f, 0)),
                _layer_spec(l, (1, d))]
    args = [x, wts["g_mlp_pre"], w_up, w_down, wts["g_mlp_post"]]
    out_specs, out_shape = [row_blk], [jax.ShapeDtypeStruct((rows, d), F32)]
    cast_next = w_next is not None
    if cast_next:
        cast_in, cast_out = _cast_specs(l + 1, w_next.shape[1:], grid, 1)
        in_specs.append(cast_in)
        args.append(w_next)
        out_specs.append(cast_out)
        out_shape.append(jax.ShapeDtypeStruct((1,) + w_next.shape[1:], BF16))
    in_hbm = pl.BlockSpec(memory_space=pl.ANY)
    res = pl.pallas_call(
        functools.partial(_mlp_kernel, grid=grid, in_specs=in_specs, out_specs=out_specs, cast_next=cast_next),
        in_specs=[in_hbm] * len(in_specs),
        out_specs=[in_hbm] * len(out_specs),
        out_shape=out_shape,
        scratch_shapes=[pltpu.VMEM((tm, d), BF16), pltpu.VMEM((tm, d), F32)],
        compiler_params=pltpu.CompilerParams(vmem_limit_bytes=VMEM_LIMIT_BYTES),
        name="mlp",
    )(*args)
    return (res[0], res[1]) if cast_next else (res[0], None)


def _block_diag_pack(w):
    depth, heads, dh, _ = w.shape
    per = max(1, min(heads, MXU_DIM // dh))
    n = heads // per
    wg = w.reshape(depth, n, per, dh, dh)
    eye = jnp.eye(per, dtype=w.dtype)
    out = jnp.einsum("lnhij,hg->lnhigj", wg, eye)
    return out.reshape(depth, n, per * dh, per * dh).astype(BF16)


def _run_layer(x, mem_k, mem_v, kv_off, h0, lb0, sb0, state_off, wts, l,
               *, n_streams, seq, width, heads, last, cast=None):
    if cast is not None:
        names = ("w_up", "w_down") + MERGE_WEIGHTS
        proj, *casts = _proj(x, wts, l, width=width, cast=tuple(cast[k] for k in names))
        wts = dict(wts, **{k: (w, 0) for k, w in zip(names, casts)})
    else:
        proj, = _proj(x, wts, l, width=width)
    zm = _attn(proj, mem_k, mem_v, kv_off, n_streams=n_streams, seq=seq, width=width, q_blk=5, heads=heads)
    if seq >= MERGE_TM and seq % MERGE_TM == 0:
        x, hn, lbn, sbn = _mixmerge(proj, zm, x, h0, lb0, sb0, state_off, wts, l,
                                    n_streams=n_streams, seq=seq, width=width)
    else:
        zl, zc, hn, lbn, sbn = _mix(proj, h0, lb0, sb0, state_off, wts, l,
                                    n_streams=n_streams, seq=seq, width=width)
        x = _merge(zl, zc, zm, proj, x, wts, l, width=width)
    w_next = cast["w_in"] if cast is not None and not last else None
    x, w_in_next = _mlp(x, wts, l, w_next=w_next)
    return x, hn, lbn, sbn, wts, w_in_next


def kernel(x_prompt, x_sample, mem_prompt, state_lru_h, state_lru_conv, state_sconv, cache_mem_k, cache_mem_v, g_mix_pre, w_in, b_gate, lru_conv_w, lru_conv_b, lru_w_a, lru_b_a, lru_w_i, lru_b_i, lru_lambda, w_branch_lru, sconv_w, w_branch_conv, g_mem, w_mem_kv, w_branch_mem, w_out, g_mix_post, g_mlp_pre, w_up, w_down, g_mlp_post):
    depth = w_in.shape[0]
    bp, sp, d = x_prompt.shape
    bs, ss, _ = x_sample.shape
    width = lru_lambda.shape[1]
    mlen, heads, dh = cache_mem_k.shape[2:]
    kl, ks = lru_conv_w.shape[1], sconv_w.shape[1]

    def rowvec(p):
        return p.reshape(depth, 1, -1)

    wts = dict(
        g_mix_pre=rowvec(g_mix_pre), w_in=(w_in[0:1].astype(BF16), 0), b_gate=rowvec(b_gate),
        lru_conv_w=lru_conv_w, lru_conv_b=rowvec(lru_conv_b),
        wa_bd=_block_diag_pack(lru_w_a), lru_b_a=rowvec(lru_b_a),
        wi_bd=_block_diag_pack(lru_w_i), lru_b_i=rowvec(lru_b_i),
        lru_lambda=rowvec(lru_lambda), sconv_w=sconv_w,
        g_mix_post=rowvec(g_mix_post), g_mlp_pre=rowvec(g_mlp_pre),
        g_mlp_post=rowvec(g_mlp_post),
        g_mem=rowvec(g_mem), w_mem_kv=w_mem_kv)

    hp = x_prompt.reshape(bp * sp, d)
    hs = x_sample.reshape(bs * ss, d)
    mem2d = mem_prompt.reshape(bp * mlen, d)
    p_h0 = jnp.zeros((bp, 1, width), F32)
    p_lb0 = jnp.zeros((bp, kl - 1, width), F32)
    p_sb0 = jnp.zeros((bp, ks - 1, width), F32)
    s_h0 = state_lru_h.reshape(depth * bs, 1, width)
    s_lb0 = state_lru_conv.reshape(depth * bs, kl - 1, width)
    s_sb0 = state_sconv.reshape(depth * bs, ks - 1, width)
    s_mk = cache_mem_k.reshape(depth * bs, mlen, heads, dh)
    s_mv = cache_mem_v.reshape(depth * bs, mlen, heads, dh)

    outs = {k: [] for k in ("p_h", "p_lb", "p_sb", "p_mk", "p_mv", "s_h", "s_lb", "s_sb")}
    f32_weights = dict(w_up=w_up, w_down=w_down, w_branch_lru=w_branch_lru, w_branch_conv=w_branch_conv,
                       w_branch_mem=w_branch_mem, w_out=w_out, w_in=w_in)
    for l in range(depth):
        last = l == depth - 1
        kv = _memkv(mem2d, wts, l, width=width)
        mk = kv[0].reshape(bp, mlen, width)
        mv = kv[1].reshape(bp, mlen, width)
        hp, hn, lbn, sbn, wts, w_in_next = _run_layer(
            hp, mk, mv, 0, p_h0, p_lb0, p_sb0, 0, wts, l,
            n_streams=bp, seq=sp, width=width, heads=heads, last=last, cast=f32_weights)
        outs["p_h"].append(hn.reshape(bp, width))
        outs["p_lb"].append(lbn)
        outs["p_sb"].append(sbn)
        outs["p_mk"].append(mk.reshape(bp, mlen, heads, dh))
        outs["p_mv"].append(mv.reshape(bp, mlen, heads, dh))
        hs, hn, lbn, sbn, _, _ = _run_layer(
            hs, s_mk, s_mv, l * bs, s_h0, s_lb0, s_sb0, l * bs, wts, l,
            n_streams=bs, seq=ss, width=width, heads=heads, last=last)
        outs["s_h"].append(hn.reshape(bs, width))
        outs["s_lb"].append(lbn)
        outs["s_sb"].append(sbn)
        if not last:
            wts = dict(wts, w_in=(w_in_next, 0))

    return (hp.reshape(bp, sp, d), hs.reshape(bs, ss, d),
            jnp.stack(outs["p_h"]), jnp.stack(outs["p_lb"]), jnp.stack(outs["p_sb"]),
            jnp.stack(outs["p_mk"]), jnp.stack(outs["p_mv"]),
            jnp.stack(outs["s_h"]), jnp.stack(outs["s_lb"]), jnp.stack(outs["s_sb"]))
```
